```python
import jax
import jax.numpy as jnp
from jax import lax
import numpy as np

D_MODEL = 1024
BATCH = 8
SEQ = 4096
DEPTH = 4

CTX_LEN = 256
GRID_W = 64
NORM_EPS = 1e-6
N_MOD = 6

MIX_WIDTH = D_MODEL
SC_WIDTH = MIX_WIDTH // 2
SC_CONV = 3
SC_PAD_LEFT = 1
GLA_HEADS = 4
GLA_VALUE = MIX_WIDTH // 2
GLA_DV = GLA_VALUE // GLA_HEADS
GLA_DK = GLA_DV // 2
GLA_KEY = GLA_HEADS * GLA_DK
GLA_RANK = 16
GLA_TAU = 16.0
GLA_CHUNK = 64
EVEN_SIZES = (SC_WIDTH, SC_WIDTH, SC_WIDTH, GLA_KEY, GLA_VALUE)
EVEN_KV_SIZES = (GLA_KEY, GLA_VALUE, GLA_RANK, GLA_RANK)
EVEN_KV_START = 3 * SC_WIDTH + GLA_KEY + GLA_VALUE
EVEN_IN = EVEN_KV_START + GLA_KEY + GLA_VALUE + 2 * GLA_RANK
EVEN_OUT = SC_WIDTH + GLA_VALUE
LRU_WIDTH = MIX_WIDTH // 2
LRU_HEADS = 4
LRU_HEAD_DIM = LRU_WIDTH // LRU_HEADS
LRU_CONV = 4
LRU_PAD_LEFT = 2
LRU_C = 8.0
SG_WIDTH = MIX_WIDTH // 2
SG_GROUPS = 4
SG_GROUP_DIM = SG_WIDTH // SG_GROUPS
SG_CHUNK = 128
ODD_SIZES = (LRU_WIDTH, LRU_WIDTH, SG_WIDTH, SG_WIDTH)
ODD_IN = 2 * LRU_WIDTH + 2 * SG_WIDTH
ODD_OUT = LRU_WIDTH + SG_WIDTH
N_EXPERTS = 16
EXPERT_HIDDEN = D_MODEL
EC_CAPACITY = 2

kernel_name = 'hybrid_diffusion_conv_gla_rglru_sgmlp_ec'


def _split(z, sizes):
    return jnp.split(z, [int(s) for s in np.cumsum(sizes)[:-1]], axis=-1)


def rmsnorm(x, g):
    xf = x.astype(jnp.float32)
    y = xf * lax.rsqrt(jnp.mean(xf * xf, axis=-1, keepdims=True) + NORM_EPS)
    return (y * g.astype(jnp.float32)).astype(x.dtype)


def layernorm(x, g):
    xf = x.astype(jnp.float32)
    xc = xf - jnp.mean(xf, axis=-1, keepdims=True)
    y = xc * lax.rsqrt(jnp.mean(xc * xc, axis=-1, keepdims=True) + NORM_EPS)
    return (y * g.astype(jnp.float32)).astype(x.dtype)


def modulate(h, shift, scale):
    return h * (1 + scale) + shift


def dwconv(x, w, b, pad_left):
    width = w.shape[0]
    length = x.shape[-2]
    pad = [(0, 0)] * (x.ndim - 2) + [(pad_left, width - 1 - pad_left), (0, 0)]
    xp = jnp.pad(x, pad)
    y = b
    for j in range(width):
        y = y + w[j] * xp[..., j:j + length, :]
    return y


def _heads(t, n):
    bsz, length, _ = t.shape
    return t.reshape(bsz, length, n, -1).transpose(0, 2, 1, 3)


def gla_chunk_scan(q, k, v, la, s0):
    bsz, heads, length, _ = k.shape
    n = length // GLA_CHUNK

    def blocks(t):
        return t.reshape(bsz, heads, n, GLA_CHUNK, t.shape[-1])

    k, v, la = blocks(k), blocks(v), blocks(la)
    b = jnp.cumsum(la, axis=3)
    b_last = b[:, :, :, -1:, :]
    ds = jnp.einsum('bhncd,bhnce->nbhde', k * jnp.exp(b_last - b), v)
    decay = jnp.moveaxis(jnp.exp(b_last[:, :, :, 0, :]), 2, 0)
    with_out = q is not None

    def step(s, inp):
        dec, d = inp
        return dec[..., None] * s + d, (s if with_out else None)

    s_last, s_in = lax.scan(step, s0, (decay, ds))
    if not with_out:
        return None, s_last
    q_d = blocks(q) * jnp.exp(b)
    att = jnp.einsum('bhnid,bhnjd->bhnij', q_d, k * jnp.exp(-b))
    att = jnp.where(jnp.tril(jnp.ones((GLA_CHUNK, GLA_CHUNK), bool)), att, 0.0)
    o = jnp.einsum('bhnij,bhnje->bhnie', att, v) + jnp.einsum('bhncd,nbhde->bhnce', q_d, s_in)
    return o.reshape(bsz, heads, length, v.shape[-1]), s_last


def gla_bidir(q, k, v, la_f, la_b, s0_f, s0_b):
    def flip(t):
        return None if t is None else jnp.flip(t, axis=2)

    o_f, s_f = gla_chunk_scan(q, k, v, la_f, s0_f)
    o_b, s_b = gla_chunk_scan(flip(q), flip(k), flip(v), flip(la_b), s0_b)
    o = None if q is None else o_f + jnp.flip(o_b, axis=2)
    return o, s_f, s_b


def gla_readout(o, g, norm_g):
    bsz, heads, length, dv = o.shape
    o = o * lax.rsqrt(jnp.mean(o * o, axis=-1, keepdims=True) + NORM_EPS)
    o = o.transpose(0, 2, 1, 3).reshape(bsz, length, heads * dv)
    return (o * norm_g.astype(jnp.float32) * jax.nn.silu(g.astype(jnp.float32))).astype(g.dtype)


def _lin_combine(earlier, later):
    a1, b1 = earlier
    a2, b2 = later
    return a1 * a2, a2 * b1 + b2


def linear_scan(a, b, h0, reverse):
    if reverse:
        a, b = jnp.flip(a, axis=1), jnp.flip(b, axis=1)
    b = b.at[:, 0].add(a[:, 0] * h0)
    _, h = lax.associative_scan(_lin_combine, (a, b), axis=1)
    h_last = h[:, -1]
    if reverse:
        h = jnp.flip(h, axis=1)
    return h, h_last


def _blockdiag(x, w, b):
    bsz, length, _ = x.shape
    y = jnp.einsum('bthi,hij->bthj', x.reshape(bsz, length, LRU_HEADS, LRU_HEAD_DIM), w.astype(jnp.float32))
    return y.reshape(bsz, length, LRU_WIDTH) + b.astype(jnp.float32)


def even_mixer(h_lat, h_ctx, w_in, conv_w, conv_b, decay_up, decay_b, norm_g, w_out, need_ctx_out):
    f32 = jnp.float32
    bsz, seq, _ = h_lat.shape
    rows = seq // GRID_W
    scale = GLA_DK ** -0.5

    def kv_decay(zkv):
        k, v, lr_f, lr_b = _split(zkv, EVEN_KV_SIZES)
        la_f = jax.nn.log_sigmoid(lr_f.astype(f32) @ decay_up[0].astype(f32) + decay_b[0].astype(f32)) / GLA_TAU
        la_b = jax.nn.log_sigmoid(lr_b.astype(f32) @ decay_up[1].astype(f32) + decay_b[1].astype(f32)) / GLA_TAU
        return (_heads(k.astype(f32), GLA_HEADS), _heads(v.astype(f32), GLA_HEADS),
                _heads(la_f, GLA_HEADS), _heads(la_b, GLA_HEADS))

    zero = jnp.zeros((bsz, GLA_HEADS, GLA_DK, GLA_DV), f32)
    if need_ctx_out:
        zc = h_ctx @ w_in
        bg_c, cg_c, xa_c, q_c, g_c = _split(zc[..., :EVEN_KV_START], EVEN_SIZES)
        kv_c = kv_decay(zc[..., EVEN_KV_START:])
        qh_c = _heads(q_c.astype(f32), GLA_HEADS) * scale
    else:
        kv_c = kv_decay(h_ctx @ w_in[:, EVEN_KV_START:])
        qh_c = None
    o_c, s_f, s_b = gla_bidir(qh_c, *kv_c, zero, zero)

    z = h_lat @ w_in
    bg, cg, xa, q, g = _split(z[..., :EVEN_KV_START], EVEN_SIZES)
    o, _, _ = gla_bidir(_heads(q.astype(f32), GLA_HEADS) * scale, *kv_decay(z[..., EVEN_KV_START:]), s_f, s_b)
    conv = dwconv((cg * xa).reshape(bsz, rows, GRID_W, SC_WIDTH), conv_w, conv_b, SC_PAD_LEFT)
    conv = conv.reshape(bsz, seq, SC_WIDTH)
    y_lat = jnp.concatenate([bg * conv, gla_readout(o, g, norm_g)], axis=-1) @ w_out
    y_ctx = None
    if need_ctx_out:
        conv_c = dwconv(cg_c * xa_c, conv_w, conv_b, SC_PAD_LEFT)
        y_ctx = jnp.concatenate([bg_c * conv_c, gla_readout(o_c, g_c, norm_g)], axis=-1) @ w_out
    return y_lat, y_ctx


def odd_mixer(h_lat, h_ctx, w_in, conv_w, conv_b, gate_a_w, gate_a_b, gate_x_w, gate_x_b, lam,
              sg_norm_g, sg_w, sg_b, w_out, need_ctx_out):
    f32 = jnp.float32
    bsz = h_lat.shape[0]

    def lru_coeffs(xc, d):
        xf = xc.astype(f32)
        r = jax.nn.sigmoid(_blockdiag(xf, gate_a_w[d], gate_a_b[d]))
        i = jax.nn.sigmoid(_blockdiag(xf, gate_x_w[d], gate_x_b[d]))
        log_a = -LRU_C * r * jax.nn.softplus(-lam[d].astype(f32))
        return jnp.exp(log_a), jnp.sqrt(-jnp.expm1(2.0 * log_a)) * (i * xf)

    def spatial_gate(u, v):
        b, length, _ = u.shape
        n = length // SG_CHUNK
        vb = layernorm(jax.nn.gelu(v), sg_norm_g).reshape(b, n, SG_CHUNK, SG_GROUPS, SG_GROUP_DIM)
        mixed = jnp.einsum('gij,bnjgc->bnigc', sg_w, vb) + sg_b.T[:, :, None]
        return jax.nn.gelu(u) * mixed.reshape(b, length, SG_WIDTH)

    if need_ctx_out:
        xb_c, gb_c, u_c, v_c = _split(h_ctx @ w_in, ODD_SIZES)
    else:
        xb_c = h_ctx @ w_in[:, :LRU_WIDTH]
    xc_c = dwconv(xb_c, conv_w, conv_b, LRU_PAD_LEFT)
    xb, gb, u, v = _split(h_lat @ w_in, ODD_SIZES)
    xc = dwconv(xb, conv_w, conv_b, LRU_PAD_LEFT)
    zero = jnp.zeros((bsz, LRU_WIDTH), f32)
    h_lat_dirs, h_ctx_dirs = [], []
    for d, reverse in enumerate((False, True)):
        h_c, h_c_last = linear_scan(*lru_coeffs(xc_c, d), zero, reverse)
        h_l, _ = linear_scan(*lru_coeffs(xc, d), h_c_last, reverse)
        h_lat_dirs.append(h_l)
        h_ctx_dirs.append(h_c)
    y_rec = (h_lat_dirs[0] + h_lat_dirs[1]).astype(gb.dtype) * jax.nn.gelu(gb)
    y_lat = jnp.concatenate([y_rec, spatial_gate(u, v)], axis=-1) @ w_out
    y_ctx = None
    if need_ctx_out:
        y_rec_c = (h_ctx_dirs[0] + h_ctx_dirs[1]).astype(gb_c.dtype) * jax.nn.gelu(gb_c)
        y_ctx = jnp.concatenate([y_rec_c, spatial_gate(u_c, v_c)], axis=-1) @ w_out
    return y_lat, y_ctx


def ec_moe(h, w_router, w_gate, w_up, w_down):
    bsz, length, dim = h.shape
    cap = EC_CAPACITY * length // N_EXPERTS
    aff = jax.nn.softmax((h @ w_router).astype(jnp.float32), axis=-1)
    g, idx = lax.top_k(jnp.swapaxes(aff, 1, 2), cap)
    xg = jax.vmap(lambda hb, ib: hb[ib])(h, idx)
    hid = jax.nn.silu(jnp.einsum('becd,edf->becf', xg, w_gate)) * jnp.einsum('becd,edf->becf', xg, w_up)
    out = jnp.einsum('becf,efd->becd', hid, w_down) * g.astype(h.dtype)[..., None]
    return jax.vmap(lambda ib, ob: jnp.zeros((length, dim), ob.dtype).at[ib.reshape(-1)].add(ob.reshape(-1, dim)))(idx, out)


def setup_inputs(seed: int = 0) -> dict:
    key = jax.random.key(seed)
    keys = [k for k in jax.random.split(key, 48)]

    def nrm(shape, scale):
        return jax.random.normal(keys.pop(), shape, jnp.float32) * scale

    d = D_MODEL
    ne, no = (DEPTH + 1) // 2, DEPTH // 2
    u = jax.random.uniform(keys.pop(), (no, 2, LRU_WIDTH), jnp.float32, 0.9, 0.999)
    a = u ** (1.0 / LRU_C)
    lam = jnp.log(a) - jnp.log1p(-a)
    return {
        'x': nrm((BATCH, SEQ, d), 1.0),
        'c': nrm((BATCH, d), 1.0),
        'ctx': nrm((BATCH, CTX_LEN, d), 1.0),
        'c_ctx': nrm((d,), 1.0),
        'ada_w': nrm((DEPTH, d, N_MOD * d), 0.5 * d ** -0.5),
        'ada_b': nrm((DEPTH, N_MOD * d), 0.02),
        'norm_mix_g': 1.0 + nrm((DEPTH, d), 0.02),
        'norm_ffn_g': 1.0 + nrm((DEPTH, d), 0.02),
        'norm_final_g': 1.0 + nrm((d,), 0.02),
        'ev_w_in': nrm((ne, d, EVEN_IN), d ** -0.5),
        'ev_conv_w': nrm((ne, SC_CONV, SC_WIDTH), SC_CONV ** -0.5),
        'ev_conv_b': nrm((ne, SC_WIDTH), 0.02),
        'ev_decay_up': nrm((ne, 2, GLA_RANK, GLA_KEY), GLA_RANK ** -0.5),
        'ev_decay_b': 1.0 + nrm((ne, 2, GLA_KEY), 0.5),
        'ev_norm_g': 1.0 + nrm((ne, GLA_VALUE), 0.02),
        'ev_w_out': nrm((ne, EVEN_OUT, d), EVEN_OUT ** -0.5),
        'od_w_in': nrm((no, d, ODD_IN), d ** -0.5),
        'od_conv_w': nrm((no, LRU_CONV, LRU_WIDTH), LRU_CONV ** -0.5),
        'od_conv_b': nrm((no, LRU_WIDTH), 0.02),
        'od_gate_a_w': nrm((no, 2, LRU_HEADS, LRU_HEAD_DIM, LRU_HEAD_DIM), LRU_HEAD_DIM ** -0.5),
        'od_gate_a_b': nrm((no, 2, LRU_WIDTH), 0.02),
        'od_gate_x_w': nrm((no, 2, LRU_HEADS, LRU_HEAD_DIM, LRU_HEAD_DIM), LRU_HEAD_DIM ** -0.5),
        'od_gate_x_b': nrm((no, 2, LRU_WIDTH), 0.02),
        'od_lambda': lam,
        'od_sg_norm_g': 1.0 + nrm((no, SG_WIDTH), 0.02),
        'od_sg_w': nrm((no, SG_GROUPS, SG_CHUNK, SG_CHUNK), SG_CHUNK ** -0.5),
        'od_sg_b': 1.0 + nrm((no, SG_GROUPS, SG_CHUNK), 0.1),
        'od_w_out': nrm((no, ODD_OUT, d), ODD_OUT ** -0.5),
        'moe_router': nrm((DEPTH, d, N_EXPERTS), d ** -0.5),
        'moe_w_gate': nrm((DEPTH, N_EXPERTS, d, EXPERT_HIDDEN), d ** -0.5),
        'moe_w_up': nrm((DEPTH, N_EXPERTS, d, EXPERT_HIDDEN), d ** -0.5),
        'moe_w_down': nrm((DEPTH, N_EXPERTS, EXPERT_HIDDEN, d), EXPERT_HIDDEN ** -0.5),
    }


def reference(x, c, ctx, c_ctx, ada_w, ada_b, norm_mix_g, norm_ffn_g, norm_final_g,
              ev_w_in, ev_conv_w, ev_conv_b, ev_decay_up, ev_decay_b, ev_norm_g, ev_w_out,
              od_w_in, od_conv_w, od_conv_b, od_gate_a_w, od_gate_a_b, od_gate_x_w, od_gate_x_b,
              od_lambda, od_sg_norm_g, od_sg_w, od_sg_b, od_w_out,
              moe_router, moe_w_gate, moe_w_up, moe_w_down):
    x_lat, x_ctx = x, ctx
    s_lat = jax.nn.silu(c)
    s_ctx = jax.nn.silu(c_ctx)
    for l in range(DEPTH):
        last = l == DEPTH - 1
        mod = jnp.split((s_lat @ ada_w[l] + ada_b[l])[:, None, :], N_MOD, axis=-1)
        mod_c = jnp.split(s_ctx @ ada_w[l] + ada_b[l], N_MOD, axis=-1)
        h_lat = modulate(rmsnorm(x_lat, norm_mix_g[l]), mod[0], mod[1])
        h_ctx = modulate(rmsnorm(x_ctx, norm_mix_g[l]), mod_c[0], mod_c[1])
        i = l // 2
        if l % 2 == 0:
            y_lat, y_ctx = even_mixer(h_lat, h_ctx, ev_w_in[i], ev_conv_w[i], ev_conv_b[i], ev_decay_up[i],
                                      ev_decay_b[i], ev_norm_g[i], ev_w_out[i], not last)
        else:
            y_lat, y_ctx = odd_mixer(h_lat, h_ctx, od_w_in[i], od_conv_w[i], od_conv_b[i], od_gate_a_w[i],
                                     od_gate_a_b[i], od_gate_x_w[i], od_gate_x_b[i], od_lambda[i],
                                     od_sg_norm_g[i], od_sg_w[i], od_sg_b[i], od_w_out[i], not last)
        x_lat = x_lat + mod[2] * y_lat
        h_lat = modulate(rmsnorm(x_lat, norm_ffn_g[l]), mod[3], mod[4])
        x_lat = x_lat + mod[5] * ec_moe(h_lat, moe_router[l], moe_w_gate[l], moe_w_up[l], moe_w_down[l])
        if not last:
            x_ctx = x_ctx + mod_c[2] * y_ctx
            h_ctx = modulate(rmsnorm(x_ctx, norm_ffn_g[l]), mod_c[3], mod_c[4])
            x_ctx = x_ctx + mod_c[5] * ec_moe(h_ctx, moe_router[l], moe_w_gate[l], moe_w_up[l], moe_w_down[l])
    return rmsnorm(x_lat, norm_final_g)
```

```python
import functools

import jax
import jax.numpy as jnp
import numpy as np
from jax import lax
from jax.experimental import pallas as pl
from jax.experimental.pallas import tpu as pltpu

_MXU = jnp.bfloat16
_F32 = jnp.float32
NORM_EPS = 1e-6
N_MOD = 6
GRID_W = 64
SC_PAD_LEFT = 1
GLA_HEADS = 4
GLA_RANK = 16
GLA_TAU = 16.0
GLA_CHUNK = 64
LRU_HEADS = 4
LRU_PAD_LEFT = 2
LRU_C = 8.0
SG_GROUPS = 4
SG_CHUNK = 128
EC_CAPACITY = 2
LANES = 128
VMEM_LIMIT = 48 * 1024 * 1024


def _cparams(*sem):
    return pltpu.CompilerParams(dimension_semantics=sem, vmem_limit_bytes=VMEM_LIMIT)


def _row_tile(t, cap=256):
    tm = min(t, cap)
    assert t % tm == 0
    return tm


def _norm_mod_matmul_kernel(x_ref, a_ref, s_ref, w_ref, o_ref):
    x = x_ref[...]
    ms = jnp.mean(x * x, axis=-1, keepdims=True)
    h = x * lax.rsqrt(ms + NORM_EPS) * a_ref[...] + s_ref[...]
    o_ref[...] = jnp.dot(h.astype(_MXU), w_ref[...], preferred_element_type=_F32)


def _norm_mod_matmul(x, a, s, w):
    bsz, t, d = x.shape
    n = w.shape[1]
    tm = _row_tile(t)
    return pl.pallas_call(
        _norm_mod_matmul_kernel,
        grid=(bsz, t // tm),
        in_specs=[
            pl.BlockSpec((None, tm, d), lambda b, i: (b, i, 0)),
            pl.BlockSpec((None, 1, d), lambda b, i: (b, 0, 0)),
            pl.BlockSpec((None, 1, d), lambda b, i: (b, 0, 0)),
            pl.BlockSpec((d, n), lambda b, i: (0, 0)),
        ],
        out_specs=pl.BlockSpec((None, tm, n), lambda b, i: (b, i, 0)),
        out_shape=jax.ShapeDtypeStruct((bsz, t, n), _F32),
        compiler_params=_cparams("parallel", "parallel"),
        name="norm_mod_matmul",
    )(x, a, s, w)


def _out_proj_kernel(y_ref, w_ref, x_ref, gate_ref, a_ref, s_ref, wr_ref, xo_ref, h_ref, lg_ref):
    y = jnp.dot(y_ref[...].astype(_MXU), w_ref[...], preferred_element_type=_F32)
    xn = x_ref[...] + gate_ref[...] * y
    xo_ref[...] = xn
    ms = jnp.mean(xn * xn, axis=-1, keepdims=True)
    h = (xn * lax.rsqrt(ms + NORM_EPS) * a_ref[...] + s_ref[...]).astype(_MXU)
    h_ref[...] = h
    lg_ref[...] = jnp.dot(h, wr_ref[...], preferred_element_type=_F32)


def _out_proj(y, w, x, gate, a, s, wr):
    bsz, t, k = y.shape
    d = w.shape[1]
    tm = _row_tile(t)
    vec = pl.BlockSpec((None, 1, d), lambda b, i: (b, 0, 0))
    return pl.pallas_call(
        _out_proj_kernel,
        grid=(bsz, t // tm),
        in_specs=[
            pl.BlockSpec((None, tm, k), lambda b, i: (b, i, 0)),
            pl.BlockSpec((k, d), lambda b, i: (0, 0)),
            pl.BlockSpec((None, tm, d), lambda b, i: (b, i, 0)),
            vec, vec, vec,
            pl.BlockSpec((d, LANES), lambda b, i: (0, 0)),
        ],
        out_specs=[
            pl.BlockSpec((None, tm, d), lambda b, i: (b, i, 0)),
            pl.BlockSpec((None, tm, d), lambda b, i: (b, i, 0)),
            pl.BlockSpec((None, tm, LANES), lambda b, i: (b, i, 0)),
        ],
        out_shape=[
            jax.ShapeDtypeStruct((bsz, t, d), _F32),
            jax.ShapeDtypeStruct((bsz, t, d), _MXU),
            jax.ShapeDtypeStruct((bsz, t, LANES), _F32),
        ],
        compiler_params=_cparams("parallel", "parallel"),
        name="out_proj",
    )(y, w, x, gate, a, s, wr)


def _moe_ffn_kernel(x_ref, g_ref, wg_ref, wu_ref, wd_ref, o_ref):
    x = x_ref[...]
    gate = jnp.dot(x, wg_ref[...], preferred_element_type=_F32)
    up = jnp.dot(x, wu_ref[...], preferred_element_type=_F32)
    hid = gate * jax.nn.sigmoid(gate) * up
    out = jnp.dot(hid.astype(_MXU), wd_ref[...], preferred_element_type=_F32)
    o_ref[...] = out * g_ref[...]


def _moe_ffn(xg, g, wg, wu, wd):
    bsz, ne, cap, d = xg.shape
    f = wg.shape[2]
    return pl.pallas_call(
        _moe_ffn_kernel,
        grid=(ne, bsz),
        in_specs=[
            pl.BlockSpec((None, None, cap, d), lambda e, b: (b, e, 0, 0)),
            pl.BlockSpec((None, None, cap, 1), lambda e, b: (b, e, 0, 0)),
            pl.BlockSpec((None, d, f), lambda e, b: (e, 0, 0)),
            pl.BlockSpec((None, d, f), lambda e, b: (e, 0, 0)),
            pl.BlockSpec((None, f, d), lambda e, b: (e, 0, 0)),
        ],
        out_specs=pl.BlockSpec((None, None, cap, d), lambda e, b: (b, e, 0, 0)),
        out_shape=jax.ShapeDtypeStruct((bsz, ne, cap, d), _F32),
        compiler_params=_cparams("parallel", "arbitrary"),
        name="moe_ffn",
    )(xg, g, wg, wu, wd)


def _ec_moe(h, logits, wg, wu, wd):
    bsz, t, d = h.shape
    ne = logits.shape[-1]
    cap = EC_CAPACITY * t // ne
    aff = jax.nn.softmax(logits, axis=-1)
    g, idx = lax.top_k(jnp.swapaxes(aff, 1, 2), cap)
    xg = jax.vmap(lambda hb, ib: hb[ib])(h, idx)
    out = _moe_ffn(xg, g[..., None], wg, wu, wd)
    return jax.vmap(lambda ib, ob: jnp.zeros((t, d), ob.dtype).at[ib.reshape(-1)].add(ob.reshape(-1, d)))(idx, out)


def _split(z, sizes):
    return jnp.split(z, [int(s) for s in np.cumsum(sizes)[:-1]], axis=-1)


def _dwconv(x, w, b, pad_left):
    width = w.shape[0]
    length = x.shape[-2]
    pad = [(0, 0)] * (x.ndim - 2) + [(pad_left, width - 1 - pad_left), (0, 0)]
    xp = jnp.pad(x, pad)
    y = b
    for j in range(width):
        y = y + w[j] * xp[..., j:j + length, :]
    return y


def _heads(t, n):
    bsz, length, _ = t.shape
    return t.reshape(bsz, length, n, -1).transpose(0, 2, 1, 3)


def _gla_chunk_scan(q, k, v, la, s0):
    bsz, heads, length, _ = k.shape
    n = length // GLA_CHUNK

    def blocks(t):
        return t.reshape(bsz, heads, n, GLA_CHUNK, t.shape[-1])

    k, v, la = blocks(k), blocks(v), blocks(la)
    b = jnp.cumsum(la, axis=3)
    b_last = b[:, :, :, -1:, :]
    ds = jnp.einsum('bhncd,bhnce->nbhde', k * jnp.exp(b_last - b), v)
    decay = jnp.moveaxis(jnp.exp(b_last[:, :, :, 0, :]), 2, 0)

    def step(s, inp):
        dec, d = inp
        return dec[..., None] * s + d, s

    s_last, s_in = lax.scan(step, s0, (decay, ds))
    q_d = blocks(q) * jnp.exp(b)
    att = jnp.einsum('bhnid,bhnjd->bhnij', q_d, k * jnp.exp(-b))
    att = jnp.where(jnp.tril(jnp.ones((GLA_CHUNK, GLA_CHUNK), bool)), att, 0.0)
    o = jnp.einsum('bhnij,bhnje->bhnie', att, v) + jnp.einsum('bhncd,nbhde->bhnce', q_d, s_in)
    return o.reshape(bsz, heads, length, v.shape[-1]), s_last


def _gla_bidir(q, k, v, la_f, la_b, s0_f, s0_b):
    def flip(t):
        return jnp.flip(t, axis=2)

    o_f, s_f = _gla_chunk_scan(q, k, v, la_f, s0_f)
    o_b, s_b = _gla_chunk_scan(flip(q), flip(k), flip(v), flip(la_b), s0_b)
    return o_f + jnp.flip(o_b, axis=2), s_f, s_b


def _gla_readout(o, g, norm_g):
    bsz, heads, length, dv = o.shape
    o = o * lax.rsqrt(jnp.mean(o * o, axis=-1, keepdims=True) + NORM_EPS)
    o = o.transpose(0, 2, 1, 3).reshape(bsz, length, heads * dv)
    return o * norm_g * jax.nn.silu(g)


def _even_inner(z, conv_w, conv_b, decay_up, decay_b, norm_g, s0_f, s0_b, grid_rows):
    bsz, t, _ = z.shape
    scw = conv_w.shape[1]
    key = decay_up.shape[-1]
    val = norm_g.shape[0]
    dk = key // GLA_HEADS
    bg, cg, xa, q, g, k, v, lr_f, lr_b = _split(z, (scw, scw, scw, key, val, key, val, GLA_RANK, GLA_RANK))
    la_f = jax.nn.log_sigmoid(lr_f @ decay_up[0] + decay_b[0]) / GLA_TAU
    la_b = jax.nn.log_sigmoid(lr_b @ decay_up[1] + decay_b[1]) / GLA_TAU
    o, s_f, s_b = _gla_bidir(_heads(q, GLA_HEADS) * dk ** -0.5, _heads(k, GLA_HEADS), _heads(v, GLA_HEADS),
                             _heads(la_f, GLA_HEADS), _heads(la_b, GLA_HEADS), s0_f, s0_b)
    if grid_rows:
        conv = _dwconv((cg * xa).reshape(bsz, t // GRID_W, GRID_W, scw), conv_w, conv_b, SC_PAD_LEFT)
        conv = conv.reshape(bsz, t, scw)
    else:
        conv = _dwconv(cg * xa, conv_w, conv_b, SC_PAD_LEFT)
    y = jnp.concatenate([bg * conv, _gla_readout(o, g, norm_g)], axis=-1)
    return y, s_f, s_b


def _lin_combine(earlier, later):
    a1, b1 = earlier
    a2, b2 = later
    return a1 * a2, a2 * b1 + b2


def _linear_scan(a, b, h0, reverse):
    if reverse:
        a, b = jnp.flip(a, axis=1), jnp.flip(b, axis=1)
    b = b.at[:, 0].add(a[:, 0] * h0)
    _, h = lax.associative_scan(_lin_combine, (a, b), axis=1)
    h_last = h[:, -1]
    if reverse:
        h = jnp.flip(h, axis=1)
    return h, h_last


def _blockdiag(x, w, b):
    bsz, length, width = x.shape
    y = jnp.einsum('bthi,hij->bthj', x.reshape(bsz, length, LRU_HEADS, width // LRU_HEADS), w)
    return y.reshape(bsz, length, width) + b


def _layernorm(x, g):
    xc = x - jnp.mean(x, axis=-1, keepdims=True)
    return xc * lax.rsqrt(jnp.mean(xc * xc, axis=-1, keepdims=True) + NORM_EPS) * g


def _odd_inner(z, conv_w, conv_b, gate_a_w, gate_a_b, gate_x_w, gate_x_b, lam, sg_norm_g, sg_w, sg_b, h0):
    bsz, t, _ = z.shape
    width = conv_w.shape[1]
    xb, gb, u, v = _split(z, (width, width, width, width))
    xc = _dwconv(xb, conv_w, conv_b, LRU_PAD_LEFT)
    hs, lasts = [], []
    for d, reverse in enumerate((False, True)):
        r = jax.nn.sigmoid(_blockdiag(xc, gate_a_w[d], gate_a_b[d]))
        i = jax.nn.sigmoid(_blockdiag(xc, gate_x_w[d], gate_x_b[d]))
        log_a = -LRU_C * r * jax.nn.softplus(-lam[d])
        h, h_last = _linear_scan(jnp.exp(log_a), jnp.sqrt(-jnp.expm1(2.0 * log_a)) * (i * xc), h0[d], reverse)
        hs.append(h)
        lasts.append(h_last)
    y_rec = (hs[0] + hs[1]) * jax.nn.gelu(gb)
    n = t // SG_CHUNK
    gdim = width // SG_GROUPS
    vb = _layernorm(jax.nn.gelu(v), sg_norm_g).reshape(bsz, n, SG_CHUNK, SG_GROUPS, gdim)
    mixed = jnp.einsum('gij,bnjgc->bnigc', sg_w, vb) + sg_b.T[:, :, None]
    y_sg = jax.nn.gelu(u) * mixed.reshape(bsz, t, width)
    return jnp.concatenate([y_rec, y_sg], axis=-1), lasts


def kernel(x, c, ctx, c_ctx, ada_w, ada_b, norm_mix_g, norm_ffn_g, norm_final_g, ev_w_in, ev_conv_w, ev_conv_b, ev_decay_up, ev_decay_b, ev_norm_g, ev_w_out, od_w_in, od_conv_w, od_conv_b, od_gate_a_w, od_gate_a_b, od_gate_x_w, od_gate_x_b, od_lambda, od_sg_norm_g, od_sg_w, od_sg_b, od_w_out, moe_router, moe_w_gate, moe_w_up, moe_w_down):
    depth = ada_w.shape[0]
    bsz, _, d = x.shape
    ne = moe_router.shape[-1]
    x_lat, x_ctx = x, ctx
    s_lat = jax.nn.silu(c)
    s_ctx = jax.nn.silu(c_ctx)

    def pad_cols(w, mult=LANES):
        n = w.shape[-1]
        return jnp.pad(w, [(0, 0)] * (w.ndim - 1) + [(0, (-n) % mult)])

    def vecs(m):
        return [v.reshape(bsz, 1, d) for v in m]

    for l in range(depth):
        last = l == depth - 1
        mod = jnp.split(s_lat @ ada_w[l] + ada_b[l], N_MOD, axis=-1)
        mod_c = jnp.split(jnp.broadcast_to(s_ctx @ ada_w[l] + ada_b[l], (bsz, N_MOD * d)), N_MOD, axis=-1)
        i = l // 2
        w_in = ev_w_in[i] if l % 2 == 0 else od_w_in[i]
        n_in = w_in.shape[1]
        w_in = pad_cols(w_in).astype(_MXU)
        w_out = (ev_w_out[i] if l % 2 == 0 else od_w_out[i]).astype(_MXU)
        wr = pad_cols(moe_router[l]).astype(_MXU)
        wg, wu, wd = moe_w_gate[l].astype(_MXU), moe_w_up[l].astype(_MXU), moe_w_down[l].astype(_MXU)

        def pre(xs, m, g):
            sh, sc = vecs(m[:2])
            return _norm_mod_matmul(xs, g * (1.0 + sc), sh, w_in)[..., :n_in]

        z_ctx = pre(x_ctx, mod_c, norm_mix_g[l])
        z_lat = pre(x_lat, mod, norm_mix_g[l])
        if l % 2 == 0:
            args = (ev_conv_w[i], ev_conv_b[i], ev_decay_up[i], ev_decay_b[i], ev_norm_g[i])
            key = ev_decay_up.shape[-1]
            val = ev_norm_g.shape[-1]
            zero = jnp.zeros((bsz, GLA_HEADS, key // GLA_HEADS, val // GLA_HEADS), _F32)
            y_ctx, s_f, s_b = _even_inner(z_ctx, *args, zero, zero, False)
            y_lat, _, _ = _even_inner(z_lat, *args, s_f, s_b, True)
        else:
            args = (od_conv_w[i], od_conv_b[i], od_gate_a_w[i], od_gate_a_b[i], od_gate_x_w[i], od_gate_x_b[i],
                    od_lambda[i], od_sg_norm_g[i], od_sg_w[i], od_sg_b[i])
            zero = jnp.zeros((bsz, od_conv_w.shape[-1]), _F32)
            y_ctx, lasts = _odd_inner(z_ctx, *args, (zero, zero))
            y_lat, _ = _odd_inner(z_lat, *args, lasts)

        def post(xs, y, m):
            gate1, sh, sc, gate2 = vecs(m[2:])
            xs, h, logits = _out_proj(y, w_out, xs, gate1, norm_ffn_g[l] * (1.0 + sc), sh, wr)
            return xs + gate2 * _ec_moe(h, logits[..., :ne], wg, wu, wd)

        x_lat = post(x_lat, y_lat, mod)
        if not last:
            x_ctx = post(x_ctx, y_ctx, mod_c)
    ms = jnp.mean(x_lat * x_lat, axis=-1, keepdims=True)
    return x_lat * lax.rsqrt(ms + NORM_EPS) * norm_final_g
```

```python
import functools

import jax
import jax.numpy as jnp
from jax import lax
from jax.experimental import pallas as pl
from jax.experimental.pallas import tpu as pltpu

_MXU = jnp.bfloat16
_F32 = jnp.float32
NORM_EPS = 1e-6
N_MOD = 6
GRID_W = 64
GLA_HEADS = 4
GLA_RANK = 16
GLA_TAU = 16.0
GLA_CHUNK = 64
LRU_HEADS = 4
LRU_C = 8.0
SG_GROUPS = 4
SG_CHUNK = 128
EC_CAPACITY = 2
LANES = 128
SUBLANES = 8
VMEM_LIMIT = 48 * 1024 * 1024


def _cparams(*sem):
    return pltpu.CompilerParams(dimension_semantics=sem, vmem_limit_bytes=VMEM_LIMIT)


def _row_tile(t, cap=256):
    tm = min(t, cap)
    assert t % tm == 0
    return tm


def _dot(a, b):
    return jnp.dot(a.astype(_MXU), b.astype(_MXU), preferred_element_type=_F32)


def _dot_nt(a, b):
    return lax.dot_general(a.astype(_MXU), b.astype(_MXU), (((1,), (1,)), ((), ())), preferred_element_type=_F32)


def _dot_tn(a, b):
    return lax.dot_general(a.astype(_MXU), b.astype(_MXU), (((0,), (0,)), ((), ())), preferred_element_type=_F32)


def _gelu(x):
    return 0.5 * x * (1.0 + jnp.tanh(0.7978845608028654 * (x + 0.044715 * (x * x * x))))


def _silu(x):
    return x * jax.nn.sigmoid(x)


def _log_sigmoid(x):
    return jnp.minimum(x, 0.0) - jnp.log(1.0 + jnp.exp(-jnp.abs(x)))


def _norm_mod(x, a, s):
    ms = jnp.mean(x * x, axis=-1, keepdims=True)
    return x * lax.rsqrt(ms + NORM_EPS) * a + s


def _pre_kernel(x_ref, a_ref, s_ref, w_ref, o_ref):
    h = _norm_mod(x_ref[...], a_ref[...], s_ref[...])
    o_ref[...] = _dot(h, w_ref[...])


def _pre_even_kernel(x_ref, a_ref, s_ref, w_ref, wlr_ref, up_ref, db_ref, o_ref, la_ref):
    h = _norm_mod(x_ref[...], a_ref[...], s_ref[...]).astype(_MXU)
    o_ref[...] = _dot(h, w_ref[...])
    lr = _dot(h, wlr_ref[...])
    la_ref[...] = _log_sigmoid(_dot(lr, up_ref[...]) + db_ref[...]) * (1.0 / GLA_TAU)


def _pre(x, a, s, w, decay=None):
    bsz, t, d = x.shape
    n = w.shape[1]
    tm = _row_tile(t)
    row = lambda b, i: (b, i, 0)
    vec = pl.BlockSpec((None, 1, d), lambda b, i: (b, 0, 0))
    full = lambda arr: pl.BlockSpec(arr.shape, lambda b, i: (0,) * arr.ndim)
    in_specs = [pl.BlockSpec((None, tm, d), row), vec, vec, full(w)]
    out_specs = [pl.BlockSpec((None, tm, n), row)]
    out_shape = [jax.ShapeDtypeStruct((bsz, t, n), _F32)]
    args = [x, a, s, w]
    body = _pre_kernel
    if decay is not None:
        body = _pre_even_kernel
        n_la = decay[1].shape[1]
        in_specs += [full(v) for v in decay]
        out_specs.append(pl.BlockSpec((None, tm, n_la), row))
        out_shape.append(jax.ShapeDtypeStruct((bsz, t, n_la), _F32))
        args += list(decay)
    return pl.pallas_call(
        body, grid=(bsz, t // tm), in_specs=in_specs, out_specs=out_specs, out_shape=out_shape,
        compiler_params=_cparams("parallel", "parallel"), name="pre_proj",
    )(*args)


def _cumsum_rows(tri, x):
    hi = x.astype(_MXU)
    r1 = x - hi.astype(_F32)
    mid = r1.astype(_MXU)
    lo = (r1 - mid.astype(_F32)).astype(_MXU)
    acc = jnp.dot(tri, hi, preferred_element_type=_F32)
    acc = acc + jnp.dot(tri, mid, preferred_element_type=_F32)
    return acc + jnp.dot(tri, lo, preferred_element_type=_F32)


def _gla_chunk(qkv_ref, la_ref, o_ref, st_ref, r0, keep, scale, reverse):
    c = GLA_CHUNK
    rows = pl.ds(r0, c)
    key = la_ref.shape[-1]
    dv = (qkv_ref.shape[-1] - 2 * key) // GLA_HEADS
    dk = key // GLA_HEADS
    b = _cumsum_rows(keep.astype(_MXU), la_ref[rows, :])
    btot = b[0:1] if reverse else b[c - 1:c]
    q = qkv_ref[rows, 0:key]
    k = qkv_ref[rows, key:2 * key]
    qd = q * (scale * jnp.exp(b))
    kd = k * jnp.exp(-b)
    kr = k * jnp.exp(btot - b)
    dec = jnp.exp(btot)
    pair = 2 * dk
    lane_lo = lax.broadcasted_iota(jnp.int32, (c, pair), 1) < dk
    for p in range(GLA_HEADS // 2):
        lanes = slice(p * pair, (p + 1) * pair)
        qd_p, kd_p, kr_p = qd[:, lanes], kd[:, lanes], kr[:, lanes]
        st = st_ref[p]
        vs = []
        for s in range(2):
            h = 2 * p + s
            qm = jnp.where(lane_lo if s == 0 else jnp.logical_not(lane_lo), qd_p, 0.0)
            att = jnp.where(keep, _dot_nt(qm, kd_p), 0.0)
            v_h = qkv_ref[rows, 2 * key + h * dv:2 * key + (h + 1) * dv]
            vs.append(v_h)
            o_ref[rows, h * dv:(h + 1) * dv] = _dot(att, v_h) + _dot_nt(qm, st)
        km = jnp.concatenate([jnp.where(lane_lo, kr_p, 0.0), jnp.where(lane_lo, 0.0, kr_p)], axis=0)
        st_ref[p] = dec[:, lanes] * st + _dot_tn(jnp.concatenate(vs, axis=0), km)


def _gla_kernel(qkv_f_ref, la_f_ref, qkv_b_ref, la_b_ref, s0f_ref, s0b_ref, of_ref, ob_ref, sf_ref, sb_ref, *, scale):
    i = pl.program_id(1)

    @pl.when(i == 0)
    def _():
        sf_ref[...] = s0f_ref[...]
        sb_ref[...] = s0b_ref[...]

    c = GLA_CHUNK
    n = qkv_f_ref.shape[0] // c
    ri = lax.broadcasted_iota(jnp.int32, (c, c), 0)
    ci = lax.broadcasted_iota(jnp.int32, (c, c), 1)
    keep_f = ri >= ci
    keep_b = ri <= ci

    def body(j, carry):
        _gla_chunk(qkv_f_ref, la_f_ref, of_ref, sf_ref, pl.multiple_of(j * c, c), keep_f, scale, False)
        _gla_chunk(qkv_b_ref, la_b_ref, ob_ref, sb_ref, pl.multiple_of((n - 1 - j) * c, c), keep_b, scale, True)
        return carry

    lax.fori_loop(0, n, body, 0)


def _gla(z, la, s0f, s0b, qkv_block, scale):
    bsz, t, _ = z.shape
    key = la.shape[-1] // 2
    st_shape = s0f.shape[1:]
    val = GLA_HEADS * st_shape[1]
    width = 2 * key + val
    tb = _row_tile(t, 512)
    nt = t // tb
    st_spec = pl.BlockSpec((None,) + st_shape, lambda b, i: (b, 0, 0, 0))
    outs = pl.pallas_call(
        functools.partial(_gla_kernel, scale=scale),
        grid=(bsz, nt),
        in_specs=[
            pl.BlockSpec((None, tb, width), lambda b, i: (b, i, qkv_block)),
            pl.BlockSpec((None, tb, key), lambda b, i: (b, i, 0)),
            pl.BlockSpec((None, tb, width), lambda b, i: (b, nt - 1 - i, qkv_block)),
            pl.BlockSpec((None, tb, key), lambda b, i: (b, nt - 1 - i, 1)),
            st_spec, st_spec,
        ],
        out_specs=[
            pl.BlockSpec((None, tb, val), lambda b, i: (b, i, 0)),
            pl.BlockSpec((None, tb, val), lambda b, i: (b, nt - 1 - i, 0)),
            st_spec, st_spec,
        ],
        out_shape=[
            jax.ShapeDtypeStruct((bsz, t, val), _F32),
            jax.ShapeDtypeStruct((bsz, t, val), _F32),
            jax.ShapeDtypeStruct(s0f.shape, _F32),
            jax.ShapeDtypeStruct(s0b.shape, _F32),
        ],
        compiler_params=_cparams("parallel", "arbitrary"),
        name="gla_scan",
    )(z, la, z, la, s0f, s0b)
    return outs


def _lru_coeffs(x_ref, prev_ref, next_ref, first, last, cw_ref, cb_ref, wg_ref, ba_ref, bx_ref, sp_ref):
    x = x_ref[...]
    tm, width = x.shape
    hd = width // LRU_HEADS
    prev = jnp.where(first, 0.0, prev_ref[...])
    nxt = jnp.where(last, 0.0, next_ref[...])
    r = lax.broadcasted_iota(jnp.int32, (tm, 1), 0)
    x_m1 = jnp.where(r == 0, prev[SUBLANES - 1:SUBLANES], pltpu.roll(x, 1, 0))
    x_m2 = jnp.where(r == 0, prev[SUBLANES - 2:SUBLANES - 1],
                     jnp.where(r == 1, prev[SUBLANES - 1:SUBLANES], pltpu.roll(x, 2, 0)))
    x_p1 = jnp.where(r == tm - 1, nxt[0:1], pltpu.roll(x, tm - 1, 0))
    xc = cb_ref[...] + cw_ref[0:1] * x_m2 + cw_ref[1:2] * x_m1 + cw_ref[2:3] * x + cw_ref[3:4] * x_p1
    a_parts, b_parts = [], []
    for h in range(LRU_HEADS):
        lanes = slice(h * hd, (h + 1) * hd)
        xh = xc[:, lanes]
        g = _dot(xh, wg_ref[h])
        rg = jax.nn.sigmoid(g[:, :hd] + ba_ref[:, lanes])
        ig = jax.nn.sigmoid(g[:, hd:] + bx_ref[:, lanes])
        log_a = rg * sp_ref[:, lanes]
        th = jnp.tanh(log_a)
        one_m_a2 = -2.0 * th / (1.0 - th)
        a_parts.append(jnp.exp(log_a))
        b_parts.append(jnp.sqrt(one_m_a2) * (ig * xh))
    return jnp.concatenate(a_parts, axis=-1), jnp.concatenate(b_parts, axis=-1)


def _lru_scan_tile(a, bb, a_scr, b_scr, h_ref, carry_ref, reverse):
    tm, width = a.shape
    rr = lax.broadcasted_iota(jnp.int32, (tm, 1), 0) % SUBLANES
    d = 1
    while d < SUBLANES:
        if reverse:
            valid = rr < SUBLANES - d
            shift = tm - d
        else:
            valid = rr >= d
            shift = d
        a_s = jnp.where(valid, pltpu.roll(a, shift, 0), 1.0)
        b_s = jnp.where(valid, pltpu.roll(bb, shift, 0), 0.0)
        bb = a * b_s + bb
        a = a * a_s
        d *= 2
    a_scr[...] = a
    b_scr[...] = bb
    groups = tm // SUBLANES
    edge = 0 if reverse else SUBLANES - 1

    def body(j, h):
        g = (groups - 1 - j) if reverse else j
        rows = pl.ds(pl.multiple_of(g * SUBLANES, SUBLANES), SUBLANES)
        hg = a_scr[rows, :] * h + b_scr[rows, :]
        h_ref[rows, :] = hg
        return hg[edge:edge + 1]

    carry_ref[...] = lax.fori_loop(0, groups, body, carry_ref[...], unroll=4)


def _lru_kernel(xf_ref, xf_prev_ref, xf_next_ref, xb_ref, xb_prev_ref, xb_next_ref, cw_ref, cb_ref, wg_ref, ba_ref,
                bx_ref, sp_ref, h0f_ref, h0b_ref, hf_ref, hb_ref, cf_ref, cbk_ref, a_scr, b_scr):
    i = pl.program_id(1)
    nt = pl.num_programs(1)

    @pl.when(i == 0)
    def _():
        cf_ref[...] = h0f_ref[...]
        cbk_ref[...] = h0b_ref[...]

    a, bb = _lru_coeffs(xf_ref, xf_prev_ref, xf_next_ref, i == 0, i == nt - 1, cw_ref, cb_ref, wg_ref.at[0],
                        ba_ref.at[0], bx_ref.at[0], sp_ref.at[0])
    _lru_scan_tile(a, bb, a_scr, b_scr, hf_ref, cf_ref, False)
    a, bb = _lru_coeffs(xb_ref, xb_prev_ref, xb_next_ref, i == nt - 1, i == 0, cw_ref, cb_ref, wg_ref.at[1],
                        ba_ref.at[1], bx_ref.at[1], sp_ref.at[1])
    _lru_scan_tile(a, bb, a_scr, b_scr, hb_ref, cbk_ref, True)


def _lru(z, x_block, conv_w, conv_b, wg, ba, bx, sp, h0f, h0b):
    bsz, t, _ = z.shape
    width = conv_w.shape[1]
    tm = _row_tile(t)
    nt = t // tm
    gpt = tm // SUBLANES
    ngr = t // SUBLANES

    def tile(rev):
        return lambda b, i: (b, (nt - 1 - i) if rev else i, x_block)

    def prev(rev):
        return lambda b, i: (b, jnp.maximum(((nt - 1 - i) if rev else i) * gpt - 1, 0), x_block)

    def nxt(rev):
        return lambda b, i: (b, jnp.minimum((((nt - 1 - i) if rev else i) + 1) * gpt, ngr - 1), x_block)

    full = lambda arr: pl.BlockSpec(arr.shape, lambda b, i: (0,) * arr.ndim)
    st_spec = pl.BlockSpec((None, 1, width), lambda b, i: (b, 0, 0))
    in_specs = []
    for rev in (False, True):
        in_specs += [pl.BlockSpec((None, tm, width), tile(rev)),
                     pl.BlockSpec((None, SUBLANES, width), prev(rev)),
                     pl.BlockSpec((None, SUBLANES, width), nxt(rev))]
    in_specs += [full(conv_w), full(conv_b), full(wg), full(ba), full(bx), full(sp), st_spec, st_spec]
    return pl.pallas_call(
        _lru_kernel,
        grid=(bsz, nt),
        in_specs=in_specs,
        out_specs=[
            pl.BlockSpec((None, tm, width), lambda b, i: (b, i, 0)),
            pl.BlockSpec((None, tm, width), lambda b, i: (b, nt - 1 - i, 0)),
            st_spec, st_spec,
        ],
        out_shape=[
            jax.ShapeDtypeStruct((bsz, t, width), _F32),
            jax.ShapeDtypeStruct((bsz, t, width), _F32),
            jax.ShapeDtypeStruct((bsz, 1, width), _F32),
            jax.ShapeDtypeStruct((bsz, 1, width), _F32),
        ],
        scratch_shapes=[pltpu.VMEM((tm, width), _F32), pltpu.VMEM((tm, width), _F32)],
        compiler_params=_cparams("parallel", "arbitrary"),
        name="lru_scan",
    )(z, z, z, z, z, z, conv_w, conv_b, wg, ba, bx, sp, h0f, h0b)


def _residual_ffn_prenorm(y, x_ref, gate_ref, a_ref, s_ref, wr_ref, xo_ref, h_ref, lg_ref):
    xn = x_ref[...] + gate_ref[...] * y
    xo_ref[...] = xn
    h = _norm_mod(xn, a_ref[...], s_ref[...]).astype(_MXU)
    h_ref[...] = h
    lg_ref[...] = _dot(h, wr_ref[...])


def _post_even_kernel(za_ref, of_ref, ob_ref, cw_ref, cb_ref, ng_ref, w_ref, x_ref, gate_ref, a_ref, s_ref, wr_ref,
                      xo_ref, h_ref, lg_ref, y_scr, *, row_len):
    tm = za_ref.shape[0]
    scw = cw_ref.shape[1]
    val = ng_ref.shape[1]
    dv = val // GLA_HEADS
    bg = za_ref[:, 0:scw]
    cx = za_ref[:, scw:2 * scw] * za_ref[:, 2 * scw:3 * scw]
    r = lax.broadcasted_iota(jnp.int32, (tm, 1), 0) % row_len
    left = jnp.where(r == 0, 0.0, pltpu.roll(cx, 1, 0))
    right = jnp.where(r == row_len - 1, 0.0, pltpu.roll(cx, tm - 1, 0))
    conv = cb_ref[...] + cw_ref[0:1] * left + cw_ref[1:2] * cx + cw_ref[2:3] * right
    acc = _dot(bg * conv, w_ref[0:scw, :])
    for h in range(GLA_HEADS):
        lanes = slice(h * dv, (h + 1) * dv)
        o = of_ref[:, lanes] + ob_ref[:, lanes]
        o = o * lax.rsqrt(jnp.mean(o * o, axis=-1, keepdims=True) + NORM_EPS)
        y_scr[:, lanes] = o * ng_ref[:, lanes] * _silu(za_ref[:, 3 * scw + h * dv:3 * scw + (h + 1) * dv])
    acc = acc + _dot(y_scr[...], w_ref[scw:scw + val, :])
    _residual_ffn_prenorm(acc, x_ref, gate_ref, a_ref, s_ref, wr_ref, xo_ref, h_ref, lg_ref)


def _post_odd_kernel(zc_ref, hf_ref, hb_ref, lng_ref, sgw_ref, sgb_ref, w_ref, x_ref, gate_ref, a_ref, s_ref, wr_ref,
                     xo_ref, h_ref, lg_ref, y_scr):
    tm = zc_ref.shape[0]
    width = hf_ref.shape[1]
    gdim = width // SG_GROUPS
    y_rec = (hf_ref[...] + hb_ref[...]) * _gelu(zc_ref[:, 0:width])
    acc = _dot(y_rec, w_ref[0:width, :])
    gv = _gelu(zc_ref[:, 2 * width:3 * width])
    gc = gv - jnp.mean(gv, axis=-1, keepdims=True)
    vb = gc * lax.rsqrt(jnp.mean(gc * gc, axis=-1, keepdims=True) + NORM_EPS) * lng_ref[...]
    for n in range(tm // SG_CHUNK):
        rows = slice(n * SG_CHUNK, (n + 1) * SG_CHUNK)
        for g in range(SG_GROUPS):
            lanes = slice(g * gdim, (g + 1) * gdim)
            mixed = _dot(sgw_ref[g], vb[rows, lanes]) + sgb_ref[g]
            y_scr[rows, lanes] = _gelu(zc_ref[rows, width + g * gdim:width + (g + 1) * gdim]) * mixed
    acc = acc + _dot(y_scr[...], w_ref[width:2 * width, :])
    _residual_ffn_prenorm(acc, x_ref, gate_ref, a_ref, s_ref, wr_ref, xo_ref, h_ref, lg_ref)


def _post(body, z, z_width, seq_in, consts, w_out, x, gate, a, s, wr):
    bsz, t, d = x.shape
    tm = _row_tile(t)
    row = lambda b, i: (b, i, 0)
    vec = pl.BlockSpec((None, 1, d), lambda b, i: (b, 0, 0))
    full = lambda arr: pl.BlockSpec(arr.shape, lambda b, i: (0,) * arr.ndim)
    width = seq_in[0].shape[-1]
    in_specs = [pl.BlockSpec((None, tm, z_width), row)]
    in_specs += [pl.BlockSpec((None, tm, width), row) for _ in seq_in]
    in_specs += [full(v) for v in consts] + [full(w_out), pl.BlockSpec((None, tm, d), row), vec, vec, vec, full(wr)]
    return pl.pallas_call(
        body,
        grid=(bsz, t // tm),
        in_specs=in_specs,
        out_specs=[
            pl.BlockSpec((None, tm, d), row),
            pl.BlockSpec((None, tm, d), row),
            pl.BlockSpec((None, tm, LANES), row),
        ],
        out_shape=[
            jax.ShapeDtypeStruct((bsz, t, d), _F32),
            jax.ShapeDtypeStruct((bsz, t, d), _MXU),
            jax.ShapeDtypeStruct((bsz, t, LANES), _F32),
        ],
        scratch_shapes=[pltpu.VMEM((tm, width), _F32)],
        compiler_params=_cparams("parallel", "parallel"),
        name="post_proj",
    )(z, *seq_in, *consts, w_out, x, gate, a, s, wr)


def _moe_ffn_kernel(x_ref, g_ref, wg_ref, wu_ref, wd_ref, o_ref):
    x = x_ref[...]
    gate = _dot(x, wg_ref[...])
    up = _dot(x, wu_ref[...])
    o_ref[...] = _dot(_silu(gate) * up, wd_ref[...]) * g_ref[...]


def _moe_ffn(xg, g, wg, wu, wd):
    bsz, ne, cap, d = xg.shape
    f = wg.shape[2]
    return pl.pallas_call(
        _moe_ffn_kernel,
        grid=(ne, bsz),
        in_specs=[
            pl.BlockSpec((None, None, cap, d), lambda e, b: (b, e, 0, 0)),
            pl.BlockSpec((None, None, cap, 1), lambda e, b: (b, e, 0, 0)),
            pl.BlockSpec((None, d, f), lambda e, b: (e, 0, 0)),
            pl.BlockSpec((None, d, f), lambda e, b: (e, 0, 0)),
            pl.BlockSpec((None, f, d), lambda e, b: (e, 0, 0)),
        ],
        out_specs=pl.BlockSpec((None, None, cap, d), lambda e, b: (b, e, 0, 0)),
        out_shape=jax.ShapeDtypeStruct((bsz, ne, cap, d), _F32),
        compiler_params=_cparams("parallel", "arbitrary"),
        name="moe_ffn",
    )(xg, g, wg, wu, wd)


def _ec_moe(h, logits, wg, wu, wd):
    bsz, t, d = h.shape
    ne = logits.shape[-1]
    cap = EC_CAPACITY * t // ne
    aff = jax.nn.softmax(logits, axis=-1)
    g, idx = lax.top_k(jnp.swapaxes(aff, 1, 2), cap)
    xg = jax.vmap(lambda hb, ib: hb[ib])(h, idx)
    out = _moe_ffn(xg, g[..., None], wg, wu, wd)
    return jax.vmap(lambda ib, ob: jnp.zeros((t, d), ob.dtype).at[ib.reshape(-1)].add(ob.reshape(-1, d)))(idx, out)


def _pad_to(w, axis, size):
    pad = [(0, 0)] * w.ndim
    pad[axis] = (0, size - w.shape[axis])
    return jnp.pad(w, pad)


def kernel(x, c, ctx, c_ctx, ada_w, ada_b, norm_mix_g, norm_ffn_g, norm_final_g, ev_w_in, ev_conv_w, ev_conv_b, ev_decay_up, ev_decay_b, ev_norm_g, ev_w_out, od_w_in, od_conv_w, od_conv_b, od_gate_a_w, od_gate_a_b, od_gate_x_w, od_gate_x_b, od_lambda, od_sg_norm_g, od_sg_w, od_sg_b, od_w_out, moe_router, moe_w_gate, moe_w_up, moe_w_down):
    depth = ada_w.shape[0]
    bsz, seq, d = x.shape
    ne = moe_router.shape[-1]
    x_lat, x_ctx = x, ctx
    s_lat = jax.nn.silu(c)
    s_ctx = jax.nn.silu(c_ctx)

    def vecs(m):
        return [v.reshape(bsz, 1, d) for v in m]

    for l in range(depth):
        last = l == depth - 1
        even = l % 2 == 0
        i = l // 2
        mod = jnp.split(s_lat @ ada_w[l] + ada_b[l], N_MOD, axis=-1)
        mod_c = jnp.split(jnp.broadcast_to(s_ctx @ ada_w[l] + ada_b[l], (bsz, N_MOD * d)), N_MOD, axis=-1)
        wr = _pad_to(moe_router[l], 1, LANES).astype(_MXU)
        wg, wu, wd = moe_w_gate[l].astype(_MXU), moe_w_up[l].astype(_MXU), moe_w_down[l].astype(_MXU)

        if even:
            w = ev_w_in[i]
            scw = ev_conv_w.shape[-1]
            key = ev_decay_up.shape[-1]
            val = ev_norm_g.shape[-1]
            o_q = 3 * scw
            o_g = o_q + key
            o_k = o_g + val
            o_v = o_k + key
            o_lr = o_v + val
            w_in = jnp.concatenate([w[:, :o_q], w[:, o_g:o_k], w[:, o_q:o_g], w[:, o_k:o_lr]], axis=1).astype(_MXU)
            w_lr = _pad_to(w[:, o_lr:], 1, LANES).astype(_MXU)
            up = jnp.zeros((LANES, 2 * key), _F32)
            up = up.at[:GLA_RANK, :key].set(ev_decay_up[i, 0]).at[GLA_RANK:2 * GLA_RANK, key:].set(ev_decay_up[i, 1])
            decay = (w_lr, up.astype(_MXU), ev_decay_b[i].reshape(1, 2 * key))
            w_out = ev_w_out[i].astype(_MXU)
            conv_w = _pad_to(ev_conv_w[i], 0, SUBLANES)
            consts = (conv_w, ev_conv_b[i].reshape(1, scw), ev_norm_g[i].reshape(1, val))
            epi_width = 3 * scw + val
            assert (3 * scw + val) % (2 * key + val) == 0
            qkv_block = epi_width // (2 * key + val)
            dk, dv = key // GLA_HEADS, val // GLA_HEADS
            scale = dk ** -0.5
            zero_st = jnp.zeros((bsz, GLA_HEADS // 2, dv, 2 * dk), _F32)

            def mixer(xs, m, s0, row_len):
                sh, sc = vecs(m[:2])
                z, la = _pre(xs, norm_mix_g[l] * (1.0 + sc), sh, w_in, decay)
                o_f, o_b, s_f, s_b = _gla(z, la, s0[0], s0[1], qkv_block, scale)
                body = functools.partial(_post_even_kernel, row_len=row_len)
                return (body, z, epi_width, (o_f, o_b), consts, w_out), (s_f, s_b)

            ctx_args, state = mixer(x_ctx, mod_c, (zero_st, zero_st), x_ctx.shape[1])
            lat_args, _ = mixer(x_lat, mod, state, GRID_W)
        else:
            w = od_w_in[i]
            width = od_conv_w.shape[-1]
            hd = width // LRU_HEADS
            w_in = jnp.concatenate([w[:, width:], w[:, :width]], axis=1).astype(_MXU)
            w_out = od_w_out[i].astype(_MXU)
            conv_w = _pad_to(od_conv_w[i], 0, SUBLANES)
            conv_b = od_conv_b[i].reshape(1, width)
            wgate = jnp.concatenate([od_gate_a_w[i], od_gate_x_w[i]], axis=-1).astype(_MXU)
            ba = od_gate_a_b[i].reshape(2, 1, width)
            bx = od_gate_x_b[i].reshape(2, 1, width)
            sp = (-LRU_C * jax.nn.softplus(-od_lambda[i])).reshape(2, 1, width)
            sgb = jnp.broadcast_to(od_sg_b[i][:, :, None], (SG_GROUPS, SG_CHUNK, width // SG_GROUPS))
            consts = (od_sg_norm_g[i].reshape(1, width), od_sg_w[i].astype(_MXU), sgb)
            zero_st = jnp.zeros((bsz, 1, width), _F32)

            def mixer(xs, m, s0, row_len):
                sh, sc = vecs(m[:2])
                (z,) = _pre(xs, norm_mix_g[l] * (1.0 + sc), sh, w_in)
                h_f, h_b, c_f, c_b = _lru(z, 3, conv_w, conv_b, wgate, ba, bx, sp, s0[0], s0[1])
                return (_post_odd_kernel, z, 3 * width, (h_f, h_b), consts, w_out), (c_f, c_b)

            ctx_args, state = mixer(x_ctx, mod_c, (zero_st, zero_st), None)
            lat_args, _ = mixer(x_lat, mod, state, None)

        def post(args, xs, m):
            gate1, sh, sc, gate2 = vecs(m[2:])
            xs, h, logits = _post(*args, xs, gate1, norm_ffn_g[l] * (1.0 + sc), sh, wr)
            return xs + gate2 * _ec_moe(h, logits[..., :ne], wg, wu, wd)

        x_lat = post(lat_args, x_lat, mod)
        if not last:
            x_ctx = post(ctx_args, x_ctx, mod_c)
    ms = jnp.mean(x_lat * x_lat, axis=-1, keepdims=True)
    return x_lat * lax.rsqrt(ms + NORM_EPS) * norm_final_g
```

```python
import functools

import jax
import jax.numpy as jnp
from jax import lax
from jax.experimental import pallas as pl
from jax.experimental.pallas import tpu as pltpu

_MXU = jnp.bfloat16
_F32 = jnp.float32
NORM_EPS = 1e-6
N_MOD = 6
GRID_W = 64
GLA_HEADS = 4
GLA_RANK = 16
GLA_TAU = 16.0
GLA_CHUNK = 64
LRU_HEADS = 4
LRU_C = 8.0
SG_GROUPS = 4
SG_CHUNK = 128
EC_CAPACITY = 2
LANES = 128
SUBLANES = 8
VMEM_LIMIT = 48 * 1024 * 1024
MOE_VMEM_LIMIT = 56 * 1024 * 1024
MOE_ROWS = 512


def _cparams(*sem):
    return pltpu.CompilerParams(dimension_semantics=sem, vmem_limit_bytes=VMEM_LIMIT)


def _row_tile(t, cap=256):
    tm = min(t, cap)
    assert t % tm == 0
    return tm


def _dot(a, b):
    return jnp.dot(a.astype(_MXU), b.astype(_MXU), preferred_element_type=_F32)


def _dot_nt(a, b):
    return lax.dot_general(a.astype(_MXU), b.astype(_MXU), (((1,), (1,)), ((), ())), preferred_element_type=_F32)


def _dot_tn(a, b):
    return lax.dot_general(a.astype(_MXU), b.astype(_MXU), (((0,), (0,)), ((), ())), preferred_element_type=_F32)


def _gelu(x):
    return 0.5 * x * (1.0 + jnp.tanh(0.7978845608028654 * (x + 0.044715 * (x * x * x))))


def _silu(x):
    return x * jax.nn.sigmoid(x)


def _log_sigmoid(x):
    return jnp.minimum(x, 0.0) - jnp.log(1.0 + jnp.exp(-jnp.abs(x)))


def _norm_mod(x, a, s):
    ms = jnp.mean(x * x, axis=-1, keepdims=True)
    return x * lax.rsqrt(ms + NORM_EPS) * a + s


def _pre_kernel(x_ref, a_ref, s_ref, w_ref, o_ref):
    h = _norm_mod(x_ref[...], a_ref[...], s_ref[...])
    o_ref[...] = _dot(h, w_ref[...])


def _pre_even_kernel(x_ref, a_ref, s_ref, w_ref, wlr_ref, up_ref, db_ref, o_ref, la_ref):
    h = _norm_mod(x_ref[...], a_ref[...], s_ref[...]).astype(_MXU)
    o_ref[...] = _dot(h, w_ref[...])
    lr = _dot(h, wlr_ref[...])
    la_ref[...] = _log_sigmoid(_dot(lr, up_ref[...]) + db_ref[...]) * (1.0 / GLA_TAU)


def _pre(x, a, s, w, decay=None):
    bsz, t, d = x.shape
    n = w.shape[1]
    tm = _row_tile(t)
    row = lambda b, i: (b, i, 0)
    vec = pl.BlockSpec((None, 1, d), lambda b, i: (b, 0, 0))
    full = lambda arr: pl.BlockSpec(arr.shape, lambda b, i: (0,) * arr.ndim)
    in_specs = [pl.BlockSpec((None, tm, d), row), vec, vec, full(w)]
    out_specs = [pl.BlockSpec((None, tm, n), row)]
    out_shape = [jax.ShapeDtypeStruct((bsz, t, n), _F32)]
    args = [x, a, s, w]
    body = _pre_kernel
    if decay is not None:
        body = _pre_even_kernel
        n_la = decay[1].shape[1]
        in_specs += [full(v) for v in decay]
        out_specs.append(pl.BlockSpec((None, tm, n_la), row))
        out_shape.append(jax.ShapeDtypeStruct((bsz, t, n_la), _F32))
        args += list(decay)
    return pl.pallas_call(
        body, grid=(bsz, t // tm), in_specs=in_specs, out_specs=out_specs, out_shape=out_shape,
        compiler_params=_cparams("parallel", "parallel"), name="pre_proj",
    )(*args)


def _cumsum_rows(tri, x):
    hi = x.astype(_MXU)
    r1 = x - hi.astype(_F32)
    mid = r1.astype(_MXU)
    lo = (r1 - mid.astype(_F32)).astype(_MXU)
    acc = jnp.dot(tri, hi, preferred_element_type=_F32)
    acc = acc + jnp.dot(tri, mid, preferred_element_type=_F32)
    return acc + jnp.dot(tri, lo, preferred_element_type=_F32)


def _gla_chunk(qkv_ref, la_ref, o_ref, st_ref, r0, keep, scale, reverse):
    c = GLA_CHUNK
    rows = pl.ds(r0, c)
    key = la_ref.shape[-1]
    dv = (qkv_ref.shape[-1] - 2 * key) // GLA_HEADS
    dk = key // GLA_HEADS
    b = _cumsum_rows(keep.astype(_MXU), la_ref[rows, :])
    btot = b[0:1] if reverse else b[c - 1:c]
    q = qkv_ref[rows, 0:key]
    k = qkv_ref[rows, key:2 * key]
    qd = q * (scale * jnp.exp(b))
    kd = k * jnp.exp(-b)
    kr = k * jnp.exp(btot - b)
    dec = jnp.exp(btot)
    pair = 2 * dk
    lane_lo = lax.broadcasted_iota(jnp.int32, (c, pair), 1) < dk
    for p in range(GLA_HEADS // 2):
        lanes = slice(p * pair, (p + 1) * pair)
        qd_p, kd_p, kr_p = qd[:, lanes], kd[:, lanes], kr[:, lanes]
        st = st_ref[p]
        vs = []
        for s in range(2):
            h = 2 * p + s
            qm = jnp.where(lane_lo if s == 0 else jnp.logical_not(lane_lo), qd_p, 0.0)
            att = jnp.where(keep, _dot_nt(qm, kd_p), 0.0)
            v_h = qkv_ref[rows, 2 * key + h * dv:2 * key + (h + 1) * dv]
            vs.append(v_h)
            o_ref[rows, h * dv:(h + 1) * dv] = _dot(att, v_h) + _dot_nt(qm, st)
        km = jnp.concatenate([jnp.where(lane_lo, kr_p, 0.0), jnp.where(lane_lo, 0.0, kr_p)], axis=0)
        st_ref[p] = dec[:, lanes] * st + _dot_tn(jnp.concatenate(vs, axis=0), km)


def _gla_kernel(qkv_f_ref, la_f_ref, qkv_b_ref, la_b_ref, s0f_ref, s0b_ref, of_ref, ob_ref, sf_ref, sb_ref, *, scale):
    i = pl.program_id(1)

    @pl.when(i == 0)
    def _():
        sf_ref[...] = s0f_ref[...]
        sb_ref[...] = s0b_ref[...]

    c = GLA_CHUNK
    n = qkv_f_ref.shape[0] // c
    ri = lax.broadcasted_iota(jnp.int32, (c, c), 0)
    ci = lax.broadcasted_iota(jnp.int32, (c, c), 1)
    keep_f = ri >= ci
    keep_b = ri <= ci

    def body(j, carry):
        _gla_chunk(qkv_f_ref, la_f_ref, of_ref, sf_ref, pl.multiple_of(j * c, c), keep_f, scale, False)
        _gla_chunk(qkv_b_ref, la_b_ref, ob_ref, sb_ref, pl.multiple_of((n - 1 - j) * c, c), keep_b, scale, True)
        return carry

    lax.fori_loop(0, n, body, 0)


def _gla(z, la, s0f, s0b, qkv_block, scale):
    bsz, t, _ = z.shape
    key = la.shape[-1] // 2
    st_shape = s0f.shape[1:]
    val = GLA_HEADS * st_shape[1]
    width = 2 * key + val
    tb = _row_tile(t, 512)
    nt = t // tb
    st_spec = pl.BlockSpec((None,) + st_shape, lambda b, i: (b, 0, 0, 0))
    outs = pl.pallas_call(
        functools.partial(_gla_kernel, scale=scale),
        grid=(bsz, nt),
        in_specs=[
            pl.BlockSpec((None, tb, width), lambda b, i: (b, i, qkv_block)),
            pl.BlockSpec((None, tb, key), lambda b, i: (b, i, 0)),
            pl.BlockSpec((None, tb, width), lambda b, i: (b, nt - 1 - i, qkv_block)),
            pl.BlockSpec((None, tb, key), lambda b, i: (b, nt - 1 - i, 1)),
            st_spec, st_spec,
        ],
        out_specs=[
            pl.BlockSpec((None, tb, val), lambda b, i: (b, i, 0)),
            pl.BlockSpec((None, tb, val), lambda b, i: (b, nt - 1 - i, 0)),
            st_spec, st_spec,
        ],
        out_shape=[
            jax.ShapeDtypeStruct((bsz, t, val), _F32),
            jax.ShapeDtypeStruct((bsz, t, val), _F32),
            jax.ShapeDtypeStruct(s0f.shape, _F32),
            jax.ShapeDtypeStruct(s0b.shape, _F32),
        ],
        compiler_params=_cparams("parallel", "arbitrary"),
        name="gla_scan",
    )(z, la, z, la, s0f, s0b)
    return outs


def _lru_coeffs(x_ref, prev_ref, next_ref, first, last, cw_ref, cb_ref, wg_ref, ba_ref, bx_ref, sp_ref):
    x = x_ref[...]
    tm, width = x.shape
    hd = width // LRU_HEADS
    prev = jnp.where(first, 0.0, prev_ref[...])
    nxt = jnp.where(last, 0.0, next_ref[...])
    r = lax.broadcasted_iota(jnp.int32, (tm, 1), 0)
    x_m1 = jnp.where(r == 0, prev[SUBLANES - 1:SUBLANES], pltpu.roll(x, 1, 0))
    x_m2 = jnp.where(r == 0, prev[SUBLANES - 2:SUBLANES - 1],
                     jnp.where(r == 1, prev[SUBLANES - 1:SUBLANES], pltpu.roll(x, 2, 0)))
    x_p1 = jnp.where(r == tm - 1, nxt[0:1], pltpu.roll(x, tm - 1, 0))
    xc = cb_ref[...] + cw_ref[0:1] * x_m2 + cw_ref[1:2] * x_m1 + cw_ref[2:3] * x + cw_ref[3:4] * x_p1
    a_parts, b_parts = [], []
    for h in range(LRU_HEADS):
        lanes = slice(h * hd, (h + 1) * hd)
        xh = xc[:, lanes]
        g = _dot(xh, wg_ref[h])
        rg = jax.nn.sigmoid(g[:, :hd] + ba_ref[:, lanes])
        ig = jax.nn.sigmoid(g[:, hd:] + bx_ref[:, lanes])
        log_a = rg * sp_ref[:, lanes]
        th = jnp.tanh(log_a)
        one_m_a2 = -2.0 * th / (1.0 - th)
        a_parts.append(jnp.exp(log_a))
        b_parts.append(jnp.sqrt(one_m_a2) * (ig * xh))
    return jnp.concatenate(a_parts, axis=-1), jnp.concatenate(b_parts, axis=-1)


def _lru_scan_tile(a, bb, a_scr, b_scr, h_ref, carry_ref, reverse):
    tm, width = a.shape
    rr = lax.broadcasted_iota(jnp.int32, (tm, 1), 0) % SUBLANES
    d = 1
    while d < SUBLANES:
        if reverse:
            valid = rr < SUBLANES - d
            shift = tm - d
        else:
            valid = rr >= d
            shift = d
        a_s = jnp.where(valid, pltpu.roll(a, shift, 0), 1.0)
        b_s = jnp.where(valid, pltpu.roll(bb, shift, 0), 0.0)
        bb = a * b_s + bb
        a = a * a_s
        d *= 2
    a_scr[...] = a
    b_scr[...] = bb
    groups = tm // SUBLANES
    edge = 0 if reverse else SUBLANES - 1

    def body(j, h):
        g = (groups - 1 - j) if reverse else j
        rows = pl.ds(pl.multiple_of(g * SUBLANES, SUBLANES), SUBLANES)
        hg = a_scr[rows, :] * h + b_scr[rows, :]
        h_ref[rows, :] = hg
        return hg[edge:edge + 1]

    carry_ref[...] = lax.fori_loop(0, groups, body, carry_ref[...], unroll=4)


def _lru_kernel(xf_ref, xf_prev_ref, xf_next_ref, xb_ref, xb_prev_ref, xb_next_ref, cw_ref, cb_ref, wg_ref, ba_ref,
                bx_ref, sp_ref, h0f_ref, h0b_ref, hf_ref, hb_ref, cf_ref, cbk_ref, a_scr, b_scr):
    i = pl.program_id(1)
    nt = pl.num_programs(1)

    @pl.when(i == 0)
    def _():
        cf_ref[...] = h0f_ref[...]
        cbk_ref[...] = h0b_ref[...]

    a, bb = _lru_coeffs(xf_ref, xf_prev_ref, xf_next_ref, i == 0, i == nt - 1, cw_ref, cb_ref, wg_ref.at[0],
                        ba_ref.at[0], bx_ref.at[0], sp_ref.at[0])
    _lru_scan_tile(a, bb, a_scr, b_scr, hf_ref, cf_ref, False)
    a, bb = _lru_coeffs(xb_ref, xb_prev_ref, xb_next_ref, i == nt - 1, i == 0, cw_ref, cb_ref, wg_ref.at[1],
                        ba_ref.at[1], bx_ref.at[1], sp_ref.at[1])
    _lru_scan_tile(a, bb, a_scr, b_scr, hb_ref, cbk_ref, True)


def _lru(z, x_block, conv_w, conv_b, wg, ba, bx, sp, h0f, h0b):
    bsz, t, _ = z.shape
    width = conv_w.shape[1]
    tm = _row_tile(t)
    nt = t // tm
    gpt = tm // SUBLANES
    ngr = t // SUBLANES

    def tile(rev):
        return lambda b, i: (b, (nt - 1 - i) if rev else i, x_block)

    def prev(rev):
        return lambda b, i: (b, jnp.maximum(((nt - 1 - i) if rev else i) * gpt - 1, 0), x_block)

    def nxt(rev):
        return lambda b, i: (b, jnp.minimum((((nt - 1 - i) if rev else i) + 1) * gpt, ngr - 1), x_block)

    full = lambda arr: pl.BlockSpec(arr.shape, lambda b, i: (0,) * arr.ndim)
    st_spec = pl.BlockSpec((None, 1, width), lambda b, i: (b, 0, 0))
    in_specs = []
    for rev in (False, True):
        in_specs += [pl.BlockSpec((None, tm, width), tile(rev)),
                     pl.BlockSpec((None, SUBLANES, width), prev(rev)),
                     pl.BlockSpec((None, SUBLANES, width), nxt(rev))]
    in_specs += [full(conv_w), full(conv_b), full(wg), full(ba), full(bx), full(sp), st_spec, st_spec]
    return pl.pallas_call(
        _lru_kernel,
        grid=(bsz, nt),
        in_specs=in_specs,
        out_specs=[
            pl.BlockSpec((None, tm, width), lambda b, i: (b, i, 0)),
            pl.BlockSpec((None, tm, width), lambda b, i: (b, nt - 1 - i, 0)),
            st_spec, st_spec,
        ],
        out_shape=[
            jax.ShapeDtypeStruct((bsz, t, width), _F32),
            jax.ShapeDtypeStruct((bsz, t, width), _F32),
            jax.ShapeDtypeStruct((bsz, 1, width), _F32),
            jax.ShapeDtypeStruct((bsz, 1, width), _F32),
        ],
        scratch_shapes=[pltpu.VMEM((tm, width), _F32), pltpu.VMEM((tm, width), _F32)],
        compiler_params=_cparams("parallel", "arbitrary"),
        name="lru_scan",
    )(z, z, z, z, z, z, conv_w, conv_b, wg, ba, bx, sp, h0f, h0b)


def _residual_ffn_prenorm(y, x_ref, gate_ref, a_ref, s_ref, wr_ref, xo_ref, h_ref, lg_ref):
    xn = x_ref[...] + gate_ref[...] * y
    xo_ref[...] = xn
    h = _norm_mod(xn, a_ref[...], s_ref[...]).astype(_MXU)
    h_ref[...] = h
    lg_ref[...] = _dot(h, wr_ref[...])


def _post_even_kernel(za_ref, of_ref, ob_ref, cw_ref, cb_ref, ng_ref, w_ref, x_ref, gate_ref, a_ref, s_ref, wr_ref,
                      xo_ref, h_ref, lg_ref, y_scr, *, row_len):
    tm = za_ref.shape[0]
    scw = cw_ref.shape[1]
    val = ng_ref.shape[1]
    dv = val // GLA_HEADS
    bg = za_ref[:, 0:scw]
    cx = za_ref[:, scw:2 * scw] * za_ref[:, 2 * scw:3 * scw]
    r = lax.broadcasted_iota(jnp.int32, (tm, 1), 0) % row_len
    left = jnp.where(r == 0, 0.0, pltpu.roll(cx, 1, 0))
    right = jnp.where(r == row_len - 1, 0.0, pltpu.roll(cx, tm - 1, 0))
    conv = cb_ref[...] + cw_ref[0:1] * left + cw_ref[1:2] * cx + cw_ref[2:3] * right
    acc = _dot(bg * conv, w_ref[0:scw, :])
    for h in range(GLA_HEADS):
        lanes = slice(h * dv, (h + 1) * dv)
        o = of_ref[:, lanes] + ob_ref[:, lanes]
        o = o * lax.rsqrt(jnp.mean(o * o, axis=-1, keepdims=True) + NORM_EPS)
        y_scr[:, lanes] = o * ng_ref[:, lanes] * _silu(za_ref[:, 3 * scw + h * dv:3 * scw + (h + 1) * dv])
    acc = acc + _dot(y_scr[...], w_ref[scw:scw + val, :])
    _residual_ffn_prenorm(acc, x_ref, gate_ref, a_ref, s_ref, wr_ref, xo_ref, h_ref, lg_ref)


def _post_odd_kernel(zc_ref, hf_ref, hb_ref, lng_ref, sgw_ref, sgb_ref, w_ref, x_ref, gate_ref, a_ref, s_ref, wr_ref,
                     xo_ref, h_ref, lg_ref, y_scr):
    tm = zc_ref.shape[0]
    width = hf_ref.shape[1]
    gdim = width // SG_GROUPS
    y_rec = (hf_ref[...] + hb_ref[...]) * _gelu(zc_ref[:, 0:width])
    acc = _dot(y_rec, w_ref[0:width, :])
    gv = _gelu(zc_ref[:, 2 * width:3 * width])
    gc = gv - jnp.mean(gv, axis=-1, keepdims=True)
    vb = gc * lax.rsqrt(jnp.mean(gc * gc, axis=-1, keepdims=True) + NORM_EPS) * lng_ref[...]
    for n in range(tm // SG_CHUNK):
        rows = slice(n * SG_CHUNK, (n + 1) * SG_CHUNK)
        for g in range(SG_GROUPS):
            lanes = slice(g * gdim, (g + 1) * gdim)
            mixed = _dot(sgw_ref[g], vb[rows, lanes]) + sgb_ref[g]
            y_scr[rows, lanes] = _gelu(zc_ref[rows, width + g * gdim:width + (g + 1) * gdim]) * mixed
    acc = acc + _dot(y_scr[...], w_ref[width:2 * width, :])
    _residual_ffn_prenorm(acc, x_ref, gate_ref, a_ref, s_ref, wr_ref, xo_ref, h_ref, lg_ref)


def _post(body, z, z_width, seq_in, consts, w_out, x, gate, a, s, wr):
    bsz, t, d = x.shape
    tm = _row_tile(t)
    row = lambda b, i: (b, i, 0)
    vec = pl.BlockSpec((None, 1, d), lambda b, i: (b, 0, 0))
    full = lambda arr: pl.BlockSpec(arr.shape, lambda b, i: (0,) * arr.ndim)
    width = seq_in[0].shape[-1]
    in_specs = [pl.BlockSpec((None, tm, z_width), row)]
    in_specs += [pl.BlockSpec((None, tm, width), row) for _ in seq_in]
    in_specs += [full(v) for v in consts] + [full(w_out), pl.BlockSpec((None, tm, d), row), vec, vec, vec, full(wr)]
    return pl.pallas_call(
        body,
        grid=(bsz, t // tm),
        in_specs=in_specs,
        out_specs=[
            pl.BlockSpec((None, tm, d), row),
            pl.BlockSpec((None, tm, d), row),
            pl.BlockSpec((None, tm, LANES), row),
        ],
        out_shape=[
            jax.ShapeDtypeStruct((bsz, t, d), _F32),
            jax.ShapeDtypeStruct((bsz, t, d), _MXU),
            jax.ShapeDtypeStruct((bsz, t, LANES), _F32),
        ],
        scratch_shapes=[pltpu.VMEM((tm, width), _F32)],
        compiler_params=_cparams("parallel", "parallel"),
        name="post_proj",
    )(z, *seq_in, *consts, w_out, x, gate, a, s, wr)


def _moe_ffn_kernel(x_ref, g_ref, wg_ref, wu_ref, wd_ref, o_ref, wg_scr, wu_scr, wd_scr):
    @pl.when(pl.program_id(1) == 0)
    def _():
        wg_scr[...] = wg_ref[...].astype(_MXU)
        wu_scr[...] = wu_ref[...].astype(_MXU)
        wd_scr[...] = wd_ref[...].astype(_MXU)

    bb, cap, d = x_ref.shape
    x = x_ref[...].reshape(bb * cap, d)
    gate = _dot(x, wg_scr[...])
    up = _dot(x, wu_scr[...])
    out = _dot(_silu(gate) * up, wd_scr[...]) * g_ref[...].reshape(bb * cap, 1)
    o_ref[...] = out.reshape(bb, cap, d).astype(o_ref.dtype)


def _moe_ffn(xg, g, wg, wu, wd):
    bsz, ne, cap, d = xg.shape
    f = wg.shape[2]
    bb = max(1, min(bsz, MOE_ROWS // cap))
    assert bsz % bb == 0
    tok = lambda e, b: (b, e, 0, 0)
    return pl.pallas_call(
        _moe_ffn_kernel,
        grid=(ne, bsz // bb),
        in_specs=[
            pl.BlockSpec((bb, None, cap, d), tok),
            pl.BlockSpec((bb, None, cap, 1), tok),
            pl.BlockSpec((None, d, f), lambda e, b: (e, 0, 0)),
            pl.BlockSpec((None, d, f), lambda e, b: (e, 0, 0)),
            pl.BlockSpec((None, f, d), lambda e, b: (e, 0, 0)),
        ],
        out_specs=pl.BlockSpec((bb, None, cap, d), tok),
        out_shape=jax.ShapeDtypeStruct((bsz, ne, cap, d), _MXU),
        scratch_shapes=[pltpu.VMEM((d, f), _MXU), pltpu.VMEM((d, f), _MXU), pltpu.VMEM((f, d), _MXU)],
        compiler_params=pltpu.CompilerParams(dimension_semantics=("parallel", "arbitrary"),
                                             vmem_limit_bytes=MOE_VMEM_LIMIT),
        name="moe_ffn",
    )(xg, g, wg, wu, wd)


def _combine_kernel(tile_ref, chunk_ref, valid_ref, tok_ref, y_ref, x_ref, gate_ref, fg_ref, o_ref, acc_ref, *,
                    final_norm):
    b = pl.program_id(0)
    s = pl.program_id(1)
    ns = pl.num_programs(1)
    t = tile_ref[b, s]
    first = jnp.logical_or(s == 0, tile_ref[b, jnp.maximum(s - 1, 0)] != t)
    last = jnp.logical_or(s == ns - 1, tile_ref[b, jnp.minimum(s + 1, ns - 1)] != t)
    tm = acc_ref.shape[0]
    ch = y_ref.shape[0]

    @pl.when(first)
    def _():
        acc_ref[...] = jnp.zeros_like(acc_ref)

    @pl.when(valid_ref[b, s] == 1)
    def _():
        rows = t * tm + lax.broadcasted_iota(jnp.int32, (tm, ch), 0)
        onehot = jnp.where(rows == tok_ref[...], 1.0, 0.0)
        acc_ref[...] += _dot(onehot, y_ref[...])

    @pl.when(last)
    def _():
        xn = x_ref[...] + gate_ref[...] * acc_ref[...]
        if final_norm:
            ms = jnp.mean(xn * xn, axis=-1, keepdims=True)
            xn = xn * lax.rsqrt(ms + NORM_EPS) * fg_ref[...]
        o_ref[...] = xn


def _combine(ys, toks, x, gate, final_g):
    bsz, t, d = x.shape
    p = ys.shape[1]
    tm = _row_tile(t)
    ch = _row_tile(p)
    nt, nc = t // tm, p // ch
    ns = nt + nc
    bounds = jnp.arange(nt + 1, dtype=jnp.int32) * tm
    cut = jax.vmap(lambda row: jnp.searchsorted(row, bounds, side="left"))(toks).astype(jnp.int32)
    lo, hi = cut[:, :-1], cut[:, 1:]
    c_lo = jnp.minimum(lo // ch, nc - 1)
    c_hi = jnp.maximum((hi - 1) // ch, c_lo)
    cnt = c_hi - c_lo + 1
    ends = jnp.cumsum(cnt, axis=-1)
    steps = jnp.arange(ns, dtype=jnp.int32)
    tile = jax.vmap(lambda e: jnp.searchsorted(e, steps, side="right"))(ends).astype(jnp.int32)
    valid = (tile < nt).astype(jnp.int32)
    tile = jnp.minimum(tile, nt - 1)
    starts = ends - cnt
    pick = lambda a: jnp.take_along_axis(a, tile, axis=-1)
    chunk = jnp.where(valid == 1, pick(c_lo) + steps[None, :] - pick(starts), pick(c_hi)).astype(jnp.int32)
    fg = jnp.ones((1, d), _F32) if final_g is None else final_g.reshape(1, d)
    grid_spec = pltpu.PrefetchScalarGridSpec(
        num_scalar_prefetch=3,
        grid=(bsz, ns),
        in_specs=[
            pl.BlockSpec((None, 1, ch), lambda b, s, ti, ci, vi: (b, 0, ci[b, s])),
            pl.BlockSpec((None, ch, d), lambda b, s, ti, ci, vi: (b, ci[b, s], 0)),
            pl.BlockSpec((None, tm, d), lambda b, s, ti, ci, vi: (b, ti[b, s], 0)),
            pl.BlockSpec((None, 1, d), lambda b, s, ti, ci, vi: (b, 0, 0)),
            pl.BlockSpec((1, d), lambda b, s, ti, ci, vi: (0, 0)),
        ],
        out_specs=pl.BlockSpec((None, tm, d), lambda b, s, ti, ci, vi: (b, ti[b, s], 0)),
        scratch_shapes=[pltpu.VMEM((tm, d), _F32)],
    )
    return pl.pallas_call(
        functools.partial(_combine_kernel, final_norm=final_g is not None),
        grid_spec=grid_spec,
        out_shape=jax.ShapeDtypeStruct((bsz, t, d), _F32),
        compiler_params=_cparams("parallel", "arbitrary"),
        name="moe_combine",
    )(tile, chunk, valid, toks.reshape(bsz, 1, p), ys, x, gate, fg)


def _ec_moe(x, gate, h, logits, wg, wu, wd, final_g):
    bsz, t, d = h.shape
    ne = logits.shape[-1]
    cap = EC_CAPACITY * t // ne
    aff = jax.nn.softmax(logits, axis=-1)
    g, idx = lax.top_k(jnp.swapaxes(aff, 1, 2), cap)
    xg = jax.vmap(lambda hb, ib: hb[ib])(h, idx)
    y = _moe_ffn(xg, g[..., None], wg, wu, wd).reshape(bsz, ne * cap, d)
    slots = jnp.broadcast_to(jnp.arange(ne * cap, dtype=jnp.int32), (bsz, ne * cap))
    toks, perm = lax.sort_key_val(idx.reshape(bsz, ne * cap).astype(jnp.int32), slots, dimension=-1)
    ys = jax.vmap(lambda yb, pb: yb[pb])(y, perm)
    return _combine(ys, toks, x, gate, final_g)


def _pad_to(w, axis, size):
    pad = [(0, 0)] * w.ndim
    pad[axis] = (0, size - w.shape[axis])
    return jnp.pad(w, pad)


def kernel(x, c, ctx, c_ctx, ada_w, ada_b, norm_mix_g, norm_ffn_g, norm_final_g, ev_w_in, ev_conv_w, ev_conv_b, ev_decay_up, ev_decay_b, ev_norm_g, ev_w_out, od_w_in, od_conv_w, od_conv_b, od_gate_a_w, od_gate_a_b, od_gate_x_w, od_gate_x_b, od_lambda, od_sg_norm_g, od_sg_w, od_sg_b, od_w_out, moe_router, moe_w_gate, moe_w_up, moe_w_down):
    depth = ada_w.shape[0]
    bsz, seq, d = x.shape
    ne = moe_router.shape[-1]
    x_lat, x_ctx = x, ctx
    s_lat = jax.nn.silu(c)
    s_ctx = jax.nn.silu(c_ctx)

    def vecs(m):
        return [v.reshape(bsz, 1, d) for v in m]

    for l in range(depth):
        last = l == depth - 1
        even = l % 2 == 0
        i = l // 2
        mod = jnp.split(s_lat @ ada_w[l] + ada_b[l], N_MOD, axis=-1)
        mod_c = jnp.split(jnp.broadcast_to(s_ctx @ ada_w[l] + ada_b[l], (bsz, N_MOD * d)), N_MOD, axis=-1)
        wr = _pad_to(moe_router[l], 1, LANES).astype(_MXU)
        wg, wu, wd = moe_w_gate[l], moe_w_up[l], moe_w_down[l]

        if even:
            w = ev_w_in[i]
            scw = ev_conv_w.shape[-1]
            key = ev_decay_up.shape[-1]
            val = ev_norm_g.shape[-1]
            o_q = 3 * scw
            o_g = o_q + key
            o_k = o_g + val
            o_v = o_k + key
            o_lr = o_v + val
            w_in = jnp.concatenate([w[:, :o_q], w[:, o_g:o_k], w[:, o_q:o_g], w[:, o_k:o_lr]], axis=1).astype(_MXU)
            w_lr = _pad_to(w[:, o_lr:], 1, LANES).astype(_MXU)
            up = jnp.zeros((LANES, 2 * key), _F32)
            up = up.at[:GLA_RANK, :key].set(ev_decay_up[i, 0]).at[GLA_RANK:2 * GLA_RANK, key:].set(ev_decay_up[i, 1])
            decay = (w_lr, up.astype(_MXU), ev_decay_b[i].reshape(1, 2 * key))
            w_out = ev_w_out[i].astype(_MXU)
            conv_w = _pad_to(ev_conv_w[i], 0, SUBLANES)
            consts = (conv_w, ev_conv_b[i].reshape(1, scw), ev_norm_g[i].reshape(1, val))
            epi_width = 3 * scw + val
            assert (3 * scw + val) % (2 * key + val) == 0
            qkv_block = epi_width // (2 * key + val)
            dk, dv = key // GLA_HEADS, val // GLA_HEADS
            scale = dk ** -0.5
            zero_st = jnp.zeros((bsz, GLA_HEADS // 2, dv, 2 * dk), _F32)

            def mixer(xs, m, s0, row_len):
                sh, sc = vecs(m[:2])
                z, la = _pre(xs, norm_mix_g[l] * (1.0 + sc), sh, w_in, decay)
                o_f, o_b, s_f, s_b = _gla(z, la, s0[0], s0[1], qkv_block, scale)
                body = functools.partial(_post_even_kernel, row_len=row_len)
                return (body, z, epi_width, (o_f, o_b), consts, w_out), (s_f, s_b)

            ctx_args, state = mixer(x_ctx, mod_c, (zero_st, zero_st), x_ctx.shape[1])
            lat_args, _ = mixer(x_lat, mod, state, GRID_W)
        else:
            w = od_w_in[i]
            width = od_conv_w.shape[-1]
            hd = width // LRU_HEADS
            w_in = jnp.concatenate([w[:, width:], w[:, :width]], axis=1).astype(_MXU)
            w_out = od_w_out[i].astype(_MXU)
            conv_w = _pad_to(od_conv_w[i], 0, SUBLANES)
            conv_b = od_conv_b[i].reshape(1, width)
            wgate = jnp.concatenate([od_gate_a_w[i], od_gate_x_w[i]], axis=-1).astype(_MXU)
            ba = od_gate_a_b[i].reshape(2, 1, width)
            bx = od_gate_x_b[i].reshape(2, 1, width)
            sp = (-LRU_C * jax.nn.softplus(-od_lambda[i])).reshape(2, 1, width)
            sgb = jnp.broadcast_to(od_sg_b[i][:, :, None], (SG_GROUPS, SG_CHUNK, width // SG_GROUPS))
            consts = (od_sg_norm_g[i].reshape(1, width), od_sg_w[i].astype(_MXU), sgb)
            zero_st = jnp.zeros((bsz, 1, width), _F32)

            def mixer(xs, m, s0, row_len):
                sh, sc = vecs(m[:2])
                (z,) = _pre(xs, norm_mix_g[l] * (1.0 + sc), sh, w_in)
                h_f, h_b, c_f, c_b = _lru(z, 3, conv_w, conv_b, wgate, ba, bx, sp, s0[0], s0[1])
                return (_post_odd_kernel, z, 3 * width, (h_f, h_b), consts, w_out), (c_f, c_b)

            ctx_args, state = mixer(x_ctx, mod_c, (zero_st, zero_st), None)
            lat_args, _ = mixer(x_lat, mod, state, None)

        def post(args, xs, m, final_g):
            gate1, sh, sc, gate2 = vecs(m[2:])
            xs, h, logits = _post(*args, xs, gate1, norm_ffn_g[l] * (1.0 + sc), sh, wr)
            return _ec_moe(xs, gate2, h, logits[..., :ne], wg, wu, wd, final_g)

        x_lat = post(lat_args, x_lat, mod, norm_final_g if last else None)
        if not last:
            x_ctx = post(ctx_args, x_ctx, mod_c, None)
    return x_lat
```

```python
import functools

import jax
import jax.numpy as jnp
from jax import lax
from jax.experimental import pallas as pl
from jax.experimental.pallas import tpu as pltpu

_MXU = jnp.bfloat16
_F32 = jnp.float32
NORM_EPS = 1e-6
N_MOD = 6
GRID_W = 64
GLA_HEADS = 4
GLA_RANK = 16
GLA_TAU = 16.0
GLA_CHUNK = 64
LRU_HEADS = 4
LRU_C = 8.0
SG_GROUPS = 4
SG_CHUNK = 128
EC_CAPACITY = 2
LANES = 128
SUBLANES = 8
VMEM_LIMIT = 48 * 1024 * 1024
MOE_VMEM_LIMIT = 56 * 1024 * 1024
MOE_ROWS = 512
GLA_UNROLL = 4
PROJ_ROWS = 512


def _cparams(*sem):
    return pltpu.CompilerParams(dimension_semantics=sem, vmem_limit_bytes=VMEM_LIMIT)


def _row_tile(t, cap=256):
    tm = min(t, cap)
    assert t % tm == 0
    return tm


def _dot(a, b):
    return jnp.dot(a.astype(_MXU), b.astype(_MXU), preferred_element_type=_F32)


def _dot_nt(a, b):
    return lax.dot_general(a.astype(_MXU), b.astype(_MXU), (((1,), (1,)), ((), ())), preferred_element_type=_F32)


def _dot_tn(a, b):
    return lax.dot_general(a.astype(_MXU), b.astype(_MXU), (((0,), (0,)), ((), ())), preferred_element_type=_F32)


def _gelu(x):
    return 0.5 * x * (1.0 + jnp.tanh(0.7978845608028654 * (x + 0.044715 * (x * x * x))))


def _silu(x):
    return x * jax.nn.sigmoid(x)


def _log_sigmoid(x):
    return jnp.minimum(x, 0.0) - jnp.log(1.0 + jnp.exp(-jnp.abs(x)))


def _norm_mod(x, a, s):
    ms = jnp.mean(x * x, axis=-1, keepdims=True)
    return x * lax.rsqrt(ms + NORM_EPS) * a + s


def _adaln_kernel(c_ref, w_ref, b_ref, o_ref):
    o_ref[...] = _dot(_silu(c_ref[...]), w_ref[...]) + b_ref[...]


def _adaln(cond, w, b):
    depth, d, n = w.shape
    r = cond.shape[0]
    tn = n // N_MOD
    return pl.pallas_call(
        _adaln_kernel,
        grid=(depth, N_MOD),
        in_specs=[
            pl.BlockSpec((r, d), lambda l, j: (0, 0)),
            pl.BlockSpec((None, d, tn), lambda l, j: (l, 0, j)),
            pl.BlockSpec((None, 1, tn), lambda l, j: (l, 0, j)),
        ],
        out_specs=pl.BlockSpec((None, r, tn), lambda l, j: (l, 0, j)),
        out_shape=jax.ShapeDtypeStruct((depth, r, n), _F32),
        compiler_params=_cparams("parallel", "parallel"),
        name="adaln",
    )(cond, w, b)


def _pre_kernel(x_ref, a_ref, s_ref, w_ref, o_ref):
    h = _norm_mod(x_ref[...], a_ref[...], s_ref[...])
    o_ref[...] = _dot(h, w_ref[...])


def _pre_even_kernel(x_ref, a_ref, s_ref, w_ref, wlr_ref, up_ref, db_ref, o_ref, la_ref):
    h = _norm_mod(x_ref[...], a_ref[...], s_ref[...]).astype(_MXU)
    o_ref[...] = _dot(h, w_ref[...])
    lr = _dot(h, wlr_ref[...])
    la_ref[...] = _log_sigmoid(_dot(lr, up_ref[...]) + db_ref[...]) * (1.0 / GLA_TAU)


def _pre(x, a, s, w, decay=None):
    bsz, t, d = x.shape
    n = w.shape[1]
    tm = _row_tile(t, PROJ_ROWS)
    row = lambda b, i: (b, i, 0)
    vec = pl.BlockSpec((None, 1, d), lambda b, i: (b, 0, 0))
    full = lambda arr: pl.BlockSpec(arr.shape, lambda b, i: (0,) * arr.ndim)
    in_specs = [pl.BlockSpec((None, tm, d), row), vec, vec, full(w)]
    out_specs = [pl.BlockSpec((None, tm, n), row)]
    out_shape = [jax.ShapeDtypeStruct((bsz, t, n), _F32)]
    args = [x, a, s, w]
    body = _pre_kernel
    if decay is not None:
        body = _pre_even_kernel
        n_la = decay[1].shape[1]
        in_specs += [full(v) for v in decay]
        out_specs.append(pl.BlockSpec((None, tm, n_la), row))
        out_shape.append(jax.ShapeDtypeStruct((bsz, t, n_la), _F32))
        args += list(decay)
    return pl.pallas_call(
        body, grid=(bsz, t // tm), in_specs=in_specs, out_specs=out_specs, out_shape=out_shape,
        compiler_params=_cparams("parallel", "parallel"), name="pre_proj",
    )(*args)


def _cumsum_rows(tri, x):
    hi = x.astype(_MXU)
    r1 = x - hi.astype(_F32)
    mid = r1.astype(_MXU)
    lo = (r1 - mid.astype(_F32)).astype(_MXU)
    acc = jnp.dot(tri, hi, preferred_element_type=_F32)
    acc = acc + jnp.dot(tri, mid, preferred_element_type=_F32)
    return acc + jnp.dot(tri, lo, preferred_element_type=_F32)


def _gla_chunk(qkv_ref, la_ref, o_ref, st_ref, r0, keep, scale, reverse):
    c = GLA_CHUNK
    rows = pl.ds(r0, c)
    key = la_ref.shape[-1]
    dv = (qkv_ref.shape[-1] - 2 * key) // GLA_HEADS
    dk = key // GLA_HEADS
    b = _cumsum_rows(keep.astype(_MXU), la_ref[rows, :])
    btot = b[0:1] if reverse else b[c - 1:c]
    q = qkv_ref[rows, 0:key]
    k = qkv_ref[rows, key:2 * key]
    qd = q * (scale * jnp.exp(b))
    kd = k * jnp.exp(-b)
    kr = k * jnp.exp(btot - b)
    dec = jnp.exp(btot)
    pair = 2 * dk
    lane_lo = lax.broadcasted_iota(jnp.int32, (c, pair), 1) < dk
    for p in range(GLA_HEADS // 2):
        lanes = slice(p * pair, (p + 1) * pair)
        qd_p, kd_p, kr_p = qd[:, lanes], kd[:, lanes], kr[:, lanes]
        st = st_ref[p]
        vs = []
        for s in range(2):
            h = 2 * p + s
            qm = jnp.where(lane_lo if s == 0 else jnp.logical_not(lane_lo), qd_p, 0.0)
            att = jnp.where(keep, _dot_nt(qm, kd_p), 0.0)
            v_h = qkv_ref[rows, 2 * key + h * dv:2 * key + (h + 1) * dv]
            vs.append(v_h)
            o_ref[rows, h * dv:(h + 1) * dv] = _dot(att, v_h) + _dot_nt(qm, st)
        km = jnp.concatenate([jnp.where(lane_lo, kr_p, 0.0), jnp.where(lane_lo, 0.0, kr_p)], axis=0)
        st_ref[p] = dec[:, lanes] * st + _dot_tn(jnp.concatenate(vs, axis=0), km)


def _gla_kernel(qkv_f_ref, la_f_ref, qkv_b_ref, la_b_ref, s0f_ref, s0b_ref, of_ref, ob_ref, sf_ref, sb_ref, *, scale):
    i = pl.program_id(1)

    @pl.when(i == 0)
    def _():
        sf_ref[...] = s0f_ref[...]
        sb_ref[...] = s0b_ref[...]

    c = GLA_CHUNK
    n = qkv_f_ref.shape[0] // c
    ri = lax.broadcasted_iota(jnp.int32, (c, c), 0)
    ci = lax.broadcasted_iota(jnp.int32, (c, c), 1)
    keep_f = ri >= ci
    keep_b = ri <= ci

    def body(j, carry):
        _gla_chunk(qkv_f_ref, la_f_ref, of_ref, sf_ref, pl.multiple_of(j * c, c), keep_f, scale, False)
        _gla_chunk(qkv_b_ref, la_b_ref, ob_ref, sb_ref, pl.multiple_of((n - 1 - j) * c, c), keep_b, scale, True)
        return carry

    lax.fori_loop(0, n, body, 0, unroll=min(n, GLA_UNROLL))


def _gla(z, la, s0f, s0b, qkv_block, scale):
    bsz, t, _ = z.shape
    key = la.shape[-1] // 2
    st_shape = s0f.shape[1:]
    val = GLA_HEADS * st_shape[1]
    width = 2 * key + val
    tb = _row_tile(t, 512)
    nt = t // tb
    st_spec = pl.BlockSpec((None,) + st_shape, lambda b, i: (b, 0, 0, 0))
    outs = pl.pallas_call(
        functools.partial(_gla_kernel, scale=scale),
        grid=(bsz, nt),
        in_specs=[
            pl.BlockSpec((None, tb, width), lambda b, i: (b, i, qkv_block)),
            pl.BlockSpec((None, tb, key), lambda b, i: (b, i, 0)),
            pl.BlockSpec((None, tb, width), lambda b, i: (b, nt - 1 - i, qkv_block)),
            pl.BlockSpec((None, tb, key), lambda b, i: (b, nt - 1 - i, 1)),
            st_spec, st_spec,
        ],
        out_specs=[
            pl.BlockSpec((None, tb, val), lambda b, i: (b, i, 0)),
            pl.BlockSpec((None, tb, val), lambda b, i: (b, nt - 1 - i, 0)),
            st_spec, st_spec,
        ],
        out_shape=[
            jax.ShapeDtypeStruct((bsz, t, val), _F32),
            jax.ShapeDtypeStruct((bsz, t, val), _F32),
            jax.ShapeDtypeStruct(s0f.shape, _F32),
            jax.ShapeDtypeStruct(s0b.shape, _F32),
        ],
        compiler_params=_cparams("parallel", "arbitrary"),
        name="gla_scan",
    )(z, la, z, la, s0f, s0b)
    return outs


def _lru_coeffs(x_ref, prev_ref, next_ref, first, last, cw_ref, cb_ref, wg_ref, ba_ref, bx_ref, sp_ref):
    x = x_ref[...]
    tm, width = x.shape
    hd = width // LRU_HEADS
    prev = jnp.where(first, 0.0, prev_ref[...])
    nxt = jnp.where(last, 0.0, next_ref[...])
    r = lax.broadcasted_iota(jnp.int32, (tm, 1), 0)
    x_m1 = jnp.where(r == 0, prev[SUBLANES - 1:SUBLANES], pltpu.roll(x, 1, 0))
    x_m2 = jnp.where(r == 0, prev[SUBLANES - 2:SUBLANES - 1],
                     jnp.where(r == 1, prev[SUBLANES - 1:SUBLANES], pltpu.roll(x, 2, 0)))
    x_p1 = jnp.where(r == tm - 1, nxt[0:1], pltpu.roll(x, tm - 1, 0))
    xc = cb_ref[...] + cw_ref[0:1] * x_m2 + cw_ref[1:2] * x_m1 + cw_ref[2:3] * x + cw_ref[3:4] * x_p1
    a_parts, b_parts = [], []
    for h in range(LRU_HEADS):
        lanes = slice(h * hd, (h + 1) * hd)
        xh = xc[:, lanes]
        g = _dot(xh, wg_ref[h])
        rg = jax.nn.sigmoid(g[:, :hd] + ba_ref[:, lanes])
        ig = jax.nn.sigmoid(g[:, hd:] + bx_ref[:, lanes])
        log_a = rg * sp_ref[:, lanes]
        th = jnp.tanh(0.5 * log_a)
        rcp = 1.0 / (1.0 - th)
        a = (1.0 + th) * rcp
        a_parts.append(a)
        b_parts.append(jnp.sqrt(-2.0 * th * rcp * (1.0 + a)) * (ig * xh))
    return jnp.concatenate(a_parts, axis=-1), jnp.concatenate(b_parts, axis=-1)


def _lru_scan_tile(a, bb, a_scr, b_scr, h_ref, carry_ref, reverse):
    tm, width = a.shape
    rr = lax.broadcasted_iota(jnp.int32, (tm, 1), 0) % SUBLANES
    d = 1
    while d < SUBLANES:
        if reverse:
            valid = rr < SUBLANES - d
            shift = tm - d
        else:
            valid = rr >= d
            shift = d
        a_s = jnp.where(valid, pltpu.roll(a, shift, 0), 1.0)
        b_s = jnp.where(valid, pltpu.roll(bb, shift, 0), 0.0)
        bb = a * b_s + bb
        a = a * a_s
        d *= 2
    a_scr[...] = a
    b_scr[...] = bb
    groups = tm // SUBLANES
    edge = 0 if reverse else SUBLANES - 1

    def body(j, h):
        g = (groups - 1 - j) if reverse else j
        rows = pl.ds(pl.multiple_of(g * SUBLANES, SUBLANES), SUBLANES)
        hg = a_scr[rows, :] * h + b_scr[rows, :]
        h_ref[rows, :] = hg
        return hg[edge:edge + 1]

    carry_ref[...] = lax.fori_loop(0, groups, body, carry_ref[...], unroll=4)


def _lru_kernel(xf_ref, xf_prev_ref, xf_next_ref, xb_ref, xb_prev_ref, xb_next_ref, cw_ref, cb_ref, wg_ref, ba_ref,
                bx_ref, sp_ref, h0f_ref, h0b_ref, hf_ref, hb_ref, cf_ref, cbk_ref, a_scr, b_scr):
    i = pl.program_id(1)
    nt = pl.num_programs(1)

    @pl.when(i == 0)
    def _():
        cf_ref[...] = h0f_ref[...]
        cbk_ref[...] = h0b_ref[...]

    a, bb = _lru_coeffs(xf_ref, xf_prev_ref, xf_next_ref, i == 0, i == nt - 1, cw_ref, cb_ref, wg_ref.at[0],
                        ba_ref.at[0], bx_ref.at[0], sp_ref.at[0])
    _lru_scan_tile(a, bb, a_scr, b_scr, hf_ref, cf_ref, False)
    a, bb = _lru_coeffs(xb_ref, xb_prev_ref, xb_next_ref, i == nt - 1, i == 0, cw_ref, cb_ref, wg_ref.at[1],
                        ba_ref.at[1], bx_ref.at[1], sp_ref.at[1])
    _lru_scan_tile(a, bb, a_scr, b_scr, hb_ref, cbk_ref, True)


def _lru(z, x_block, conv_w, conv_b, wg, ba, bx, sp, h0f, h0b):
    bsz, t, _ = z.shape
    width = conv_w.shape[1]
    tm = _row_tile(t)
    nt = t // tm
    gpt = tm // SUBLANES
    ngr = t // SUBLANES

    def tile(rev):
        return lambda b, i: (b, (nt - 1 - i) if rev else i, x_block)

    def prev(rev):
        return lambda b, i: (b, jnp.maximum(((nt - 1 - i) if rev else i) * gpt - 1, 0), x_block)

    def nxt(rev):
        return lambda b, i: (b, jnp.minimum((((nt - 1 - i) if rev else i) + 1) * gpt, ngr - 1), x_block)

    full = lambda arr: pl.BlockSpec(arr.shape, lambda b, i: (0,) * arr.ndim)
    st_spec = pl.BlockSpec((None, 1, width), lambda b, i: (b, 0, 0))
    in_specs = []
    for rev in (False, True):
        in_specs += [pl.BlockSpec((None, tm, width), tile(rev)),
                     pl.BlockSpec((None, SUBLANES, width), prev(rev)),
                     pl.BlockSpec((None, SUBLANES, width), nxt(rev))]
    in_specs += [full(conv_w), full(conv_b), full(wg), full(ba), full(bx), full(sp), st_spec, st_spec]
    return pl.pallas_call(
        _lru_kernel,
        grid=(bsz, nt),
        in_specs=in_specs,
        out_specs=[
            pl.BlockSpec((None, tm, width), lambda b, i: (b, i, 0)),
            pl.BlockSpec((None, tm, width), lambda b, i: (b, nt - 1 - i, 0)),
            st_spec, st_spec,
        ],
        out_shape=[
            jax.ShapeDtypeStruct((bsz, t, width), _F32),
            jax.ShapeDtypeStruct((bsz, t, width), _F32),
            jax.ShapeDtypeStruct((bsz, 1, width), _F32),
            jax.ShapeDtypeStruct((bsz, 1, width), _F32),
        ],
        scratch_shapes=[pltpu.VMEM((tm, width), _F32), pltpu.VMEM((tm, width), _F32)],
        compiler_params=_cparams("parallel", "arbitrary"),
        name="lru_scan",
    )(z, z, z, z, z, z, conv_w, conv_b, wg, ba, bx, sp, h0f, h0b)


def _residual_ffn_prenorm(y, x_ref, gate_ref, a_ref, s_ref, wr_ref, xo_ref, h_ref, lg_ref):
    xn = x_ref[...] + gate_ref[...] * y
    xo_ref[...] = xn
    h = _norm_mod(xn, a_ref[...], s_ref[...]).astype(_MXU)
    h_ref[...] = h
    lg_ref[...] = _dot(h, wr_ref[...])


def _post_even_kernel(za_ref, of_ref, ob_ref, cw_ref, cb_ref, ng_ref, w_ref, x_ref, gate_ref, a_ref, s_ref, wr_ref,
                      xo_ref, h_ref, lg_ref, y_scr, *, row_len):
    tm = za_ref.shape[0]
    scw = cw_ref.shape[1]
    val = ng_ref.shape[1]
    dv = val // GLA_HEADS
    bg = za_ref[:, 0:scw]
    cx = za_ref[:, scw:2 * scw] * za_ref[:, 2 * scw:3 * scw]
    r = lax.broadcasted_iota(jnp.int32, (tm, 1), 0) % row_len
    left = jnp.where(r == 0, 0.0, pltpu.roll(cx, 1, 0))
    right = jnp.where(r == row_len - 1, 0.0, pltpu.roll(cx, tm - 1, 0))
    conv = cb_ref[...] + cw_ref[0:1] * left + cw_ref[1:2] * cx + cw_ref[2:3] * right
    acc = _dot(bg * conv, w_ref[0:scw, :])
    for h in range(GLA_HEADS):
        lanes = slice(h * dv, (h + 1) * dv)
        o = of_ref[:, lanes] + ob_ref[:, lanes]
        o = o * lax.rsqrt(jnp.mean(o * o, axis=-1, keepdims=True) + NORM_EPS)
        y_scr[:, lanes] = o * ng_ref[:, lanes] * _silu(za_ref[:, 3 * scw + h * dv:3 * scw + (h + 1) * dv])
    acc = acc + _dot(y_scr[...], w_ref[scw:scw + val, :])
    _residual_ffn_prenorm(acc, x_ref, gate_ref, a_ref, s_ref, wr_ref, xo_ref, h_ref, lg_ref)


def _post_odd_kernel(zc_ref, hf_ref, hb_ref, lng_ref, sgw_ref, sgb_ref, w_ref, x_ref, gate_ref, a_ref, s_ref, wr_ref,
                     xo_ref, h_ref, lg_ref, y_scr):
    tm = zc_ref.shape[0]
    width = hf_ref.shape[1]
    gdim = width // SG_GROUPS
    y_rec = (hf_ref[...] + hb_ref[...]) * _gelu(zc_ref[:, 0:width])
    acc = _dot(y_rec, w_ref[0:width, :])
    gv = _gelu(zc_ref[:, 2 * width:3 * width])
    gc = gv - jnp.mean(gv, axis=-1, keepdims=True)
    vb = gc * lax.rsqrt(jnp.mean(gc * gc, axis=-1, keepdims=True) + NORM_EPS) * lng_ref[...]
    for n in range(tm // SG_CHUNK):
        rows = slice(n * SG_CHUNK, (n + 1) * SG_CHUNK)
        for g in range(SG_GROUPS):
            lanes = slice(g * gdim, (g + 1) * gdim)
            mixed = _dot(sgw_ref[g], vb[rows, lanes]) + sgb_ref[g]
            y_scr[rows, lanes] = _gelu(zc_ref[rows, width + g * gdim:width + (g + 1) * gdim]) * mixed
    acc = acc + _dot(y_scr[...], w_ref[width:2 * width, :])
    _residual_ffn_prenorm(acc, x_ref, gate_ref, a_ref, s_ref, wr_ref, xo_ref, h_ref, lg_ref)


def _post(body, z, z_width, seq_in, consts, w_out, x, gate, a, s, wr):
    bsz, t, d = x.shape
    tm = _row_tile(t, PROJ_ROWS)
    row = lambda b, i: (b, i, 0)
    vec = pl.BlockSpec((None, 1, d), lambda b, i: (b, 0, 0))
    full = lambda arr: pl.BlockSpec(arr.shape, lambda b, i: (0,) * arr.ndim)
    width = seq_in[0].shape[-1]
    in_specs = [pl.BlockSpec((None, tm, z_width), row)]
    in_specs += [pl.BlockSpec((None, tm, width), row) for _ in seq_in]
    in_specs += [full(v) for v in consts] + [full(w_out), pl.BlockSpec((None, tm, d), row), vec, vec, vec, full(wr)]
    return pl.pallas_call(
        body,
        grid=(bsz, t // tm),
        in_specs=in_specs,
        out_specs=[
            pl.BlockSpec((None, tm, d), row),
            pl.BlockSpec((None, tm, d), row),
            pl.BlockSpec((None, tm, LANES), row),
        ],
        out_shape=[
            jax.ShapeDtypeStruct((bsz, t, d), _F32),
            jax.ShapeDtypeStruct((bsz, t, d), _MXU),
            jax.ShapeDtypeStruct((bsz, t, LANES), _F32),
        ],
        scratch_shapes=[pltpu.VMEM((tm, width), _F32)],
        compiler_params=_cparams("parallel", "parallel"),
        name="post_proj",
    )(z, *seq_in, *consts, w_out, x, gate, a, s, wr)


def _moe_ffn_kernel(x_ref, g_ref, wg_ref, wu_ref, wd_ref, o_ref, wg_scr, wu_scr, wd_scr):
    @pl.when(pl.program_id(1) == 0)
    def _():
        wg_scr[...] = wg_ref[...].astype(_MXU)
        wu_scr[...] = wu_ref[...].astype(_MXU)
        wd_scr[...] = wd_ref[...].astype(_MXU)

    bb, cap, d = x_ref.shape
    x = x_ref[...].reshape(bb * cap, d)
    gate = _dot(x, wg_scr[...])
    up = _dot(x, wu_scr[...])
    out = _dot(_silu(gate) * up, wd_scr[...]) * g_ref[...].reshape(bb * cap, 1)
    o_ref[...] = out.reshape(bb, cap, d).astype(o_ref.dtype)


def _moe_ffn(xg, g, wg, wu, wd, layer):
    bsz, ne, cap, d = xg.shape
    f = wg.shape[-1]
    bb = max(1, min(bsz, MOE_ROWS // cap))
    assert bsz % bb == 0
    tok = lambda e, b: (b, e, 0, 0)
    wsel = lambda e, b: (layer, e, 0, 0)
    return pl.pallas_call(
        _moe_ffn_kernel,
        grid=(ne, bsz // bb),
        in_specs=[
            pl.BlockSpec((bb, None, cap, d), tok),
            pl.BlockSpec((bb, None, cap, 1), tok),
            pl.BlockSpec((None, None, d, f), wsel),
            pl.BlockSpec((None, None, d, f), wsel),
            pl.BlockSpec((None, None, f, d), wsel),
        ],
        out_specs=pl.BlockSpec((bb, None, cap, d), tok),
        out_shape=jax.ShapeDtypeStruct((bsz, ne, cap, d), _MXU),
        scratch_shapes=[pltpu.VMEM((d, f), _MXU), pltpu.VMEM((d, f), _MXU), pltpu.VMEM((f, d), _MXU)],
        compiler_params=pltpu.CompilerParams(dimension_semantics=("parallel", "arbitrary"),
                                             vmem_limit_bytes=MOE_VMEM_LIMIT),
        name="moe_ffn",
    )(xg, g, wg, wu, wd)


def _combine_kernel(lo_ref, hi_ref, tok_ref, y_ref, x_ref, gate_ref, fg_ref, o_ref, acc_ref, *, final_norm):
    b = pl.program_id(0)
    j = pl.program_id(1)
    tm = acc_ref.shape[0]
    ch = tok_ref.shape[-1]
    acc_ref[...] = jnp.zeros_like(acc_ref)
    rows = j * tm + lax.broadcasted_iota(jnp.int32, (tm, ch), 0)

    def body(c, carry):
        onehot = jnp.where(rows == tok_ref[c], 1.0, 0.0)
        acc_ref[...] += _dot(onehot, y_ref[pl.ds(pl.multiple_of(c * ch, ch), ch), :])
        return carry

    lax.fori_loop(lo_ref[b, j], hi_ref[b, j] + 1, body, 0)
    xn = x_ref[...] + gate_ref[...] * acc_ref[...]
    if final_norm:
        ms = jnp.mean(xn * xn, axis=-1, keepdims=True)
        xn = xn * lax.rsqrt(ms + NORM_EPS) * fg_ref[...]
    o_ref[...] = xn


def _combine(ys, toks, x, gate, final_g):
    bsz, t, d = x.shape
    p = ys.shape[1]
    tm = _row_tile(t)
    ch = _row_tile(p)
    nt, nc = t // tm, p // ch
    bounds = jnp.arange(nt + 1, dtype=jnp.int32) * tm
    cut = jax.vmap(lambda row: jnp.searchsorted(row, bounds, side="left"))(toks).astype(jnp.int32)
    c_lo = jnp.minimum(cut[:, :-1] // ch, nc - 1)
    c_hi = jnp.maximum((cut[:, 1:] - 1) // ch, c_lo)
    fg = jnp.ones((1, d), _F32) if final_g is None else final_g.reshape(1, d)
    grid_spec = pltpu.PrefetchScalarGridSpec(
        num_scalar_prefetch=2,
        grid=(bsz, nt),
        in_specs=[
            pl.BlockSpec((None, nc, 1, ch), lambda b, j, lo, hi: (b, 0, 0, 0)),
            pl.BlockSpec((None, p, d), lambda b, j, lo, hi: (b, 0, 0)),
            pl.BlockSpec((None, tm, d), lambda b, j, lo, hi: (b, j, 0)),
            pl.BlockSpec((None, 1, d), lambda b, j, lo, hi: (b, 0, 0)),
            pl.BlockSpec((1, d), lambda b, j, lo, hi: (0, 0)),
        ],
        out_specs=pl.BlockSpec((None, tm, d), lambda b, j, lo, hi: (b, j, 0)),
        scratch_shapes=[pltpu.VMEM((tm, d), _F32)],
    )
    return pl.pallas_call(
        functools.partial(_combine_kernel, final_norm=final_g is not None),
        grid_spec=grid_spec,
        out_shape=jax.ShapeDtypeStruct((bsz, t, d), _F32),
        compiler_params=_cparams("parallel", "arbitrary"),
        name="moe_combine",
    )(c_lo, c_hi, toks.reshape(bsz, nc, 1, ch), ys, x, gate, fg)


def _ec_moe(x, gate, h, logits, wg, wu, wd, layer, final_g):
    bsz, t, d = h.shape
    ne = logits.shape[-1]
    cap = EC_CAPACITY * t // ne
    aff = jax.nn.softmax(logits, axis=-1)
    g, idx = lax.top_k(jnp.swapaxes(aff, 1, 2), cap)
    xg = jax.vmap(lambda hb, ib: hb[ib])(h, idx)
    y = _moe_ffn(xg, g[..., None], wg, wu, wd, layer).reshape(bsz, ne * cap, d)
    slots = jnp.broadcast_to(jnp.arange(ne * cap, dtype=jnp.int32), (bsz, ne * cap))
    toks, perm = lax.sort_key_val(idx.reshape(bsz, ne * cap).astype(jnp.int32), slots, dimension=-1)
    ys = jax.vmap(lambda yb, pb: yb[pb])(y, perm)
    return _combine(ys, toks, x, gate, final_g)


def _pad_to(w, axis, size):
    pad = [(0, 0)] * w.ndim
    pad[axis] = (0, size - w.shape[axis])
    return jnp.pad(w, pad)


def kernel(x, c, ctx, c_ctx, ada_w, ada_b, norm_mix_g, norm_ffn_g, norm_final_g, ev_w_in, ev_conv_w, ev_conv_b, ev_decay_up, ev_decay_b, ev_norm_g, ev_w_out, od_w_in, od_conv_w, od_conv_b, od_gate_a_w, od_gate_a_b, od_gate_x_w, od_gate_x_b, od_lambda, od_sg_norm_g, od_sg_w, od_sg_b, od_w_out, moe_router, moe_w_gate, moe_w_up, moe_w_down):
    depth = ada_w.shape[0]
    bsz, seq, d = x.shape
    ne = moe_router.shape[-1]
    x_lat, x_ctx = x, ctx
    cond = _pad_to(jnp.concatenate([c, c_ctx[None, :]], axis=0), 0, -(-(bsz + 1) // SUBLANES) * SUBLANES)
    ada = _adaln(cond, ada_w, ada_b.reshape(depth, 1, N_MOD * d))

    def vecs(m):
        return [v.reshape(bsz, 1, d) for v in m]

    for l in range(depth):
        last = l == depth - 1
        even = l % 2 == 0
        i = l // 2
        mod = [ada[l, :bsz, k * d:(k + 1) * d] for k in range(N_MOD)]
        mod_c = [jnp.broadcast_to(ada[l, bsz:bsz + 1, k * d:(k + 1) * d], (bsz, d)) for k in range(N_MOD)]
        wr = _pad_to(moe_router[l], 1, LANES).astype(_MXU)

        if even:
            w = ev_w_in[i]
            scw = ev_conv_w.shape[-1]
            key = ev_decay_up.shape[-1]
            val = ev_norm_g.shape[-1]
            o_q = 3 * scw
            o_g = o_q + key
            o_k = o_g + val
            o_v = o_k + key
            o_lr = o_v + val
            w_in = jnp.concatenate([w[:, :o_q], w[:, o_g:o_k], w[:, o_q:o_g], w[:, o_k:o_lr]], axis=1).astype(_MXU)
            w_lr = _pad_to(w[:, o_lr:], 1, LANES).astype(_MXU)
            up = jnp.zeros((LANES, 2 * key), _F32)
            up = up.at[:GLA_RANK, :key].set(ev_decay_up[i, 0]).at[GLA_RANK:2 * GLA_RANK, key:].set(ev_decay_up[i, 1])
            decay = (w_lr, up.astype(_MXU), ev_decay_b[i].reshape(1, 2 * key))
            w_out = ev_w_out[i].astype(_MXU)
            conv_w = _pad_to(ev_conv_w[i], 0, SUBLANES)
            consts = (conv_w, ev_conv_b[i].reshape(1, scw), ev_norm_g[i].reshape(1, val))
            epi_width = 3 * scw + val
            assert (3 * scw + val) % (2 * key + val) == 0
            qkv_block = epi_width // (2 * key + val)
            dk, dv = key // GLA_HEADS, val // GLA_HEADS
            scale = dk ** -0.5
            zero_st = jnp.zeros((bsz, GLA_HEADS // 2, dv, 2 * dk), _F32)

            def mixer(xs, m, s0, row_len):
                sh, sc = vecs(m[:2])
                z, la = _pre(xs, norm_mix_g[l] * (1.0 + sc), sh, w_in, decay)
                o_f, o_b, s_f, s_b = _gla(z, la, s0[0], s0[1], qkv_block, scale)
                body = functools.partial(_post_even_kernel, row_len=row_len)
                return (body, z, epi_width, (o_f, o_b), consts, w_out), (s_f, s_b)

            ctx_args, state = mixer(x_ctx, mod_c, (zero_st, zero_st), x_ctx.shape[1])
            lat_args, _ = mixer(x_lat, mod, state, GRID_W)
        else:
            w = od_w_in[i]
            width = od_conv_w.shape[-1]
            hd = width // LRU_HEADS
            w_in = jnp.concatenate([w[:, width:], w[:, :width]], axis=1).astype(_MXU)
            w_out = od_w_out[i].astype(_MXU)
            conv_w = _pad_to(od_conv_w[i], 0, SUBLANES)
            conv_b = od_conv_b[i].reshape(1, width)
            wgate = jnp.concatenate([od_gate_a_w[i], od_gate_x_w[i]], axis=-1).astype(_MXU)
            ba = od_gate_a_b[i].reshape(2, 1, width)
            bx = od_gate_x_b[i].reshape(2, 1, width)
            sp = (-LRU_C * jax.nn.softplus(-od_lambda[i])).reshape(2, 1, width)
            sgb = jnp.broadcast_to(od_sg_b[i][:, :, None], (SG_GROUPS, SG_CHUNK, width // SG_GROUPS))
            consts = (od_sg_norm_g[i].reshape(1, width), od_sg_w[i].astype(_MXU), sgb)
            zero_st = jnp.zeros((bsz, 1, width), _F32)

            def mixer(xs, m, s0, row_len):
                sh, sc = vecs(m[:2])
                (z,) = _pre(xs, norm_mix_g[l] * (1.0 + sc), sh, w_in)
                h_f, h_b, c_f, c_b = _lru(z, 3, conv_w, conv_b, wgate, ba, bx, sp, s0[0], s0[1])
                return (_post_odd_kernel, z, 3 * width, (h_f, h_b), consts, w_out), (c_f, c_b)

            ctx_args, state = mixer(x_ctx, mod_c, (zero_st, zero_st), None)
            lat_args, _ = mixer(x_lat, mod, state, None)

        def post(args, xs, m, final_g):
            gate1, sh, sc, gate2 = vecs(m[2:])
            xs, h, logits = _post(*args, xs, gate1, norm_ffn_g[l] * (1.0 + sc), sh, wr)
            return _ec_moe(xs, gate2, h, logits[..., :ne], moe_w_gate, moe_w_up, moe_w_down, l, final_g)

        x_lat = post(lat_args, x_lat, mod, norm_final_g if last else None)
        if not last:
            x_ctx = post(ctx_args, x_ctx, mod_c, None)
    return x_lat
```

```python
import functools

import jax
import jax.numpy as jnp
from jax import lax
from jax.experimental import pallas as pl
from jax.experimental.pallas import tpu as pltpu

_MXU = jnp.bfloat16
_F32 = jnp.float32
NORM_EPS = 1e-6
N_MOD = 6
GRID_W = 64
GLA_HEADS = 4
GLA_RANK = 16
GLA_TAU = 16.0
GLA_CHUNK = 64
LRU_HEADS = 4
LRU_C = 8.0
SG_GROUPS = 4
SG_CHUNK = 128
EC_CAPACITY = 2
LANES = 128
SUBLANES = 8
VMEM_LIMIT = 48 * 1024 * 1024
MOE_VMEM_LIMIT = 56 * 1024 * 1024
MOE_ROWS = 512
PROJ_ROWS = 512
LRU_ROWS = 1024


def _cparams(*sem):
    return pltpu.CompilerParams(dimension_semantics=sem, vmem_limit_bytes=VMEM_LIMIT)


def _row_tile(t, cap=256):
    tm = min(t, cap)
    assert t % tm == 0
    return tm


def _dot(a, b):
    return jnp.dot(a.astype(_MXU), b.astype(_MXU), preferred_element_type=_F32)


def _dot_nt(a, b):
    return lax.dot_general(a.astype(_MXU), b.astype(_MXU), (((1,), (1,)), ((), ())), preferred_element_type=_F32)


def _dot_tn(a, b):
    return lax.dot_general(a.astype(_MXU), b.astype(_MXU), (((0,), (0,)), ((), ())), preferred_element_type=_F32)


def _gelu(x):
    return 0.5 * x * (1.0 + jnp.tanh(0.7978845608028654 * (x + 0.044715 * (x * x * x))))


def _silu(x):
    return x * jax.nn.sigmoid(x)


def _log_sigmoid(x):
    return jnp.minimum(x, 0.0) - jnp.log(1.0 + jnp.exp(-jnp.abs(x)))


def _norm_mod(x, a, s):
    ms = jnp.mean(x * x, axis=-1, keepdims=True)
    return x * lax.rsqrt(ms + NORM_EPS) * a + s


def _adaln_kernel(c_ref, w_ref, b_ref, o_ref):
    o_ref[...] = _dot(_silu(c_ref[...]), w_ref[...]) + b_ref[...]


def _adaln(cond, w, b):
    depth, d, n = w.shape
    r = cond.shape[0]
    tn = n // N_MOD
    return pl.pallas_call(
        _adaln_kernel,
        grid=(depth, N_MOD),
        in_specs=[
            pl.BlockSpec((r, d), lambda l, j: (0, 0)),
            pl.BlockSpec((None, d, tn), lambda l, j: (l, 0, j)),
            pl.BlockSpec((None, 1, tn), lambda l, j: (l, 0, j)),
        ],
        out_specs=pl.BlockSpec((None, r, tn), lambda l, j: (l, 0, j)),
        out_shape=jax.ShapeDtypeStruct((depth, r, n), _F32),
        compiler_params=_cparams("parallel", "parallel"),
        name="adaln",
    )(cond, w, b)


def _pre_kernel(x_ref, a_ref, s_ref, w_ref, o_ref):
    h = _norm_mod(x_ref[...], a_ref[...], s_ref[...])
    o_ref[...] = _dot(h, w_ref[...])


def _pre_even_kernel(x_ref, a_ref, s_ref, w_ref, wlr_ref, up_ref, db_ref, o_ref, la_ref):
    h = _norm_mod(x_ref[...], a_ref[...], s_ref[...]).astype(_MXU)
    o_ref[...] = _dot(h, w_ref[...])
    lr = _dot(h, wlr_ref[...])
    la_ref[...] = _log_sigmoid(_dot(lr, up_ref[...]) + db_ref[...]) * (1.0 / GLA_TAU)


def _pre(x, a, s, w, decay=None):
    bsz, t, d = x.shape
    n = w.shape[1]
    tm = _row_tile(t, PROJ_ROWS)
    row = lambda b, i: (b, i, 0)
    vec = pl.BlockSpec((None, 1, d), lambda b, i: (b, 0, 0))
    full = lambda arr: pl.BlockSpec(arr.shape, lambda b, i: (0,) * arr.ndim)
    in_specs = [pl.BlockSpec((None, tm, d), row), vec, vec, full(w)]
    out_specs = [pl.BlockSpec((None, tm, n), row)]
    out_shape = [jax.ShapeDtypeStruct((bsz, t, n), _F32)]
    args = [x, a, s, w]
    body = _pre_kernel
    if decay is not None:
        body = _pre_even_kernel
        n_la = decay[1].shape[1]
        in_specs += [full(v) for v in decay]
        out_specs.append(pl.BlockSpec((None, tm, n_la), row))
        out_shape.append(jax.ShapeDtypeStruct((bsz, t, n_la), _F32))
        args += list(decay)
    return pl.pallas_call(
        body, grid=(bsz, t // tm), in_specs=in_specs, out_specs=out_specs, out_shape=out_shape,
        compiler_params=_cparams("parallel", "parallel"), name="pre_proj",
    )(*args)


def _bmm(a, b, dims):
    return lax.dot_general(a.astype(_MXU), b.astype(_MXU), (dims, ((0,), (0,))), preferred_element_type=_F32)


def _cumsum_chunks(tri, x):
    hi = x.astype(_MXU)
    r1 = x - hi.astype(_F32)
    mid = r1.astype(_MXU)
    lo = (r1 - mid.astype(_F32)).astype(_MXU)
    mm = ((2,), (1,))
    return _bmm(tri, hi, mm) + _bmm(tri, mid, mm) + _bmm(tri, lo, mm)


def _gla_block(qkv_ref, la_ref, o_ref, st_ref, sin_ref, keep, scale, reverse):
    c = GLA_CHUNK
    tb = qkv_ref.shape[0]
    n = tb // c
    key = la_ref.shape[-1]
    dv = (qkv_ref.shape[-1] - 2 * key) // GLA_HEADS
    dk = key // GLA_HEADS
    tri = jnp.broadcast_to(keep.astype(_MXU)[None], (n, c, c))
    b = _cumsum_chunks(tri, la_ref[...].reshape(n, c, key))
    btot = b[:, 0:1] if reverse else b[:, c - 1:c]
    q = qkv_ref[:, 0:key].reshape(n, c, key)
    k = qkv_ref[:, key:2 * key].reshape(n, c, key)
    qd = q * (scale * jnp.exp(b))
    kd = k * jnp.exp(-b)
    kr = k * jnp.exp(btot - b)
    dec = jnp.exp(btot)
    pair = 2 * dk
    lane_lo = lax.broadcasted_iota(jnp.int32, (n, c, pair), 2) < dk
    order = range(n - 1, -1, -1) if reverse else range(n)
    for p in range(GLA_HEADS // 2):
        lanes = slice(p * pair, (p + 1) * pair)
        qd_p, kd_p, kr_p = qd[:, :, lanes], kd[:, :, lanes], kr[:, :, lanes]
        vs = [qkv_ref[:, 2 * key + h * dv:2 * key + (h + 1) * dv].reshape(n, c, dv) for h in (2 * p, 2 * p + 1)]
        km = jnp.concatenate([jnp.where(lane_lo, kr_p, 0.0), jnp.where(lane_lo, 0.0, kr_p)], axis=1)
        upd = _bmm(jnp.concatenate(vs, axis=1), km, ((1,), (1,)))
        st = st_ref[p]
        for j in order:
            sin_ref[j] = st
            st = dec[j, :, lanes] * st + upd[j]
        st_ref[p] = st
        s_in = sin_ref[...]
        for s in range(2):
            h = 2 * p + s
            qm = jnp.where(lane_lo if s == 0 else jnp.logical_not(lane_lo), qd_p, 0.0)
            att = jnp.where(keep[None], _bmm(qm, kd_p, ((2,), (2,))), 0.0)
            o = _bmm(att, vs[s], ((2,), (1,))) + _bmm(qm, s_in, ((2,), (2,)))
            o_ref[:, h * dv:(h + 1) * dv] = o.reshape(tb, dv)


def _gla_kernel(qkv_f_ref, la_f_ref, qkv_b_ref, la_b_ref, s0f_ref, s0b_ref, of_ref, ob_ref, sf_ref, sb_ref, sin_ref, *,
                scale):
    i = pl.program_id(1)

    @pl.when(i == 0)
    def _():
        sf_ref[...] = s0f_ref[...]
        sb_ref[...] = s0b_ref[...]

    c = GLA_CHUNK
    ri = lax.broadcasted_iota(jnp.int32, (c, c), 0)
    ci = lax.broadcasted_iota(jnp.int32, (c, c), 1)
    _gla_block(qkv_f_ref, la_f_ref, of_ref, sf_ref, sin_ref, ri >= ci, scale, False)
    _gla_block(qkv_b_ref, la_b_ref, ob_ref, sb_ref, sin_ref, ri <= ci, scale, True)


def _gla(z, la, s0f, s0b, qkv_block, scale):
    bsz, t, _ = z.shape
    key = la.shape[-1] // 2
    st_shape = s0f.shape[1:]
    val = GLA_HEADS * st_shape[1]
    width = 2 * key + val
    tb = _row_tile(t, 512)
    nt = t // tb
    st_spec = pl.BlockSpec((None,) + st_shape, lambda b, i: (b, 0, 0, 0))
    outs = pl.pallas_call(
        functools.partial(_gla_kernel, scale=scale),
        grid=(bsz, nt),
        in_specs=[
            pl.BlockSpec((None, tb, width), lambda b, i: (b, i, qkv_block)),
            pl.BlockSpec((None, tb, key), lambda b, i: (b, i, 0)),
            pl.BlockSpec((None, tb, width), lambda b, i: (b, nt - 1 - i, qkv_block)),
            pl.BlockSpec((None, tb, key), lambda b, i: (b, nt - 1 - i, 1)),
            st_spec, st_spec,
        ],
        out_specs=[
            pl.BlockSpec((None, tb, val), lambda b, i: (b, i, 0)),
            pl.BlockSpec((None, tb, val), lambda b, i: (b, nt - 1 - i, 0)),
            st_spec, st_spec,
        ],
        out_shape=[
            jax.ShapeDtypeStruct((bsz, t, val), _F32),
            jax.ShapeDtypeStruct((bsz, t, val), _F32),
            jax.ShapeDtypeStruct(s0f.shape, _F32),
            jax.ShapeDtypeStruct(s0b.shape, _F32),
        ],
        scratch_shapes=[pltpu.VMEM((tb // GLA_CHUNK,) + st_shape[1:], _F32)],
        compiler_params=_cparams("parallel", "arbitrary"),
        name="gla_scan",
    )(z, la, z, la, s0f, s0b)
    return outs


def _lru_coeffs(x_ref, prev_ref, next_ref, first, last, cw_ref, cb_ref, wg_ref, ba_ref, bx_ref, sp_ref, pad_scr):
    tm, hd = x_ref.shape
    steps = tm // SUBLANES
    pitch = pad_scr.shape[0] // SUBLANES
    for sg in range(SUBLANES):
        pad_scr[sg * pitch:sg * pitch + steps, :] = x_ref[sg * steps:(sg + 1) * steps, :]
    x = jnp.stack([pad_scr[pl.ds(s, SUBLANES, stride=pitch), :] for s in range(steps)], axis=0)
    prev = jnp.where(first, 0.0, prev_ref[...])
    nxt = jnp.where(last, 0.0, next_ref[...])
    seg = lax.broadcasted_iota(jnp.int32, (SUBLANES, 1), 0)

    def from_prev_segment(v, fill):
        return jnp.where(seg == 0, fill, pltpu.roll(v, 1, 0))

    def from_next_segment(v, fill):
        return jnp.where(seg == SUBLANES - 1, fill, pltpu.roll(v, SUBLANES - 1, 0))

    m1_edge = from_prev_segment(x[steps - 1], prev[SUBLANES - 1:SUBLANES])
    m2_edge = from_prev_segment(x[steps - 2], prev[SUBLANES - 2:SUBLANES - 1])
    p1_edge = from_next_segment(x[0], nxt[0:1])
    x_m1 = jnp.concatenate([m1_edge[None], x[:steps - 1]], axis=0)
    x_m2 = jnp.concatenate([m2_edge[None], m1_edge[None], x[:steps - 2]], axis=0)
    x_p1 = jnp.concatenate([x[1:], p1_edge[None]], axis=0)
    xc = cb_ref[...] + cw_ref[0:1] * x_m2 + cw_ref[1:2] * x_m1 + cw_ref[2:3] * x + cw_ref[3:4] * x_p1
    xc = xc.reshape(tm, hd)
    g = _dot(xc, wg_ref[...])
    rg = jax.nn.sigmoid(g[:, :hd] + ba_ref[...])
    ig = jax.nn.sigmoid(g[:, hd:] + bx_ref[...])
    log_a = rg * sp_ref[...]
    th = jnp.tanh(0.5 * log_a)
    rcp = 1.0 / (1.0 - th)
    a = (1.0 + th) * rcp
    bb = jnp.sqrt(-2.0 * th * rcp * (1.0 + a)) * (ig * xc)
    shape = (steps, SUBLANES, hd)
    return a.reshape(shape), bb.reshape(shape)


def _lru_scan_tile(a, bb, h_scr, p_scr, c_scr, pad_scr, h_ref, carry_ref, reverse):
    steps = a.shape[0]
    pitch = pad_scr.shape[0] // SUBLANES
    order = range(steps - 1, -1, -1) if reverse else range(steps)
    h = None
    for s in order:
        h = bb[s] if h is None else a[s] * h + bb[s]
        prod = a[s] if s == order[0] else a[s] * prod
        h_scr[s] = h
        p_scr[s] = prod
    c = carry_ref[...]
    for sg in (range(SUBLANES - 1, -1, -1) if reverse else range(SUBLANES)):
        c_scr[sg:sg + 1, :] = c
        c = h[sg:sg + 1] + prod[sg:sg + 1] * c
    carry_ref[...] = c
    c_in = c_scr[...]
    for s in range(steps):
        pad_scr[pl.ds(s, SUBLANES, stride=pitch), :] = h_scr[s] + p_scr[s] * c_in
    for sg in range(SUBLANES):
        h_ref[sg * steps:(sg + 1) * steps, :] = pad_scr[sg * pitch:sg * pitch + steps, :]


def _lru_kernel(xf_ref, xf_prev_ref, xf_next_ref, xb_ref, xb_prev_ref, xb_next_ref, cw_ref, cb_ref, wg_ref, ba_ref,
                bx_ref, sp_ref, h0f_ref, h0b_ref, hf_ref, hb_ref, cf_ref, cbk_ref, a_scr, b_scr, c_scr, pad_scr):
    i = pl.program_id(2)
    nt = pl.num_programs(2)

    @pl.when(i == 0)
    def _():
        cf_ref[...] = h0f_ref[...]
        cbk_ref[...] = h0b_ref[...]

    a, bb = _lru_coeffs(xf_ref, xf_prev_ref, xf_next_ref, i == 0, i == nt - 1, cw_ref, cb_ref, wg_ref.at[0],
                        ba_ref.at[0], bx_ref.at[0], sp_ref.at[0], pad_scr)
    _lru_scan_tile(a, bb, a_scr, b_scr, c_scr, pad_scr, hf_ref, cf_ref, False)
    a, bb = _lru_coeffs(xb_ref, xb_prev_ref, xb_next_ref, i == nt - 1, i == 0, cw_ref, cb_ref, wg_ref.at[1],
                        ba_ref.at[1], bx_ref.at[1], sp_ref.at[1], pad_scr)
    _lru_scan_tile(a, bb, a_scr, b_scr, c_scr, pad_scr, hb_ref, cbk_ref, True)


def _lru(z, x_block, conv_w, conv_b, wg, ba, bx, sp, h0f, h0b):
    bsz, t, _ = z.shape
    width = conv_w.shape[1]
    hd = width // LRU_HEADS
    assert hd == LANES
    tm = _row_tile(t, LRU_ROWS)
    nt = t // tm
    gpt = tm // SUBLANES
    ngr = t // SUBLANES
    pitch = (gpt + SUBLANES - 1) // SUBLANES * SUBLANES
    pitch += SUBLANES * (1 - (pitch // SUBLANES) % 2)
    col = lambda h: x_block * LRU_HEADS + h

    def tile(rev):
        return lambda b, h, i: (b, (nt - 1 - i) if rev else i, col(h))

    def prev(rev):
        return lambda b, h, i: (b, jnp.maximum(((nt - 1 - i) if rev else i) * gpt - 1, 0), col(h))

    def nxt(rev):
        return lambda b, h, i: (b, jnp.minimum((((nt - 1 - i) if rev else i) + 1) * gpt, ngr - 1), col(h))

    lane = lambda rows: pl.BlockSpec((rows, hd), lambda b, h, i: (0, h))
    dirs = pl.BlockSpec((2, 1, hd), lambda b, h, i: (0, 0, h))
    st_spec = pl.BlockSpec((None, 1, hd), lambda b, h, i: (b, 0, h))
    in_specs = []
    for rev in (False, True):
        in_specs += [pl.BlockSpec((None, tm, hd), tile(rev)),
                     pl.BlockSpec((None, SUBLANES, hd), prev(rev)),
                     pl.BlockSpec((None, SUBLANES, hd), nxt(rev))]
    in_specs += [lane(conv_w.shape[0]), lane(1),
                 pl.BlockSpec((2, None, hd, 2 * hd), lambda b, h, i: (0, h, 0, 0)),
                 dirs, dirs, dirs, st_spec, st_spec]
    return pl.pallas_call(
        _lru_kernel,
        grid=(bsz, LRU_HEADS, nt),
        in_specs=in_specs,
        out_specs=[
            pl.BlockSpec((None, tm, hd), lambda b, h, i: (b, i, h)),
            pl.BlockSpec((None, tm, hd), lambda b, h, i: (b, nt - 1 - i, h)),
            st_spec, st_spec,
        ],
        out_shape=[
            jax.ShapeDtypeStruct((bsz, t, width), _F32),
            jax.ShapeDtypeStruct((bsz, t, width), _F32),
            jax.ShapeDtypeStruct((bsz, 1, width), _F32),
            jax.ShapeDtypeStruct((bsz, 1, width), _F32),
        ],
        scratch_shapes=[pltpu.VMEM((gpt, SUBLANES, hd), _F32), pltpu.VMEM((gpt, SUBLANES, hd), _F32),
                        pltpu.VMEM((SUBLANES, hd), _F32), pltpu.VMEM((SUBLANES * pitch, hd), _F32)],
        compiler_params=_cparams("parallel", "parallel", "arbitrary"),
        name="lru_scan",
    )(z, z, z, z, z, z, conv_w, conv_b, wg, ba, bx, sp, h0f, h0b)


def _residual_ffn_prenorm(y, x_ref, gate_ref, a_ref, s_ref, wr_ref, xo_ref, h_ref, lg_ref):
    xn = x_ref[...] + gate_ref[...] * y
    xo_ref[...] = xn
    h = _norm_mod(xn, a_ref[...], s_ref[...]).astype(_MXU)
    h_ref[...] = h
    lg_ref[...] = _dot(h, wr_ref[...])


def _post_even_kernel(za_ref, of_ref, ob_ref, cw_ref, cb_ref, ng_ref, w_ref, x_ref, gate_ref, a_ref, s_ref, wr_ref,
                      xo_ref, h_ref, lg_ref, y_scr, *, row_len):
    tm = za_ref.shape[0]
    scw = cw_ref.shape[1]
    val = ng_ref.shape[1]
    dv = val // GLA_HEADS
    bg = za_ref[:, 0:scw]
    cx = za_ref[:, scw:2 * scw] * za_ref[:, 2 * scw:3 * scw]
    r = lax.broadcasted_iota(jnp.int32, (tm, 1), 0) % row_len
    left = jnp.where(r == 0, 0.0, pltpu.roll(cx, 1, 0))
    right = jnp.where(r == row_len - 1, 0.0, pltpu.roll(cx, tm - 1, 0))
    conv = cb_ref[...] + cw_ref[0:1] * left + cw_ref[1:2] * cx + cw_ref[2:3] * right
    acc = _dot(bg * conv, w_ref[0:scw, :])
    for h in range(GLA_HEADS):
        lanes = slice(h * dv, (h + 1) * dv)
        o = of_ref[:, lanes] + ob_ref[:, lanes]
        o = o * lax.rsqrt(jnp.mean(o * o, axis=-1, keepdims=True) + NORM_EPS)
        y_scr[:, lanes] = o * ng_ref[:, lanes] * _silu(za_ref[:, 3 * scw + h * dv:3 * scw + (h + 1) * dv])
    acc = acc + _dot(y_scr[...], w_ref[scw:scw + val, :])
    _residual_ffn_prenorm(acc, x_ref, gate_ref, a_ref, s_ref, wr_ref, xo_ref, h_ref, lg_ref)


def _post_odd_kernel(zc_ref, hf_ref, hb_ref, lng_ref, sgw_ref, sgb_ref, w_ref, x_ref, gate_ref, a_ref, s_ref, wr_ref,
                     xo_ref, h_ref, lg_ref, y_scr):
    tm = zc_ref.shape[0]
    width = hf_ref.shape[1]
    gdim = width // SG_GROUPS
    y_rec = (hf_ref[...] + hb_ref[...]) * _gelu(zc_ref[:, 0:width])
    acc = _dot(y_rec, w_ref[0:width, :])
    gv = _gelu(zc_ref[:, 2 * width:3 * width])
    gc = gv - jnp.mean(gv, axis=-1, keepdims=True)
    vb = gc * lax.rsqrt(jnp.mean(gc * gc, axis=-1, keepdims=True) + NORM_EPS) * lng_ref[...]
    for n in range(tm // SG_CHUNK):
        rows = slice(n * SG_CHUNK, (n + 1) * SG_CHUNK)
        for g in range(SG_GROUPS):
            lanes = slice(g * gdim, (g + 1) * gdim)
            mixed = _dot(sgw_ref[g], vb[rows, lanes]) + sgb_ref[g]
            y_scr[rows, lanes] = _gelu(zc_ref[rows, width + g * gdim:width + (g + 1) * gdim]) * mixed
    acc = acc + _dot(y_scr[...], w_ref[width:2 * width, :])
    _residual_ffn_prenorm(acc, x_ref, gate_ref, a_ref, s_ref, wr_ref, xo_ref, h_ref, lg_ref)


def _post(body, z, z_width, seq_in, consts, w_out, x, gate, a, s, wr):
    bsz, t, d = x.shape
    tm = _row_tile(t, PROJ_ROWS)
    row = lambda b, i: (b, i, 0)
    vec = pl.BlockSpec((None, 1, d), lambda b, i: (b, 0, 0))
    full = lambda arr: pl.BlockSpec(arr.shape, lambda b, i: (0,) * arr.ndim)
    width = seq_in[0].shape[-1]
    in_specs = [pl.BlockSpec((None, tm, z_width), row)]
    in_specs += [pl.BlockSpec((None, tm, width), row) for _ in seq_in]
    in_specs += [full(v) for v in consts] + [full(w_out), pl.BlockSpec((None, tm, d), row), vec, vec, vec, full(wr)]
    return pl.pallas_call(
        body,
        grid=(bsz, t // tm),
        in_specs=in_specs,
        out_specs=[
            pl.BlockSpec((None, tm, d), row),
            pl.BlockSpec((None, tm, d), row),
            pl.BlockSpec((None, tm, LANES), row),
        ],
        out_shape=[
            jax.ShapeDtypeStruct((bsz, t, d), _F32),
            jax.ShapeDtypeStruct((bsz, t, d), _MXU),
            jax.ShapeDtypeStruct((bsz, t, LANES), _F32),
        ],
        scratch_shapes=[pltpu.VMEM((tm, width), _F32)],
        compiler_params=_cparams("parallel", "parallel"),
        name="post_proj",
    )(z, *seq_in, *consts, w_out, x, gate, a, s, wr)


def _moe_ffn_kernel(x_ref, g_ref, wg_ref, wu_ref, wd_ref, o_ref, wg_scr, wu_scr, wd_scr):
    @pl.when(pl.program_id(1) == 0)
    def _():
        wg_scr[...] = wg_ref[...].astype(_MXU)
        wu_scr[...] = wu_ref[...].astype(_MXU)
        wd_scr[...] = wd_ref[...].astype(_MXU)

    bb, cap, d = x_ref.shape
    x = x_ref[...].reshape(bb * cap, d)
    gate = _dot(x, wg_scr[...])
    up = _dot(x, wu_scr[...])
    out = _dot(_silu(gate) * up, wd_scr[...]) * g_ref[...].reshape(bb * cap, 1)
    o_ref[...] = out.reshape(bb, cap, d).astype(o_ref.dtype)


def _moe_ffn(xg, g, wg, wu, wd, layer):
    bsz, ne, cap, d = xg.shape
    f = wg.shape[-1]
    bb = max(1, min(bsz, MOE_ROWS // cap))
    assert bsz % bb == 0
    tok = lambda e, b: (b, e, 0, 0)
    wsel = lambda e, b: (layer, e, 0, 0)
    return pl.pallas_call(
        _moe_ffn_kernel,
        grid=(ne, bsz // bb),
        in_specs=[
            pl.BlockSpec((bb, None, cap, d), tok),
            pl.BlockSpec((bb, None, cap, 1), tok),
            pl.BlockSpec((None, None, d, f), wsel),
            pl.BlockSpec((None, None, d, f), wsel),
            pl.BlockSpec((None, None, f, d), wsel),
        ],
        out_specs=pl.BlockSpec((bb, None, cap, d), tok),
        out_shape=jax.ShapeDtypeStruct((bsz, ne, cap, d), _MXU),
        scratch_shapes=[pltpu.VMEM((d, f), _MXU), pltpu.VMEM((d, f), _MXU), pltpu.VMEM((f, d), _MXU)],
        compiler_params=pltpu.CompilerParams(dimension_semantics=("parallel", "arbitrary"),
                                             vmem_limit_bytes=MOE_VMEM_LIMIT),
        name="moe_ffn",
    )(xg, g, wg, wu, wd)


def _combine_kernel(lo_ref, hi_ref, tok_ref, y_ref, x_ref, gate_ref, fg_ref, o_ref, acc_ref, *, final_norm):
    b = pl.program_id(0)
    j = pl.program_id(1)
    tm = acc_ref.shape[0]
    ch = tok_ref.shape[-1]
    acc_ref[...] = jnp.zeros_like(acc_ref)
    rows = j * tm + lax.broadcasted_iota(jnp.int32, (tm, ch), 0)

    def body(c, carry):
        onehot = jnp.where(rows == tok_ref[c], 1.0, 0.0)
        acc_ref[...] += _dot(onehot, y_ref[pl.ds(pl.multiple_of(c * ch, ch), ch), :])
        return carry

    lax.fori_loop(lo_ref[b, j], hi_ref[b, j] + 1, body, 0)
    xn = x_ref[...] + gate_ref[...] * acc_ref[...]
    if final_norm:
        ms = jnp.mean(xn * xn, axis=-1, keepdims=True)
        xn = xn * lax.rsqrt(ms + NORM_EPS) * fg_ref[...]
    o_ref[...] = xn


def _combine(ys, toks, x, gate, final_g):
    bsz, t, d = x.shape
    p = ys.shape[1]
    tm = _row_tile(t)
    ch = _row_tile(p)
    nt, nc = t // tm, p // ch
    bounds = jnp.arange(nt + 1, dtype=jnp.int32) * tm
    cut = jax.vmap(lambda row: jnp.searchsorted(row, bounds, side="left"))(toks).astype(jnp.int32)
    c_lo = jnp.minimum(cut[:, :-1] // ch, nc - 1)
    c_hi = jnp.maximum((cut[:, 1:] - 1) // ch, c_lo)
    fg = jnp.ones((1, d), _F32) if final_g is None else final_g.reshape(1, d)
    grid_spec = pltpu.PrefetchScalarGridSpec(
        num_scalar_prefetch=2,
        grid=(bsz, nt),
        in_specs=[
            pl.BlockSpec((None, nc, 1, ch), lambda b, j, lo, hi: (b, 0, 0, 0)),
            pl.BlockSpec((None, p, d), lambda b, j, lo, hi: (b, 0, 0)),
            pl.BlockSpec((None, tm, d), lambda b, j, lo, hi: (b, j, 0)),
            pl.BlockSpec((None, 1, d), lambda b, j, lo, hi: (b, 0, 0)),
            pl.BlockSpec((1, d), lambda b, j, lo, hi: (0, 0)),
        ],
        out_specs=pl.BlockSpec((None, tm, d), lambda b, j, lo, hi: (b, j, 0)),
        scratch_shapes=[pltpu.VMEM((tm, d), _F32)],
    )
    return pl.pallas_call(
        functools.partial(_combine_kernel, final_norm=final_g is not None),
        grid_spec=grid_spec,
        out_shape=jax.ShapeDtypeStruct((bsz, t, d), _F32),
        compiler_params=_cparams("parallel", "arbitrary"),
        name="moe_combine",
    )(c_lo, c_hi, toks.reshape(bsz, nc, 1, ch), ys, x, gate, fg)


def _ec_moe(x, gate, h, logits, wg, wu, wd, layer, final_g):
    bsz, t, d = h.shape
    ne = logits.shape[-1]
    cap = EC_CAPACITY * t // ne
    aff = jax.nn.softmax(logits, axis=-1)
    g, idx = lax.top_k(jnp.swapaxes(aff, 1, 2), cap)
    xg = jax.vmap(lambda hb, ib: hb[ib])(h, idx)
    y = _moe_ffn(xg, g[..., None], wg, wu, wd, layer).reshape(bsz, ne * cap, d)
    slots = jnp.broadcast_to(jnp.arange(ne * cap, dtype=jnp.int32), (bsz, ne * cap))
    toks, perm = lax.sort_key_val(idx.reshape(bsz, ne * cap).astype(jnp.int32), slots, dimension=-1)
    ys = jax.vmap(lambda yb, pb: yb[pb])(y, perm)
    return _combine(ys, toks, x, gate, final_g)


def _pad_to(w, axis, size):
    pad = [(0, 0)] * w.ndim
    pad[axis] = (0, size - w.shape[axis])
    return jnp.pad(w, pad)


def kernel(x, c, ctx, c_ctx, ada_w, ada_b, norm_mix_g, norm_ffn_g, norm_final_g, ev_w_in, ev_conv_w, ev_conv_b, ev_decay_up, ev_decay_b, ev_norm_g, ev_w_out, od_w_in, od_conv_w, od_conv_b, od_gate_a_w, od_gate_a_b, od_gate_x_w, od_gate_x_b, od_lambda, od_sg_norm_g, od_sg_w, od_sg_b, od_w_out, moe_router, moe_w_gate, moe_w_up, moe_w_down):
    depth = ada_w.shape[0]
    bsz, seq, d = x.shape
    ne = moe_router.shape[-1]
    x_lat, x_ctx = x, ctx
    cond = _pad_to(jnp.concatenate([c, c_ctx[None, :]], axis=0), 0, -(-(bsz + 1) // SUBLANES) * SUBLANES)
    ada = _adaln(cond, ada_w, ada_b.reshape(depth, 1, N_MOD * d))

    def vecs(m):
        return [v.reshape(bsz, 1, d) for v in m]

    for l in range(depth):
        last = l == depth - 1
        even = l % 2 == 0
        i = l // 2
        mod = [ada[l, :bsz, k * d:(k + 1) * d] for k in range(N_MOD)]
        mod_c = [jnp.broadcast_to(ada[l, bsz:bsz + 1, k * d:(k + 1) * d], (bsz, d)) for k in range(N_MOD)]
        wr = _pad_to(moe_router[l], 1, LANES).astype(_MXU)

        if even:
            w = ev_w_in[i]
            scw = ev_conv_w.shape[-1]
            key = ev_decay_up.shape[-1]
            val = ev_norm_g.shape[-1]
            o_q = 3 * scw
            o_g = o_q + key
            o_k = o_g + val
            o_v = o_k + key
            o_lr = o_v + val
            w_in = jnp.concatenate([w[:, :o_q], w[:, o_g:o_k], w[:, o_q:o_g], w[:, o_k:o_lr]], axis=1).astype(_MXU)
            w_lr = _pad_to(w[:, o_lr:], 1, LANES).astype(_MXU)
            up = jnp.zeros((LANES, 2 * key), _F32)
            up = up.at[:GLA_RANK, :key].set(ev_decay_up[i, 0]).at[GLA_RANK:2 * GLA_RANK, key:].set(ev_decay_up[i, 1])
            decay = (w_lr, up.astype(_MXU), ev_decay_b[i].reshape(1, 2 * key))
            w_out = ev_w_out[i].astype(_MXU)
            conv_w = _pad_to(ev_conv_w[i], 0, SUBLANES)
            consts = (conv_w, ev_conv_b[i].reshape(1, scw), ev_norm_g[i].reshape(1, val))
            epi_width = 3 * scw + val
            assert (3 * scw + val) % (2 * key + val) == 0
            qkv_block = epi_width // (2 * key + val)
            dk, dv = key // GLA_HEADS, val // GLA_HEADS
            scale = dk ** -0.5
            zero_st = jnp.zeros((bsz, GLA_HEADS // 2, dv, 2 * dk), _F32)

            def mixer(xs, m, s0, row_len):
                sh, sc = vecs(m[:2])
                z, la = _pre(xs, norm_mix_g[l] * (1.0 + sc), sh, w_in, decay)
                o_f, o_b, s_f, s_b = _gla(z, la, s0[0], s0[1], qkv_block, scale)
                body = functools.partial(_post_even_kernel, row_len=row_len)
                return (body, z, epi_width, (o_f, o_b), consts, w_out), (s_f, s_b)

            ctx_args, state = mixer(x_ctx, mod_c, (zero_st, zero_st), x_ctx.shape[1])
            lat_args, _ = mixer(x_lat, mod, state, GRID_W)
        else:
            w = od_w_in[i]
            width = od_conv_w.shape[-1]
            hd = width // LRU_HEADS
            w_in = jnp.concatenate([w[:, width:], w[:, :width]], axis=1).astype(_MXU)
            w_out = od_w_out[i].astype(_MXU)
            conv_w = _pad_to(od_conv_w[i], 0, SUBLANES)
            conv_b = od_conv_b[i].reshape(1, width)
            wgate = jnp.concatenate([od_gate_a_w[i], od_gate_x_w[i]], axis=-1).astype(_MXU)
            ba = od_gate_a_b[i].reshape(2, 1, width)
            bx = od_gate_x_b[i].reshape(2, 1, width)
            sp = (-LRU_C * jax.nn.softplus(-od_lambda[i])).reshape(2, 1, width)
            sgb = jnp.broadcast_to(od_sg_b[i][:, :, None], (SG_GROUPS, SG_CHUNK, width // SG_GROUPS))
            consts = (od_sg_norm_g[i].reshape(1, width), od_sg_w[i].astype(_MXU), sgb)
            zero_st = jnp.zeros((bsz, 1, width), _F32)

            def mixer(xs, m, s0, row_len):
                sh, sc = vecs(m[:2])
                (z,) = _pre(xs, norm_mix_g[l] * (1.0 + sc), sh, w_in)
                h_f, h_b, c_f, c_b = _lru(z, 3, conv_w, conv_b, wgate, ba, bx, sp, s0[0], s0[1])
                return (_post_odd_kernel, z, 3 * width, (h_f, h_b), consts, w_out), (c_f, c_b)

            ctx_args, state = mixer(x_ctx, mod_c, (zero_st, zero_st), None)
            lat_args, _ = mixer(x_lat, mod, state, None)

        def post(args, xs, m, final_g):
            gate1, sh, sc, gate2 = vecs(m[2:])
            xs, h, logits = _post(*args, xs, gate1, norm_ffn_g[l] * (1.0 + sc), sh, wr)
            return _ec_moe(xs, gate2, h, logits[..., :ne], moe_w_gate, moe_w_up, moe_w_down, l, final_g)

        x_lat = post(lat_args, x_lat, mod, norm_final_g if last else None)
        if not last:
            x_ctx = post(ctx_args, x_ctx, mod_c, None)
    return x_lat
```

```python
import functools

import jax
import jax.numpy as jnp
from jax import lax
from jax.experimental import pallas as pl
from jax.experimental.pallas import tpu as pltpu

_MXU = jnp.bfloat16
_F32 = jnp.float32
NORM_EPS = 1e-6
N_MOD = 6
GRID_W = 64
GLA_HEADS = 4
GLA_RANK = 16
GLA_TAU = 16.0
GLA_CHUNK = 64
LRU_HEADS = 4
LRU_C = 8.0
SG_GROUPS = 4
SG_CHUNK = 128
EC_CAPACITY = 2
LANES = 128
SUBLANES = 8
VMEM_LIMIT = 48 * 1024 * 1024
MOE_VMEM_LIMIT = 56 * 1024 * 1024
MOE_ROWS = 512
PROJ_ROWS = 512
LRU_ROWS = 1024


def _cparams(*sem):
    return pltpu.CompilerParams(dimension_semantics=sem, vmem_limit_bytes=VMEM_LIMIT)


def _row_tile(t, cap=256):
    tm = min(t, cap)
    assert t % tm == 0
    return tm


def _dot(a, b):
    return jnp.dot(a.astype(_MXU), b.astype(_MXU), preferred_element_type=_F32)


def _dot_nt(a, b):
    return lax.dot_general(a.astype(_MXU), b.astype(_MXU), (((1,), (1,)), ((), ())), preferred_element_type=_F32)


def _dot_tn(a, b):
    return lax.dot_general(a.astype(_MXU), b.astype(_MXU), (((0,), (0,)), ((), ())), preferred_element_type=_F32)


def _gelu(x):
    return 0.5 * x * (1.0 + jnp.tanh(0.7978845608028654 * (x + 0.044715 * (x * x * x))))


def _sigmoid(x):
    return 0.5 * jnp.tanh(0.5 * x) + 0.5


def _silu(x):
    return x * _sigmoid(x)


def _log_sigmoid(x):
    return jnp.minimum(x, 0.0) - jnp.log(1.0 + jnp.exp(-jnp.abs(x)))


def _norm_mod(x, a, s):
    ms = jnp.mean(x * x, axis=-1, keepdims=True)
    return x * lax.rsqrt(ms + NORM_EPS) * a + s


def _adaln_kernel(c_ref, w_ref, b_ref, o_ref):
    o_ref[...] = _dot(_silu(c_ref[...]), w_ref[...]) + b_ref[...]


def _adaln(cond, w, b):
    depth, d, n = w.shape
    r = cond.shape[0]
    tn = n // N_MOD
    return pl.pallas_call(
        _adaln_kernel,
        grid=(depth, N_MOD),
        in_specs=[
            pl.BlockSpec((r, d), lambda l, j: (0, 0)),
            pl.BlockSpec((None, d, tn), lambda l, j: (l, 0, j)),
            pl.BlockSpec((None, 1, tn), lambda l, j: (l, 0, j)),
        ],
        out_specs=pl.BlockSpec((None, r, tn), lambda l, j: (l, 0, j)),
        out_shape=jax.ShapeDtypeStruct((depth, r, n), _F32),
        compiler_params=_cparams("parallel", "parallel"),
        name="adaln",
    )(cond, w, b)


def _pre_kernel(x_ref, a_ref, s_ref, w_ref, o_ref):
    h = _norm_mod(x_ref[...], a_ref[...], s_ref[...])
    o_ref[...] = _dot(h, w_ref[...])


def _pre_even_kernel(x_ref, a_ref, s_ref, w_ref, wlr_ref, up_ref, db_ref, o_ref, la_ref):
    h = _norm_mod(x_ref[...], a_ref[...], s_ref[...]).astype(_MXU)
    o_ref[...] = _dot(h, w_ref[...])
    lr = _dot(h, wlr_ref[...])
    la_ref[...] = _log_sigmoid(_dot(lr, up_ref[...]) + db_ref[...]) * (1.0 / GLA_TAU)


def _pre(x, a, s, w, decay=None):
    bsz, t, d = x.shape
    n = w.shape[1]
    tm = _row_tile(t, PROJ_ROWS)
    row = lambda b, i: (b, i, 0)
    vec = pl.BlockSpec((None, 1, d), lambda b, i: (b, 0, 0))
    full = lambda arr: pl.BlockSpec(arr.shape, lambda b, i: (0,) * arr.ndim)
    in_specs = [pl.BlockSpec((None, tm, d), row), vec, vec, full(w)]
    out_specs = [pl.BlockSpec((None, tm, n), row)]
    out_shape = [jax.ShapeDtypeStruct((bsz, t, n), _F32)]
    args = [x, a, s, w]
    body = _pre_kernel
    if decay is not None:
        body = _pre_even_kernel
        n_la = decay[1].shape[1]
        in_specs += [full(v) for v in decay]
        out_specs.append(pl.BlockSpec((None, tm, n_la), row))
        out_shape.append(jax.ShapeDtypeStruct((bsz, t, n_la), _F32))
        args += list(decay)
    return pl.pallas_call(
        body, grid=(bsz, t // tm), in_specs=in_specs, out_specs=out_specs, out_shape=out_shape,
        compiler_params=_cparams("parallel", "parallel"), name="pre_proj",
    )(*args)


def _bmm(a, b, dims):
    return lax.dot_general(a.astype(_MXU), b.astype(_MXU), (dims, ((0,), (0,))), preferred_element_type=_F32)


def _cumsum_chunks(tri, x):
    hi = x.astype(_MXU)
    r1 = x - hi.astype(_F32)
    mid = r1.astype(_MXU)
    lo = (r1 - mid.astype(_F32)).astype(_MXU)
    mm = ((2,), (1,))
    return _bmm(tri, hi, mm) + _bmm(tri, mid, mm) + _bmm(tri, lo, mm)


def _gla_block(qkv_ref, la_ref, o_ref, st_ref, sin_ref, keep, scale, reverse):
    c = GLA_CHUNK
    tb = qkv_ref.shape[0]
    n = tb // c
    key = la_ref.shape[-1]
    dv = (qkv_ref.shape[-1] - 2 * key) // GLA_HEADS
    dk = key // GLA_HEADS
    tri = jnp.broadcast_to(keep.astype(_MXU)[None], (n, c, c))
    b = _cumsum_chunks(tri, la_ref[...].reshape(n, c, key))
    btot = b[:, 0:1] if reverse else b[:, c - 1:c]
    q = qkv_ref[:, 0:key].reshape(n, c, key)
    k = qkv_ref[:, key:2 * key].reshape(n, c, key)
    qd = q * (scale * jnp.exp(b))
    kd = k * jnp.exp(-b)
    kr = k * jnp.exp(btot - b)
    dec = jnp.exp(btot)
    pair = 2 * dk
    lane_lo = lax.broadcasted_iota(jnp.int32, (n, c, pair), 2) < dk
    order = range(n - 1, -1, -1) if reverse else range(n)
    for p in range(GLA_HEADS // 2):
        lanes = slice(p * pair, (p + 1) * pair)
        qd_p, kd_p, kr_p = qd[:, :, lanes], kd[:, :, lanes], kr[:, :, lanes]
        vs = [qkv_ref[:, 2 * key + h * dv:2 * key + (h + 1) * dv].reshape(n, c, dv) for h in (2 * p, 2 * p + 1)]
        km = jnp.concatenate([jnp.where(lane_lo, kr_p, 0.0), jnp.where(lane_lo, 0.0, kr_p)], axis=1)
        upd = _bmm(jnp.concatenate(vs, axis=1), km, ((1,), (1,)))
        st = st_ref[p]
        for j in order:
            sin_ref[j] = st
            st = dec[j, :, lanes] * st + upd[j]
        st_ref[p] = st
        s_in = sin_ref[...]
        for s in range(2):
            h = 2 * p + s
            qm = jnp.where(lane_lo if s == 0 else jnp.logical_not(lane_lo), qd_p, 0.0)
            att = jnp.where(keep[None], _bmm(qm, kd_p, ((2,), (2,))), 0.0)
            o = _bmm(att, vs[s], ((2,), (1,))) + _bmm(qm, s_in, ((2,), (2,)))
            o_ref[:, h * dv:(h + 1) * dv] = o.reshape(tb, dv)


def _gla_kernel(qkv_f_ref, la_f_ref, qkv_b_ref, la_b_ref, s0f_ref, s0b_ref, of_ref, ob_ref, sf_ref, sb_ref, sin_ref, *,
                scale):
    i = pl.program_id(1)

    @pl.when(i == 0)
    def _():
        sf_ref[...] = s0f_ref[...]
        sb_ref[...] = s0b_ref[...]

    c = GLA_CHUNK
    ri = lax.broadcasted_iota(jnp.int32, (c, c), 0)
    ci = lax.broadcasted_iota(jnp.int32, (c, c), 1)
    _gla_block(qkv_f_ref, la_f_ref, of_ref, sf_ref, sin_ref, ri >= ci, scale, False)
    _gla_block(qkv_b_ref, la_b_ref, ob_ref, sb_ref, sin_ref, ri <= ci, scale, True)


def _gla(z, la, s0f, s0b, qkv_block, scale):
    bsz, t, _ = z.shape
    key = la.shape[-1] // 2
    st_shape = s0f.shape[1:]
    val = GLA_HEADS * st_shape[1]
    width = 2 * key + val
    tb = _row_tile(t, 512)
    nt = t // tb
    st_spec = pl.BlockSpec((None,) + st_shape, lambda b, i: (b, 0, 0, 0))
    outs = pl.pallas_call(
        functools.partial(_gla_kernel, scale=scale),
        grid=(bsz, nt),
        in_specs=[
            pl.BlockSpec((None, tb, width), lambda b, i: (b, i, qkv_block)),
            pl.BlockSpec((None, tb, key), lambda b, i: (b, i, 0)),
            pl.BlockSpec((None, tb, width), lambda b, i: (b, nt - 1 - i, qkv_block)),
            pl.BlockSpec((None, tb, key), lambda b, i: (b, nt - 1 - i, 1)),
            st_spec, st_spec,
        ],
        out_specs=[
            pl.BlockSpec((None, tb, val), lambda b, i: (b, i, 0)),
            pl.BlockSpec((None, tb, val), lambda b, i: (b, nt - 1 - i, 0)),
            st_spec, st_spec,
        ],
        out_shape=[
            jax.ShapeDtypeStruct((bsz, t, val), _F32),
            jax.ShapeDtypeStruct((bsz, t, val), _F32),
            jax.ShapeDtypeStruct(s0f.shape, _F32),
            jax.ShapeDtypeStruct(s0b.shape, _F32),
        ],
        scratch_shapes=[pltpu.VMEM((tb // GLA_CHUNK,) + st_shape[1:], _F32)],
        compiler_params=_cparams("parallel", "arbitrary"),
        name="gla_scan",
    )(z, la, z, la, s0f, s0b)
    return outs


def _lru_coeffs(x_ref, prev_ref, next_ref, first, last, cw_ref, cb_ref, wg_ref, ba_ref, bx_ref, sp_ref, pad_scr):
    tm, hd = x_ref.shape
    steps = tm // SUBLANES
    pitch = pad_scr.shape[0] // SUBLANES
    for sg in range(SUBLANES):
        pad_scr[sg * pitch:sg * pitch + steps, :] = x_ref[sg * steps:(sg + 1) * steps, :]
    x = jnp.stack([pad_scr[pl.ds(s, SUBLANES, stride=pitch), :] for s in range(steps)], axis=0)
    prev = jnp.where(first, 0.0, prev_ref[...])
    nxt = jnp.where(last, 0.0, next_ref[...])
    seg = lax.broadcasted_iota(jnp.int32, (SUBLANES, 1), 0)

    def from_prev_segment(v, fill):
        return jnp.where(seg == 0, fill, pltpu.roll(v, 1, 0))

    def from_next_segment(v, fill):
        return jnp.where(seg == SUBLANES - 1, fill, pltpu.roll(v, SUBLANES - 1, 0))

    m1_edge = from_prev_segment(x[steps - 1], prev[SUBLANES - 1:SUBLANES])
    m2_edge = from_prev_segment(x[steps - 2], prev[SUBLANES - 2:SUBLANES - 1])
    p1_edge = from_next_segment(x[0], nxt[0:1])
    x_m1 = jnp.concatenate([m1_edge[None], x[:steps - 1]], axis=0)
    x_m2 = jnp.concatenate([m2_edge[None], m1_edge[None], x[:steps - 2]], axis=0)
    x_p1 = jnp.concatenate([x[1:], p1_edge[None]], axis=0)
    xc = cb_ref[...] + cw_ref[0:1] * x_m2 + cw_ref[1:2] * x_m1 + cw_ref[2:3] * x + cw_ref[3:4] * x_p1
    xc = xc.reshape(tm, hd)
    g = _dot(xc, wg_ref[...])
    rg = _sigmoid(g[:, :hd] + ba_ref[...])
    ig = _sigmoid(g[:, hd:] + bx_ref[...])
    log_a = rg * sp_ref[...]
    th = jnp.tanh(0.5 * log_a)
    rcp = 1.0 / (1.0 - th)
    a = (1.0 + th) * rcp
    bb = (2.0 * jnp.sqrt(-th) * rcp) * (ig * xc)
    shape = (steps, SUBLANES, hd)
    return a.reshape(shape), bb.reshape(shape)


def _lru_scan_tile(a, bb, h_scr, p_scr, c_scr, pad_scr, h_ref, carry_ref, reverse):
    steps = a.shape[0]
    pitch = pad_scr.shape[0] // SUBLANES
    order = range(steps - 1, -1, -1) if reverse else range(steps)
    h = None
    for s in order:
        h = bb[s] if h is None else a[s] * h + bb[s]
        prod = a[s] if s == order[0] else a[s] * prod
        h_scr[s] = h
        p_scr[s] = prod
    c = carry_ref[...]
    for sg in (range(SUBLANES - 1, -1, -1) if reverse else range(SUBLANES)):
        c_scr[sg:sg + 1, :] = c
        c = h[sg:sg + 1] + prod[sg:sg + 1] * c
    carry_ref[...] = c
    c_in = c_scr[...]
    for s in range(steps):
        pad_scr[pl.ds(s, SUBLANES, stride=pitch), :] = h_scr[s] + p_scr[s] * c_in
    for sg in range(SUBLANES):
        h_ref[sg * steps:(sg + 1) * steps, :] = pad_scr[sg * pitch:sg * pitch + steps, :]


def _lru_kernel(xf_ref, xf_prev_ref, xf_next_ref, xb_ref, xb_prev_ref, xb_next_ref, cw_ref, cb_ref, wg_ref, ba_ref,
                bx_ref, sp_ref, h0f_ref, h0b_ref, hf_ref, hb_ref, cf_ref, cbk_ref, a_scr, b_scr, c_scr, pad_scr):
    i = pl.program_id(2)
    nt = pl.num_programs(2)

    @pl.when(i == 0)
    def _():
        cf_ref[...] = h0f_ref[...]
        cbk_ref[...] = h0b_ref[...]

    a, bb = _lru_coeffs(xf_ref, xf_prev_ref, xf_next_ref, i == 0, i == nt - 1, cw_ref, cb_ref, wg_ref.at[0],
                        ba_ref.at[0], bx_ref.at[0], sp_ref.at[0], pad_scr)
    _lru_scan_tile(a, bb, a_scr, b_scr, c_scr, pad_scr, hf_ref, cf_ref, False)
    a, bb = _lru_coeffs(xb_ref, xb_prev_ref, xb_next_ref, i == nt - 1, i == 0, cw_ref, cb_ref, wg_ref.at[1],
                        ba_ref.at[1], bx_ref.at[1], sp_ref.at[1], pad_scr)
    _lru_scan_tile(a, bb, a_scr, b_scr, c_scr, pad_scr, hb_ref, cbk_ref, True)


def _lru(z, x_block, conv_w, conv_b, wg, ba, bx, sp, h0f, h0b):
    bsz, t, _ = z.shape
    width = conv_w.shape[1]
    hd = width // LRU_HEADS
    assert hd == LANES
    tm = _row_tile(t, LRU_ROWS)
    nt = t // tm
    gpt = tm // SUBLANES
    ngr = t // SUBLANES
    pitch = (gpt + SUBLANES - 1) // SUBLANES * SUBLANES
    pitch += SUBLANES * (1 - (pitch // SUBLANES) % 2)
    col = lambda h: x_block * LRU_HEADS + h

    def tile(rev):
        return lambda b, h, i: (b, (nt - 1 - i) if rev else i, col(h))

    def prev(rev):
        return lambda b, h, i: (b, jnp.maximum(((nt - 1 - i) if rev else i) * gpt - 1, 0), col(h))

    def nxt(rev):
        return lambda b, h, i: (b, jnp.minimum((((nt - 1 - i) if rev else i) + 1) * gpt, ngr - 1), col(h))

    lane = lambda rows: pl.BlockSpec((rows, hd), lambda b, h, i: (0, h))
    dirs = pl.BlockSpec((2, 1, hd), lambda b, h, i: (0, 0, h))
    st_spec = pl.BlockSpec((None, 1, hd), lambda b, h, i: (b, 0, h))
    in_specs = []
    for rev in (False, True):
        in_specs += [pl.BlockSpec((None, tm, hd), tile(rev)),
                     pl.BlockSpec((None, SUBLANES, hd), prev(rev)),
                     pl.BlockSpec((None, SUBLANES, hd), nxt(rev))]
    in_specs += [lane(conv_w.shape[0]), lane(1),
                 pl.BlockSpec((2, None, hd, 2 * hd), lambda b, h, i: (0, h, 0, 0)),
                 dirs, dirs, dirs, st_spec, st_spec]
    return pl.pallas_call(
        _lru_kernel,
        grid=(bsz, LRU_HEADS, nt),
        in_specs=in_specs,
        out_specs=[
            pl.BlockSpec((None, tm, hd), lambda b, h, i: (b, i, h)),
            pl.BlockSpec((None, tm, hd), lambda b, h, i: (b, nt - 1 - i, h)),
            st_spec, st_spec,
        ],
        out_shape=[
            jax.ShapeDtypeStruct((bsz, t, width), _F32),
            jax.ShapeDtypeStruct((bsz, t, width), _F32),
            jax.ShapeDtypeStruct((bsz, 1, width), _F32),
            jax.ShapeDtypeStruct((bsz, 1, width), _F32),
        ],
        scratch_shapes=[pltpu.VMEM((gpt, SUBLANES, hd), _F32), pltpu.VMEM((gpt, SUBLANES, hd), _F32),
                        pltpu.VMEM((SUBLANES, hd), _F32), pltpu.VMEM((SUBLANES * pitch, hd), _F32)],
        compiler_params=_cparams("parallel", "parallel", "arbitrary"),
        name="lru_scan",
    )(z, z, z, z, z, z, conv_w, conv_b, wg, ba, bx, sp, h0f, h0b)


def _residual_ffn_prenorm(y, x_ref, gate_ref, a_ref, s_ref, wr_ref, xo_ref, h_ref, lg_ref):
    xn = x_ref[...] + gate_ref[...] * y
    xo_ref[...] = xn
    h = _norm_mod(xn, a_ref[...], s_ref[...]).astype(_MXU)
    h_ref[...] = h
    lg_ref[...] = _dot(h, wr_ref[...])


def _post_even_kernel(za_ref, of_ref, ob_ref, cw_ref, cb_ref, ng_ref, w_ref, x_ref, gate_ref, a_ref, s_ref, wr_ref,
                      xo_ref, h_ref, lg_ref, y_scr, *, row_len):
    tm = za_ref.shape[0]
    scw = cw_ref.shape[1]
    val = ng_ref.shape[1]
    dv = val // GLA_HEADS
    bg = za_ref[:, 0:scw]
    cx = za_ref[:, scw:2 * scw] * za_ref[:, 2 * scw:3 * scw]
    r = lax.broadcasted_iota(jnp.int32, (tm, 1), 0) % row_len
    left = jnp.where(r == 0, 0.0, pltpu.roll(cx, 1, 0))
    right = jnp.where(r == row_len - 1, 0.0, pltpu.roll(cx, tm - 1, 0))
    conv = cb_ref[...] + cw_ref[0:1] * left + cw_ref[1:2] * cx + cw_ref[2:3] * right
    acc = _dot(bg * conv, w_ref[0:scw, :])
    for h in range(GLA_HEADS):
        lanes = slice(h * dv, (h + 1) * dv)
        o = of_ref[:, lanes] + ob_ref[:, lanes]
        o = o * lax.rsqrt(jnp.mean(o * o, axis=-1, keepdims=True) + NORM_EPS)
        y_scr[:, lanes] = o * ng_ref[:, lanes] * _silu(za_ref[:, 3 * scw + h * dv:3 * scw + (h + 1) * dv])
    acc = acc + _dot(y_scr[...], w_ref[scw:scw + val, :])
    _residual_ffn_prenorm(acc, x_ref, gate_ref, a_ref, s_ref, wr_ref, xo_ref, h_ref, lg_ref)


def _post_odd_kernel(zc_ref, hf_ref, hb_ref, lng_ref, sgw_ref, sgb_ref, w_ref, x_ref, gate_ref, a_ref, s_ref, wr_ref,
                     xo_ref, h_ref, lg_ref, y_scr):
    tm = zc_ref.shape[0]
    width = hf_ref.shape[1]
    gdim = width // SG_GROUPS
    y_rec = (hf_ref[...] + hb_ref[...]) * _gelu(zc_ref[:, 0:width])
    acc = _dot(y_rec, w_ref[0:width, :])
    gv = _gelu(zc_ref[:, 2 * width:3 * width])
    gc = gv - jnp.mean(gv, axis=-1, keepdims=True)
    vb = gc * lax.rsqrt(jnp.mean(gc * gc, axis=-1, keepdims=True) + NORM_EPS) * lng_ref[...]
    for n in range(tm // SG_CHUNK):
        rows = slice(n * SG_CHUNK, (n + 1) * SG_CHUNK)
        for g in range(SG_GROUPS):
            lanes = slice(g * gdim, (g + 1) * gdim)
            mixed = _dot(sgw_ref[g], vb[rows, lanes]) + sgb_ref[g]
            y_scr[rows, lanes] = _gelu(zc_ref[rows, width + g * gdim:width + (g + 1) * gdim]) * mixed
    acc = acc + _dot(y_scr[...], w_ref[width:2 * width, :])
    _residual_ffn_prenorm(acc, x_ref, gate_ref, a_ref, s_ref, wr_ref, xo_ref, h_ref, lg_ref)


def _post(body, z, z_width, seq_in, consts, w_out, x, gate, a, s, wr):
    bsz, t, d = x.shape
    tm = _row_tile(t, PROJ_ROWS)
    row = lambda b, i: (b, i, 0)
    vec = pl.BlockSpec((None, 1, d), lambda b, i: (b, 0, 0))
    full = lambda arr: pl.BlockSpec(arr.shape, lambda b, i: (0,) * arr.ndim)
    width = seq_in[0].shape[-1]
    in_specs = [pl.BlockSpec((None, tm, z_width), row)]
    in_specs += [pl.BlockSpec((None, tm, width), row) for _ in seq_in]
    in_specs += [full(v) for v in consts] + [full(w_out), pl.BlockSpec((None, tm, d), row), vec, vec, vec, full(wr)]
    return pl.pallas_call(
        body,
        grid=(bsz, t // tm),
        in_specs=in_specs,
        out_specs=[
            pl.BlockSpec((None, tm, d), row),
            pl.BlockSpec((None, tm, d), row),
            pl.BlockSpec((None, tm, LANES), row),
        ],
        out_shape=[
            jax.ShapeDtypeStruct((bsz, t, d), _F32),
            jax.ShapeDtypeStruct((bsz, t, d), _MXU),
            jax.ShapeDtypeStruct((bsz, t, LANES), _F32),
        ],
        scratch_shapes=[pltpu.VMEM((tm, width), _F32)],
        compiler_params=_cparams("parallel", "parallel"),
        name="post_proj",
    )(z, *seq_in, *consts, w_out, x, gate, a, s, wr)


def _moe_ffn_kernel(x_ref, g_ref, wg_ref, wu_ref, wd_ref, o_ref, wg_scr, wu_scr, wd_scr):
    @pl.when(pl.program_id(1) == 0)
    def _():
        wg_scr[...] = wg_ref[...].astype(_MXU)
        wu_scr[...] = wu_ref[...].astype(_MXU)
        wd_scr[...] = wd_ref[...].astype(_MXU)

    bb, cap, d = x_ref.shape
    x = x_ref[...].reshape(bb * cap, d)
    gate = _dot(x, wg_scr[...])
    up = _dot(x, wu_scr[...])
    out = _dot(_silu(gate) * up, wd_scr[...]) * g_ref[...].reshape(bb * cap, 1)
    o_ref[...] = out.reshape(bb, cap, d).astype(o_ref.dtype)


def _moe_ffn(xg, g, wg, wu, wd, layer):
    bsz, ne, cap, d = xg.shape
    f = wg.shape[-1]
    bb = max(1, min(bsz, MOE_ROWS // cap))
    assert bsz % bb == 0
    tok = lambda e, b: (b, e, 0, 0)
    wsel = lambda e, b: (layer, e, 0, 0)
    return pl.pallas_call(
        _moe_ffn_kernel,
        grid=(ne, bsz // bb),
        in_specs=[
            pl.BlockSpec((bb, None, cap, d), tok),
            pl.BlockSpec((bb, None, cap, 1), tok),
            pl.BlockSpec((None, None, d, f), wsel),
            pl.BlockSpec((None, None, d, f), wsel),
            pl.BlockSpec((None, None, f, d), wsel),
        ],
        out_specs=pl.BlockSpec((bb, None, cap, d), tok),
        out_shape=jax.ShapeDtypeStruct((bsz, ne, cap, d), _MXU),
        scratch_shapes=[pltpu.VMEM((d, f), _MXU), pltpu.VMEM((d, f), _MXU), pltpu.VMEM((f, d), _MXU)],
        compiler_params=pltpu.CompilerParams(dimension_semantics=("parallel", "arbitrary"),
                                             vmem_limit_bytes=MOE_VMEM_LIMIT),
        name="moe_ffn",
    )(xg, g, wg, wu, wd)


def _combine_kernel(lo_ref, hi_ref, tok_ref, y_ref, x_ref, gate_ref, fg_ref, o_ref, acc_ref, *, final_norm):
    b = pl.program_id(0)
    j = pl.program_id(1)
    tm = acc_ref.shape[0]
    ch = tok_ref.shape[-1]
    acc_ref[...] = jnp.zeros_like(acc_ref)
    rows = j * tm + lax.broadcasted_iota(jnp.int32, (tm, ch), 0)

    def body(c, carry):
        onehot = jnp.where(rows == tok_ref[c], 1.0, 0.0)
        acc_ref[...] += _dot(onehot, y_ref[pl.ds(pl.multiple_of(c * ch, ch), ch), :])
        return carry

    lax.fori_loop(lo_ref[b, j], hi_ref[b, j] + 1, body, 0)
    xn = x_ref[...] + gate_ref[...] * acc_ref[...]
    if final_norm:
        ms = jnp.mean(xn * xn, axis=-1, keepdims=True)
        xn = xn * lax.rsqrt(ms + NORM_EPS) * fg_ref[...]
    o_ref[...] = xn


def _combine(ys, toks, x, gate, final_g):
    bsz, t, d = x.shape
    p = ys.shape[1]
    tm = _row_tile(t)
    ch = _row_tile(p)
    nt, nc = t // tm, p // ch
    bounds = jnp.arange(nt + 1, dtype=jnp.int32) * tm
    cut = jnp.sum((toks[:, :, None] < bounds[None, None, :]).astype(jnp.int32), axis=1)
    c_lo = jnp.minimum(cut[:, :-1] // ch, nc - 1)
    c_hi = jnp.maximum((cut[:, 1:] - 1) // ch, c_lo)
    fg = jnp.ones((1, d), _F32) if final_g is None else final_g.reshape(1, d)
    grid_spec = pltpu.PrefetchScalarGridSpec(
        num_scalar_prefetch=2,
        grid=(bsz, nt),
        in_specs=[
            pl.BlockSpec((None, nc, 1, ch), lambda b, j, lo, hi: (b, 0, 0, 0)),
            pl.BlockSpec((None, p, d), lambda b, j, lo, hi: (b, 0, 0)),
            pl.BlockSpec((None, tm, d), lambda b, j, lo, hi: (b, j, 0)),
            pl.BlockSpec((None, 1, d), lambda b, j, lo, hi: (b, 0, 0)),
            pl.BlockSpec((1, d), lambda b, j, lo, hi: (0, 0)),
        ],
        out_specs=pl.BlockSpec((None, tm, d), lambda b, j, lo, hi: (b, j, 0)),
        scratch_shapes=[pltpu.VMEM((tm, d), _F32)],
    )
    return pl.pallas_call(
        functools.partial(_combine_kernel, final_norm=final_g is not None),
        grid_spec=grid_spec,
        out_shape=jax.ShapeDtypeStruct((bsz, t, d), _F32),
        compiler_params=_cparams("parallel", "arbitrary"),
        name="moe_combine",
    )(c_lo, c_hi, toks.reshape(bsz, nc, 1, ch), ys, x, gate, fg)


def _ec_moe(x, gate, h, logits, wg, wu, wd, layer, final_g):
    bsz, t, d = h.shape
    ne = logits.shape[-1]
    cap = EC_CAPACITY * t // ne
    aff = jax.nn.softmax(logits, axis=-1)
    keys = jnp.transpose(-aff, (1, 0, 2)).reshape(t, bsz * ne)
    token = lax.broadcasted_iota(jnp.int32, (t, bsz * ne), 0)
    keys, token = lax.sort((keys, token), dimension=0, num_keys=1, is_stable=True)
    g = jnp.transpose((-keys[:cap]).reshape(cap, bsz, ne), (1, 2, 0))
    idx = jnp.transpose(token[:cap].reshape(cap, bsz, ne), (1, 2, 0))
    xg = jax.vmap(lambda hb, ib: hb[ib])(h, idx)
    y = _moe_ffn(xg, g[..., None], wg, wu, wd, layer).reshape(bsz, ne * cap, d)
    slots = jnp.broadcast_to(jnp.arange(ne * cap, dtype=jnp.int32), (bsz, ne * cap))
    toks, perm = lax.sort_key_val(idx.reshape(bsz, ne * cap).astype(jnp.int32), slots, dimension=-1)
    ys = jax.vmap(lambda yb, pb: yb[pb])(y, perm)
    return _combine(ys, toks, x, gate, final_g)


def _pad_to(w, axis, size):
    pad = [(0, 0)] * w.ndim
    pad[axis] = (0, size - w.shape[axis])
    return jnp.pad(w, pad)


def kernel(x, c, ctx, c_ctx, ada_w, ada_b, norm_mix_g, norm_ffn_g, norm_final_g, ev_w_in, ev_conv_w, ev_conv_b, ev_decay_up, ev_decay_b, ev_norm_g, ev_w_out, od_w_in, od_conv_w, od_conv_b, od_gate_a_w, od_gate_a_b, od_gate_x_w, od_gate_x_b, od_lambda, od_sg_norm_g, od_sg_w, od_sg_b, od_w_out, moe_router, moe_w_gate, moe_w_up, moe_w_down):
    depth = ada_w.shape[0]
    bsz, seq, d = x.shape
    ne = moe_router.shape[-1]
    x_lat, x_ctx = x, ctx
    cond = _pad_to(jnp.concatenate([c, c_ctx[None, :]], axis=0), 0, -(-(bsz + 1) // SUBLANES) * SUBLANES)
    ada = _adaln(cond, ada_w, ada_b.reshape(depth, 1, N_MOD * d))

    def vecs(m):
        return [v.reshape(bsz, 1, d) for v in m]

    for l in range(depth):
        last = l == depth - 1
        even = l % 2 == 0
        i = l // 2
        mod = [ada[l, :bsz, k * d:(k + 1) * d] for k in range(N_MOD)]
        mod_c = [jnp.broadcast_to(ada[l, bsz:bsz + 1, k * d:(k + 1) * d], (bsz, d)) for k in range(N_MOD)]
        wr = _pad_to(moe_router[l], 1, LANES).astype(_MXU)

        if even:
            w = ev_w_in[i]
            scw = ev_conv_w.shape[-1]
            key = ev_decay_up.shape[-1]
            val = ev_norm_g.shape[-1]
            o_q = 3 * scw
            o_g = o_q + key
            o_k = o_g + val
            o_v = o_k + key
            o_lr = o_v + val
            w_in = jnp.concatenate([w[:, :o_q], w[:, o_g:o_k], w[:, o_q:o_g], w[:, o_k:o_lr]], axis=1).astype(_MXU)
            w_lr = _pad_to(w[:, o_lr:], 1, LANES).astype(_MXU)
            up = jnp.zeros((LANES, 2 * key), _F32)
            up = up.at[:GLA_RANK, :key].set(ev_decay_up[i, 0]).at[GLA_RANK:2 * GLA_RANK, key:].set(ev_decay_up[i, 1])
            decay = (w_lr, up.astype(_MXU), ev_decay_b[i].reshape(1, 2 * key))
            w_out = ev_w_out[i].astype(_MXU)
            conv_w = _pad_to(ev_conv_w[i], 0, SUBLANES)
            consts = (conv_w, ev_conv_b[i].reshape(1, scw), ev_norm_g[i].reshape(1, val))
            epi_width = 3 * scw + val
            assert (3 * scw + val) % (2 * key + val) == 0
            qkv_block = epi_width // (2 * key + val)
            dk, dv = key // GLA_HEADS, val // GLA_HEADS
            scale = dk ** -0.5
            zero_st = jnp.zeros((bsz, GLA_HEADS // 2, dv, 2 * dk), _F32)

            def mixer(xs, m, s0, row_len):
                sh, sc = vecs(m[:2])
                z, la = _pre(xs, norm_mix_g[l] * (1.0 + sc), sh, w_in, decay)
                o_f, o_b, s_f, s_b = _gla(z, la, s0[0], s0[1], qkv_block, scale)
                body = functools.partial(_post_even_kernel, row_len=row_len)
                return (body, z, epi_width, (o_f, o_b), consts, w_out), (s_f, s_b)

            ctx_args, state = mixer(x_ctx, mod_c, (zero_st, zero_st), x_ctx.shape[1])
            lat_args, _ = mixer(x_lat, mod, state, GRID_W)
        else:
            w = od_w_in[i]
            width = od_conv_w.shape[-1]
            hd = width // LRU_HEADS
            w_in = jnp.concatenate([w[:, width:], w[:, :width]], axis=1).astype(_MXU)
            w_out = od_w_out[i].astype(_MXU)
            conv_w = _pad_to(od_conv_w[i], 0, SUBLANES)
            conv_b = od_conv_b[i].reshape(1, width)
            wgate = jnp.concatenate([od_gate_a_w[i], od_gate_x_w[i]], axis=-1).astype(_MXU)
            ba = od_gate_a_b[i].reshape(2, 1, width)
            bx = od_gate_x_b[i].reshape(2, 1, width)
            sp = (-LRU_C * jax.nn.softplus(-od_lambda[i])).reshape(2, 1, width)
            sgb = jnp.broadcast_to(od_sg_b[i][:, :, None], (SG_GROUPS, SG_CHUNK, width // SG_GROUPS))
            consts = (od_sg_norm_g[i].reshape(1, width), od_sg_w[i].astype(_MXU), sgb)
            zero_st = jnp.zeros((bsz, 1, width), _F32)

            def mixer(xs, m, s0, row_len):
                sh, sc = vecs(m[:2])
                (z,) = _pre(xs, norm_mix_g[l] * (1.0 + sc), sh, w_in)
                h_f, h_b, c_f, c_b = _lru(z, 3, conv_w, conv_b, wgate, ba, bx, sp, s0[0], s0[1])
                return (_post_odd_kernel, z, 3 * width, (h_f, h_b), consts, w_out), (c_f, c_b)

            ctx_args, state = mixer(x_ctx, mod_c, (zero_st, zero_st), None)
            lat_args, _ = mixer(x_lat, mod, state, None)

        def post(args, xs, m, final_g):
            gate1, sh, sc, gate2 = vecs(m[2:])
            xs, h, logits = _post(*args, xs, gate1, norm_ffn_g[l] * (1.0 + sc), sh, wr)
            return _ec_moe(xs, gate2, h, logits[..., :ne], moe_w_gate, moe_w_up, moe_w_down, l, final_g)

        x_lat = post(lat_args, x_lat, mod, norm_final_g if last else None)
        if not last:
            x_ctx = post(ctx_args, x_ctx, mod_c, None)
    return x_lat
```

```python
import functools

import jax
import jax.numpy as jnp
from jax import lax
from jax.experimental import pallas as pl
from jax.experimental.pallas import tpu as pltpu

_MXU = jnp.bfloat16
_ACT = jnp.bfloat16
_F32 = jnp.float32
NORM_EPS = 1e-6
N_MOD = 6
GRID_W = 64
GLA_HEADS = 4
GLA_RANK = 16
GLA_TAU = 16.0
GLA_CHUNK = 64
LRU_HEADS = 4
LRU_C = 8.0
SG_GROUPS = 4
SG_CHUNK = 128
EC_CAPACITY = 2
LANES = 128
SUBLANES = 8
HALO = 16
VMEM_LIMIT = 48 * 1024 * 1024
MOE_VMEM_LIMIT = 56 * 1024 * 1024
MOE_ROWS = 512
PROJ_ROWS = 512
LRU_ROWS = 1024


def _cparams(*sem):
    return pltpu.CompilerParams(dimension_semantics=sem, vmem_limit_bytes=VMEM_LIMIT)


def _row_tile(t, cap=256):
    tm = min(t, cap)
    assert t % tm == 0
    return tm


def _dot(a, b):
    return jnp.dot(a.astype(_MXU), b.astype(_MXU), preferred_element_type=_F32)


def _dot_nt(a, b):
    return lax.dot_general(a.astype(_MXU), b.astype(_MXU), (((1,), (1,)), ((), ())), preferred_element_type=_F32)


def _dot_tn(a, b):
    return lax.dot_general(a.astype(_MXU), b.astype(_MXU), (((0,), (0,)), ((), ())), preferred_element_type=_F32)


def _gelu(x):
    return 0.5 * x * (1.0 + jnp.tanh(0.7978845608028654 * (x + 0.044715 * (x * x * x))))


def _sigmoid(x):
    return 0.5 * jnp.tanh(0.5 * x) + 0.5


def _silu(x):
    return x * _sigmoid(x)


def _log_sigmoid(x):
    return jnp.minimum(x, 0.0) - jnp.log(1.0 + jnp.exp(-jnp.abs(x)))


def _norm_mod(x, a, s):
    ms = jnp.mean(x * x, axis=-1, keepdims=True)
    return x * lax.rsqrt(ms + NORM_EPS) * a + s


def _adaln_kernel(c_ref, w_ref, b_ref, o_ref):
    o_ref[...] = _dot(_silu(c_ref[...]), w_ref[...]) + b_ref[...]


def _adaln(cond, w, b):
    depth, d, n = w.shape
    r = cond.shape[0]
    tn = n // N_MOD
    return pl.pallas_call(
        _adaln_kernel,
        grid=(depth, N_MOD),
        in_specs=[
            pl.BlockSpec((r, d), lambda l, j: (0, 0)),
            pl.BlockSpec((None, d, tn), lambda l, j: (l, 0, j)),
            pl.BlockSpec((None, 1, tn), lambda l, j: (l, 0, j)),
        ],
        out_specs=pl.BlockSpec((None, r, tn), lambda l, j: (l, 0, j)),
        out_shape=jax.ShapeDtypeStruct((depth, r, n), _F32),
        compiler_params=_cparams("parallel", "parallel"),
        name="adaln",
    )(cond, w, b)


def _pre_kernel(x_ref, a_ref, s_ref, w_ref, o_ref):
    h = _norm_mod(x_ref[...], a_ref[...], s_ref[...])
    o_ref[...] = _dot(h, w_ref[...]).astype(o_ref.dtype)


def _pre_even_kernel(x_ref, a_ref, s_ref, w_ref, wlr_ref, up_ref, db_ref, o_ref, la_ref):
    h = _norm_mod(x_ref[...], a_ref[...], s_ref[...]).astype(_MXU)
    o_ref[...] = _dot(h, w_ref[...]).astype(o_ref.dtype)
    lr = _dot(h, wlr_ref[...])
    la_ref[...] = _log_sigmoid(_dot(lr, up_ref[...]) + db_ref[...]) * (1.0 / GLA_TAU)


def _pre(x, a, s, w, decay=None):
    bsz, t, d = x.shape
    n = w.shape[1]
    tm = _row_tile(t, PROJ_ROWS)
    row = lambda b, i: (b, i, 0)
    vec = pl.BlockSpec((None, 1, d), lambda b, i: (b, 0, 0))
    full = lambda arr: pl.BlockSpec(arr.shape, lambda b, i: (0,) * arr.ndim)
    in_specs = [pl.BlockSpec((None, tm, d), row), vec, vec, full(w)]
    out_specs = [pl.BlockSpec((None, tm, n), row)]
    out_shape = [jax.ShapeDtypeStruct((bsz, t, n), _ACT)]
    args = [x, a, s, w]
    body = _pre_kernel
    if decay is not None:
        body = _pre_even_kernel
        n_la = decay[1].shape[1]
        in_specs += [full(v) for v in decay]
        out_specs.append(pl.BlockSpec((None, tm, n_la), row))
        out_shape.append(jax.ShapeDtypeStruct((bsz, t, n_la), _F32))
        args += list(decay)
    return pl.pallas_call(
        body, grid=(bsz, t // tm), in_specs=in_specs, out_specs=out_specs, out_shape=out_shape,
        compiler_params=_cparams("parallel", "parallel"), name="pre_proj",
    )(*args)


def _bmm(a, b, dims):
    return lax.dot_general(a.astype(_MXU), b.astype(_MXU), (dims, ((0,), (0,))), preferred_element_type=_F32)


def _cumsum_chunks(tri, x):
    hi = x.astype(_MXU)
    r1 = x - hi.astype(_F32)
    mid = r1.astype(_MXU)
    lo = (r1 - mid.astype(_F32)).astype(_MXU)
    mm = ((2,), (1,))
    return _bmm(tri, hi, mm) + _bmm(tri, mid, mm) + _bmm(tri, lo, mm)


def _gla_block(qkv_ref, la_ref, o_ref, st_ref, sin_ref, keep, scale, reverse):
    c = GLA_CHUNK
    tb = qkv_ref.shape[0]
    n = tb // c
    key = la_ref.shape[-1]
    dv = (qkv_ref.shape[-1] - 2 * key) // GLA_HEADS
    dk = key // GLA_HEADS
    tri = jnp.broadcast_to(keep.astype(_MXU)[None], (n, c, c))
    b = _cumsum_chunks(tri, la_ref[...].reshape(n, c, key))
    btot = b[:, 0:1] if reverse else b[:, c - 1:c]
    q = qkv_ref[:, 0:key].astype(_F32).reshape(n, c, key)
    k = qkv_ref[:, key:2 * key].astype(_F32).reshape(n, c, key)
    qd = q * (scale * jnp.exp(b))
    kd = k * jnp.exp(-b)
    kr = k * jnp.exp(btot - b)
    dec = jnp.exp(btot)
    pair = 2 * dk
    lane_lo = lax.broadcasted_iota(jnp.int32, (n, c, pair), 2) < dk
    order = range(n - 1, -1, -1) if reverse else range(n)
    for p in range(GLA_HEADS // 2):
        lanes = slice(p * pair, (p + 1) * pair)
        qd_p, kd_p, kr_p = qd[:, :, lanes], kd[:, :, lanes], kr[:, :, lanes]
        vs = [qkv_ref[:, 2 * key + h * dv:2 * key + (h + 1) * dv].reshape(n, c, dv) for h in (2 * p, 2 * p + 1)]
        km = jnp.concatenate([jnp.where(lane_lo, kr_p, 0.0), jnp.where(lane_lo, 0.0, kr_p)], axis=1)
        upd = _bmm(jnp.concatenate(vs, axis=1), km, ((1,), (1,)))
        st = st_ref[p]
        for j in order:
            sin_ref[j] = st
            st = dec[j, :, lanes] * st + upd[j]
        st_ref[p] = st
        s_in = sin_ref[...]
        for s in range(2):
            h = 2 * p + s
            qm = jnp.where(lane_lo if s == 0 else jnp.logical_not(lane_lo), qd_p, 0.0)
            att = jnp.where(keep[None], _bmm(qm, kd_p, ((2,), (2,))), 0.0)
            o = _bmm(att, vs[s], ((2,), (1,))) + _bmm(qm, s_in, ((2,), (2,)))
            o_ref[:, h * dv:(h + 1) * dv] = o.reshape(tb, dv).astype(o_ref.dtype)


def _gla_kernel(qkv_f_ref, la_f_ref, qkv_b_ref, la_b_ref, s0f_ref, s0b_ref, of_ref, ob_ref, sf_ref, sb_ref, sin_ref, *,
                scale):
    i = pl.program_id(1)

    @pl.when(i == 0)
    def _():
        sf_ref[...] = s0f_ref[...]
        sb_ref[...] = s0b_ref[...]

    c = GLA_CHUNK
    ri = lax.broadcasted_iota(jnp.int32, (c, c), 0)
    ci = lax.broadcasted_iota(jnp.int32, (c, c), 1)
    _gla_block(qkv_f_ref, la_f_ref, of_ref, sf_ref, sin_ref, ri >= ci, scale, False)
    _gla_block(qkv_b_ref, la_b_ref, ob_ref, sb_ref, sin_ref, ri <= ci, scale, True)


def _gla(z, la, s0f, s0b, qkv_block, scale):
    bsz, t, _ = z.shape
    key = la.shape[-1] // 2
    st_shape = s0f.shape[1:]
    val = GLA_HEADS * st_shape[1]
    width = 2 * key + val
    tb = _row_tile(t, 512)
    nt = t // tb
    st_spec = pl.BlockSpec((None,) + st_shape, lambda b, i: (b, 0, 0, 0))
    outs = pl.pallas_call(
        functools.partial(_gla_kernel, scale=scale),
        grid=(bsz, nt),
        in_specs=[
            pl.BlockSpec((None, tb, width), lambda b, i: (b, i, qkv_block)),
            pl.BlockSpec((None, tb, key), lambda b, i: (b, i, 0)),
            pl.BlockSpec((None, tb, width), lambda b, i: (b, nt - 1 - i, qkv_block)),
            pl.BlockSpec((None, tb, key), lambda b, i: (b, nt - 1 - i, 1)),
            st_spec, st_spec,
        ],
        out_specs=[
            pl.BlockSpec((None, tb, val), lambda b, i: (b, i, 0)),
            pl.BlockSpec((None, tb, val), lambda b, i: (b, nt - 1 - i, 0)),
            st_spec, st_spec,
        ],
        out_shape=[
            jax.ShapeDtypeStruct((bsz, t, val), _ACT),
            jax.ShapeDtypeStruct((bsz, t, val), _ACT),
            jax.ShapeDtypeStruct(s0f.shape, _F32),
            jax.ShapeDtypeStruct(s0b.shape, _F32),
        ],
        scratch_shapes=[pltpu.VMEM((tb // GLA_CHUNK,) + st_shape[1:], _F32)],
        compiler_params=_cparams("parallel", "arbitrary"),
        name="gla_scan",
    )(z, la, z, la, s0f, s0b)
    return outs


def _lru_coeffs(x_ref, prev_ref, next_ref, first, last, cw_ref, cb_ref, wg_ref, ba_ref, bx_ref, sp_ref, pad_scr):
    tm, hd = x_ref.shape
    steps = tm // SUBLANES
    pitch = pad_scr.shape[0] // SUBLANES
    for sg in range(SUBLANES):
        pad_scr[sg * pitch:sg * pitch + steps, :] = x_ref[sg * steps:(sg + 1) * steps, :].astype(_F32)
    x = jnp.stack([pad_scr[pl.ds(s, SUBLANES, stride=pitch), :] for s in range(steps)], axis=0)
    prev = jnp.where(first, 0.0, prev_ref[...].astype(_F32)[HALO - SUBLANES:HALO])
    nxt = jnp.where(last, 0.0, next_ref[...].astype(_F32)[0:SUBLANES])
    seg = lax.broadcasted_iota(jnp.int32, (SUBLANES, 1), 0)

    def from_prev_segment(v, fill):
        return jnp.where(seg == 0, fill, pltpu.roll(v, 1, 0))

    def from_next_segment(v, fill):
        return jnp.where(seg == SUBLANES - 1, fill, pltpu.roll(v, SUBLANES - 1, 0))

    m1_edge = from_prev_segment(x[steps - 1], prev[SUBLANES - 1:SUBLANES])
    m2_edge = from_prev_segment(x[steps - 2], prev[SUBLANES - 2:SUBLANES - 1])
    p1_edge = from_next_segment(x[0], nxt[0:1])
    x_m1 = jnp.concatenate([m1_edge[None], x[:steps - 1]], axis=0)
    x_m2 = jnp.concatenate([m2_edge[None], m1_edge[None], x[:steps - 2]], axis=0)
    x_p1 = jnp.concatenate([x[1:], p1_edge[None]], axis=0)
    xc = cb_ref[...] + cw_ref[0:1] * x_m2 + cw_ref[1:2] * x_m1 + cw_ref[2:3] * x + cw_ref[3:4] * x_p1
    xc = xc.reshape(tm, hd)
    g = _dot(xc, wg_ref[...])
    rg = _sigmoid(g[:, :hd] + ba_ref[...])
    ig = _sigmoid(g[:, hd:] + bx_ref[...])
    log_a = rg * sp_ref[...]
    th = jnp.tanh(0.5 * log_a)
    rcp = 1.0 / (1.0 - th)
    a = (1.0 + th) * rcp
    bb = (2.0 * jnp.sqrt(-th) * rcp) * (ig * xc)
    shape = (steps, SUBLANES, hd)
    return a.reshape(shape), bb.reshape(shape)


def _lru_scan_tile(a, bb, h_scr, p_scr, c_scr, pad_scr, h_ref, carry_ref, reverse):
    steps = a.shape[0]
    pitch = pad_scr.shape[0] // SUBLANES
    order = range(steps - 1, -1, -1) if reverse else range(steps)
    h = None
    for s in order:
        h = bb[s] if h is None else a[s] * h + bb[s]
        prod = a[s] if s == order[0] else a[s] * prod
        h_scr[s] = h
        p_scr[s] = prod
    c = carry_ref[...]
    for sg in (range(SUBLANES - 1, -1, -1) if reverse else range(SUBLANES)):
        c_scr[sg:sg + 1, :] = c
        c = h[sg:sg + 1] + prod[sg:sg + 1] * c
    carry_ref[...] = c
    c_in = c_scr[...]
    for s in range(steps):
        pad_scr[pl.ds(s, SUBLANES, stride=pitch), :] = h_scr[s] + p_scr[s] * c_in
    for sg in range(SUBLANES):
        h_ref[sg * steps:(sg + 1) * steps, :] = pad_scr[sg * pitch:sg * pitch + steps, :].astype(h_ref.dtype)


def _lru_kernel(xf_ref, xf_prev_ref, xf_next_ref, xb_ref, xb_prev_ref, xb_next_ref, cw_ref, cb_ref, wg_ref, ba_ref,
                bx_ref, sp_ref, h0f_ref, h0b_ref, hf_ref, hb_ref, cf_ref, cbk_ref, a_scr, b_scr, c_scr, pad_scr):
    i = pl.program_id(2)
    nt = pl.num_programs(2)

    @pl.when(i == 0)
    def _():
        cf_ref[...] = h0f_ref[...]
        cbk_ref[...] = h0b_ref[...]

    a, bb = _lru_coeffs(xf_ref, xf_prev_ref, xf_next_ref, i == 0, i == nt - 1, cw_ref, cb_ref, wg_ref.at[0],
                        ba_ref.at[0], bx_ref.at[0], sp_ref.at[0], pad_scr)
    _lru_scan_tile(a, bb, a_scr, b_scr, c_scr, pad_scr, hf_ref, cf_ref, False)
    a, bb = _lru_coeffs(xb_ref, xb_prev_ref, xb_next_ref, i == nt - 1, i == 0, cw_ref, cb_ref, wg_ref.at[1],
                        ba_ref.at[1], bx_ref.at[1], sp_ref.at[1], pad_scr)
    _lru_scan_tile(a, bb, a_scr, b_scr, c_scr, pad_scr, hb_ref, cbk_ref, True)


def _lru(z, x_block, conv_w, conv_b, wg, ba, bx, sp, h0f, h0b):
    bsz, t, _ = z.shape
    width = conv_w.shape[1]
    hd = width // LRU_HEADS
    assert hd == LANES
    tm = _row_tile(t, LRU_ROWS)
    nt = t // tm
    gpt = tm // SUBLANES
    hpt = tm // HALO
    nhb = t // HALO
    pitch = (gpt + SUBLANES - 1) // SUBLANES * SUBLANES
    pitch += SUBLANES * (1 - (pitch // SUBLANES) % 2)
    col = lambda h: x_block * LRU_HEADS + h

    def tile(rev):
        return lambda b, h, i: (b, (nt - 1 - i) if rev else i, col(h))

    def prev(rev):
        return lambda b, h, i: (b, jnp.maximum(((nt - 1 - i) if rev else i) * hpt - 1, 0), col(h))

    def nxt(rev):
        return lambda b, h, i: (b, jnp.minimum((((nt - 1 - i) if rev else i) + 1) * hpt, nhb - 1), col(h))

    lane = lambda rows: pl.BlockSpec((rows, hd), lambda b, h, i: (0, h))
    dirs = pl.BlockSpec((2, 1, hd), lambda b, h, i: (0, 0, h))
    st_spec = pl.BlockSpec((None, 1, hd), lambda b, h, i: (b, 0, h))
    in_specs = []
    for rev in (False, True):
        in_specs += [pl.BlockSpec((None, tm, hd), tile(rev)),
                     pl.BlockSpec((None, HALO, hd), prev(rev)),
                     pl.BlockSpec((None, HALO, hd), nxt(rev))]
    in_specs += [lane(conv_w.shape[0]), lane(1),
                 pl.BlockSpec((2, None, hd, 2 * hd), lambda b, h, i: (0, h, 0, 0)),
                 dirs, dirs, dirs, st_spec, st_spec]
    return pl.pallas_call(
        _lru_kernel,
        grid=(bsz, LRU_HEADS, nt),
        in_specs=in_specs,
        out_specs=[
            pl.BlockSpec((None, tm, hd), lambda b, h, i: (b, i, h)),
            pl.BlockSpec((None, tm, hd), lambda b, h, i: (b, nt - 1 - i, h)),
            st_spec, st_spec,
        ],
        out_shape=[
            jax.ShapeDtypeStruct((bsz, t, width), _ACT),
            jax.ShapeDtypeStruct((bsz, t, width), _ACT),
            jax.ShapeDtypeStruct((bsz, 1, width), _F32),
            jax.ShapeDtypeStruct((bsz, 1, width), _F32),
        ],
        scratch_shapes=[pltpu.VMEM((gpt, SUBLANES, hd), _F32), pltpu.VMEM((gpt, SUBLANES, hd), _F32),
                        pltpu.VMEM((SUBLANES, hd), _F32), pltpu.VMEM((SUBLANES * pitch, hd), _F32)],
        compiler_params=_cparams("parallel", "parallel", "arbitrary"),
        name="lru_scan",
    )(z, z, z, z, z, z, conv_w, conv_b, wg, ba, bx, sp, h0f, h0b)


def _residual_ffn_prenorm(y, x_ref, gate_ref, a_ref, s_ref, wr_ref, xo_ref, h_ref, lg_ref):
    xn = x_ref[...] + gate_ref[...] * y
    xo_ref[...] = xn
    h = _norm_mod(xn, a_ref[...], s_ref[...]).astype(_MXU)
    h_ref[...] = h
    lg_ref[...] = _dot(h, wr_ref[...])


def _post_even_kernel(za_ref, of_ref, ob_ref, cw_ref, cb_ref, ng_ref, w_ref, x_ref, gate_ref, a_ref, s_ref, wr_ref,
                      xo_ref, h_ref, lg_ref, y_scr, *, row_len):
    tm = za_ref.shape[0]
    scw = cw_ref.shape[1]
    val = ng_ref.shape[1]
    dv = val // GLA_HEADS
    bg = za_ref[:, 0:scw].astype(_F32)
    cx = za_ref[:, scw:2 * scw].astype(_F32) * za_ref[:, 2 * scw:3 * scw].astype(_F32)
    r = lax.broadcasted_iota(jnp.int32, (tm, 1), 0) % row_len
    left = jnp.where(r == 0, 0.0, pltpu.roll(cx, 1, 0))
    right = jnp.where(r == row_len - 1, 0.0, pltpu.roll(cx, tm - 1, 0))
    conv = cb_ref[...] + cw_ref[0:1] * left + cw_ref[1:2] * cx + cw_ref[2:3] * right
    acc = _dot(bg * conv, w_ref[0:scw, :])
    for h in range(GLA_HEADS):
        lanes = slice(h * dv, (h + 1) * dv)
        o = of_ref[:, lanes].astype(_F32) + ob_ref[:, lanes].astype(_F32)
        o = o * lax.rsqrt(jnp.mean(o * o, axis=-1, keepdims=True) + NORM_EPS)
        gate = za_ref[:, 3 * scw + h * dv:3 * scw + (h + 1) * dv].astype(_F32)
        y_scr[:, lanes] = o * ng_ref[:, lanes] * _silu(gate)
    acc = acc + _dot(y_scr[...], w_ref[scw:scw + val, :])
    _residual_ffn_prenorm(acc, x_ref, gate_ref, a_ref, s_ref, wr_ref, xo_ref, h_ref, lg_ref)


def _post_odd_kernel(zc_ref, hf_ref, hb_ref, lng_ref, sgw_ref, sgb_ref, w_ref, x_ref, gate_ref, a_ref, s_ref, wr_ref,
                     xo_ref, h_ref, lg_ref, y_scr):
    tm = zc_ref.shape[0]
    width = hf_ref.shape[1]
    gdim = width // SG_GROUPS
    y_rec = (hf_ref[...].astype(_F32) + hb_ref[...].astype(_F32)) * _gelu(zc_ref[:, 0:width].astype(_F32))
    acc = _dot(y_rec, w_ref[0:width, :])
    gv = _gelu(zc_ref[:, 2 * width:3 * width].astype(_F32))
    gc = gv - jnp.mean(gv, axis=-1, keepdims=True)
    vb = gc * lax.rsqrt(jnp.mean(gc * gc, axis=-1, keepdims=True) + NORM_EPS) * lng_ref[...]
    for n in range(tm // SG_CHUNK):
        rows = slice(n * SG_CHUNK, (n + 1) * SG_CHUNK)
        for g in range(SG_GROUPS):
            lanes = slice(g * gdim, (g + 1) * gdim)
            mixed = _dot(sgw_ref[g], vb[rows, lanes]) + sgb_ref[g]
            u = zc_ref[rows, width + g * gdim:width + (g + 1) * gdim].astype(_F32)
            y_scr[rows, lanes] = _gelu(u) * mixed
    acc = acc + _dot(y_scr[...], w_ref[width:2 * width, :])
    _residual_ffn_prenorm(acc, x_ref, gate_ref, a_ref, s_ref, wr_ref, xo_ref, h_ref, lg_ref)


def _post(body, z, z_width, seq_in, consts, w_out, x, gate, a, s, wr):
    bsz, t, d = x.shape
    tm = _row_tile(t, PROJ_ROWS)
    row = lambda b, i: (b, i, 0)
    vec = pl.BlockSpec((None, 1, d), lambda b, i: (b, 0, 0))
    full = lambda arr: pl.BlockSpec(arr.shape, lambda b, i: (0,) * arr.ndim)
    width = seq_in[0].shape[-1]
    in_specs = [pl.BlockSpec((None, tm, z_width), row)]
    in_specs += [pl.BlockSpec((None, tm, width), row) for _ in seq_in]
    in_specs += [full(v) for v in consts] + [full(w_out), pl.BlockSpec((None, tm, d), row), vec, vec, vec, full(wr)]
    return pl.pallas_call(
        body,
        grid=(bsz, t // tm),
        in_specs=in_specs,
        out_specs=[
            pl.BlockSpec((None, tm, d), row),
            pl.BlockSpec((None, tm, d), row),
            pl.BlockSpec((None, tm, LANES), row),
        ],
        out_shape=[
            jax.ShapeDtypeStruct((bsz, t, d), _F32),
            jax.ShapeDtypeStruct((bsz, t, d), _MXU),
            jax.ShapeDtypeStruct((bsz, t, LANES), _F32),
        ],
        scratch_shapes=[pltpu.VMEM((tm, width), _F32)],
        compiler_params=_cparams("parallel", "parallel"),
        name="post_proj",
    )(z, *seq_in, *consts, w_out, x, gate, a, s, wr)


def _moe_ffn_kernel(x_ref, g_ref, wg_ref, wu_ref, wd_ref, o_ref, wg_scr, wu_scr, wd_scr):
    @pl.when(pl.program_id(1) == 0)
    def _():
        wg_scr[...] = wg_ref[...].astype(_MXU)
        wu_scr[...] = wu_ref[...].astype(_MXU)
        wd_scr[...] = wd_ref[...].astype(_MXU)

    bb, cap, d = x_ref.shape
    x = x_ref[...].reshape(bb * cap, d)
    gate = _dot(x, wg_scr[...])
    up = _dot(x, wu_scr[...])
    out = _dot(_silu(gate) * up, wd_scr[...]) * g_ref[...].reshape(bb * cap, 1)
    o_ref[...] = out.reshape(bb, cap, d).astype(o_ref.dtype)


def _moe_ffn(xg, g, wg, wu, wd, layer):
    bsz, ne, cap, d = xg.shape
    f = wg.shape[-1]
    bb = max(1, min(bsz, MOE_ROWS // cap))
    assert bsz % bb == 0
    tok = lambda e, b: (b, e, 0, 0)
    wsel = lambda e, b: (layer, e, 0, 0)
    return pl.pallas_call(
        _moe_ffn_kernel,
        grid=(ne, bsz // bb),
        in_specs=[
            pl.BlockSpec((bb, None, cap, d), tok),
            pl.BlockSpec((bb, None, cap, 1), tok),
            pl.BlockSpec((None, None, d, f), wsel),
            pl.BlockSpec((None, None, d, f), wsel),
            pl.BlockSpec((None, None, f, d), wsel),
        ],
        out_specs=pl.BlockSpec((bb, None, cap, d), tok),
        out_shape=jax.ShapeDtypeStruct((bsz, ne, cap, d), _MXU),
        scratch_shapes=[pltpu.VMEM((d, f), _MXU), pltpu.VMEM((d, f), _MXU), pltpu.VMEM((f, d), _MXU)],
        compiler_params=pltpu.CompilerParams(dimension_semantics=("parallel", "arbitrary"),
                                             vmem_limit_bytes=MOE_VMEM_LIMIT),
        name="moe_ffn",
    )(xg, g, wg, wu, wd)


def _combine_kernel(lo_ref, hi_ref, tok_ref, y_ref, x_ref, gate_ref, fg_ref, o_ref, acc_ref, *, final_norm):
    b = pl.program_id(0)
    j = pl.program_id(1)
    tm = acc_ref.shape[0]
    ch = tok_ref.shape[-1]
    acc_ref[...] = jnp.zeros_like(acc_ref)
    rows = j * tm + lax.broadcasted_iota(jnp.int32, (tm, ch), 0)

    def body(c, carry):
        onehot = jnp.where(rows == tok_ref[c], 1.0, 0.0)
        acc_ref[...] += _dot(onehot, y_ref[pl.ds(pl.multiple_of(c * ch, ch), ch), :])
        return carry

    lax.fori_loop(lo_ref[b, j], hi_ref[b, j] + 1, body, 0)
    xn = x_ref[...] + gate_ref[...] * acc_ref[...]
    if final_norm:
        ms = jnp.mean(xn * xn, axis=-1, keepdims=True)
        xn = xn * lax.rsqrt(ms + NORM_EPS) * fg_ref[...]
    o_ref[...] = xn


def _combine(ys, toks, x, gate, final_g):
    bsz, t, d = x.shape
    p = ys.shape[1]
    tm = _row_tile(t, PROJ_ROWS)
    ch = _row_tile(p)
    nt, nc = t // tm, p // ch
    bounds = jnp.arange(nt + 1, dtype=jnp.int32) * tm
    cut = jnp.sum((toks[:, :, None] < bounds[None, None, :]).astype(jnp.int32), axis=1)
    c_lo = jnp.minimum(cut[:, :-1] // ch, nc - 1)
    c_hi = jnp.maximum((cut[:, 1:] - 1) // ch, c_lo)
    fg = jnp.ones((1, d), _F32) if final_g is None else final_g.reshape(1, d)
    grid_spec = pltpu.PrefetchScalarGridSpec(
        num_scalar_prefetch=2,
        grid=(bsz, nt),
        in_specs=[
            pl.BlockSpec((None, nc, 1, ch), lambda b, j, lo, hi: (b, 0, 0, 0)),
            pl.BlockSpec((None, p, d), lambda b, j, lo, hi: (b, 0, 0)),
            pl.BlockSpec((None, tm, d), lambda b, j, lo, hi: (b, j, 0)),
            pl.BlockSpec((None, 1, d), lambda b, j, lo, hi: (b, 0, 0)),
            pl.BlockSpec((1, d), lambda b, j, lo, hi: (0, 0)),
        ],
        out_specs=pl.BlockSpec((None, tm, d), lambda b, j, lo, hi: (b, j, 0)),
        scratch_shapes=[pltpu.VMEM((tm, d), _F32)],
    )
    return pl.pallas_call(
        functools.partial(_combine_kernel, final_norm=final_g is not None),
        grid_spec=grid_spec,
        out_shape=jax.ShapeDtypeStruct((bsz, t, d), _F32),
        compiler_params=pltpu.CompilerParams(dimension_semantics=("parallel", "arbitrary"),
                                             vmem_limit_bytes=MOE_VMEM_LIMIT),
        name="moe_combine",
    )(c_lo, c_hi, toks.reshape(bsz, nc, 1, ch), ys, x, gate, fg)


def _ec_moe(x, gate, h, logits, wg, wu, wd, layer, final_g):
    bsz, t, d = h.shape
    ne = logits.shape[-1]
    cap = EC_CAPACITY * t // ne
    aff = jax.nn.softmax(logits, axis=-1)
    keys = jnp.transpose(-aff, (1, 0, 2)).reshape(t, bsz * ne)
    token = lax.broadcasted_iota(jnp.int32, (t, bsz * ne), 0)
    keys, token = lax.sort((keys, token), dimension=0, num_keys=1, is_stable=True)
    g = jnp.transpose((-keys[:cap]).reshape(cap, bsz, ne), (1, 2, 0))
    idx = jnp.transpose(token[:cap].reshape(cap, bsz, ne), (1, 2, 0))
    xg = jax.vmap(lambda hb, ib: hb[ib])(h, idx)
    y = _moe_ffn(xg, g[..., None], wg, wu, wd, layer).reshape(bsz, ne * cap, d)
    slots = jnp.broadcast_to(jnp.arange(ne * cap, dtype=jnp.int32), (bsz, ne * cap))
    toks, perm = lax.sort_key_val(idx.reshape(bsz, ne * cap).astype(jnp.int32), slots, dimension=-1)
    ys = jax.vmap(lambda yb, pb: yb[pb])(y, perm)
    return _combine(ys, toks, x, gate, final_g)


def _pad_to(w, axis, size):
    pad = [(0, 0)] * w.ndim
    pad[axis] = (0, size - w.shape[axis])
    return jnp.pad(w, pad)


def kernel(x, c, ctx, c_ctx, ada_w, ada_b, norm_mix_g, norm_ffn_g, norm_final_g, ev_w_in, ev_conv_w, ev_conv_b, ev_decay_up, ev_decay_b, ev_norm_g, ev_w_out, od_w_in, od_conv_w, od_conv_b, od_gate_a_w, od_gate_a_b, od_gate_x_w, od_gate_x_b, od_lambda, od_sg_norm_g, od_sg_w, od_sg_b, od_w_out, moe_router, moe_w_gate, moe_w_up, moe_w_down):
    depth = ada_w.shape[0]
    bsz, seq, d = x.shape
    ne = moe_router.shape[-1]
    x_lat, x_ctx = x, ctx
    cond = _pad_to(jnp.concatenate([c, c_ctx[None, :]], axis=0), 0, -(-(bsz + 1) // SUBLANES) * SUBLANES)
    ada = _adaln(cond, ada_w, ada_b.reshape(depth, 1, N_MOD * d))

    def vecs(m):
        return [v.reshape(bsz, 1, d) for v in m]

    for l in range(depth):
        last = l == depth - 1
        even = l % 2 == 0
        i = l // 2
        mod = [ada[l, :bsz, k * d:(k + 1) * d] for k in range(N_MOD)]
        mod_c = [jnp.broadcast_to(ada[l, bsz:bsz + 1, k * d:(k + 1) * d], (bsz, d)) for k in range(N_MOD)]
        wr = _pad_to(moe_router[l], 1, LANES).astype(_MXU)

        if even:
            w = ev_w_in[i]
            scw = ev_conv_w.shape[-1]
            key = ev_decay_up.shape[-1]
            val = ev_norm_g.shape[-1]
            o_q = 3 * scw
            o_g = o_q + key
            o_k = o_g + val
            o_v = o_k + key
            o_lr = o_v + val
            w_in = jnp.concatenate([w[:, :o_q], w[:, o_g:o_k], w[:, o_q:o_g], w[:, o_k:o_lr]], axis=1).astype(_MXU)
            w_lr = _pad_to(w[:, o_lr:], 1, LANES).astype(_MXU)
            up = jnp.zeros((LANES, 2 * key), _F32)
            up = up.at[:GLA_RANK, :key].set(ev_decay_up[i, 0]).at[GLA_RANK:2 * GLA_RANK, key:].set(ev_decay_up[i, 1])
            decay = (w_lr, up.astype(_MXU), ev_decay_b[i].reshape(1, 2 * key))
            w_out = ev_w_out[i].astype(_MXU)
            conv_w = _pad_to(ev_conv_w[i], 0, SUBLANES)
            consts = (conv_w, ev_conv_b[i].reshape(1, scw), ev_norm_g[i].reshape(1, val))
            epi_width = 3 * scw + val
            assert (3 * scw + val) % (2 * key + val) == 0
            qkv_block = epi_width // (2 * key + val)
            dk, dv = key // GLA_HEADS, val // GLA_HEADS
            scale = dk ** -0.5
            zero_st = jnp.zeros((bsz, GLA_HEADS // 2, dv, 2 * dk), _F32)

            def mixer(xs, m, s0, row_len):
                sh, sc = vecs(m[:2])
                z, la = _pre(xs, norm_mix_g[l] * (1.0 + sc), sh, w_in, decay)
                o_f, o_b, s_f, s_b = _gla(z, la, s0[0], s0[1], qkv_block, scale)
                body = functools.partial(_post_even_kernel, row_len=row_len)
                return (body, z, epi_width, (o_f, o_b), consts, w_out), (s_f, s_b)

            ctx_args, state = mixer(x_ctx, mod_c, (zero_st, zero_st), x_ctx.shape[1])
            lat_args, _ = mixer(x_lat, mod, state, GRID_W)
        else:
            w = od_w_in[i]
            width = od_conv_w.shape[-1]
            hd = width // LRU_HEADS
            w_in = jnp.concatenate([w[:, width:], w[:, :width]], axis=1).astype(_MXU)
            w_out = od_w_out[i].astype(_MXU)
            conv_w = _pad_to(od_conv_w[i], 0, SUBLANES)
            conv_b = od_conv_b[i].reshape(1, width)
            wgate = jnp.concatenate([od_gate_a_w[i], od_gate_x_w[i]], axis=-1).astype(_MXU)
            ba = od_gate_a_b[i].reshape(2, 1, width)
            bx = od_gate_x_b[i].reshape(2, 1, width)
            sp = (-LRU_C * jax.nn.softplus(-od_lambda[i])).reshape(2, 1, width)
            sgb = jnp.broadcast_to(od_sg_b[i][:, :, None], (SG_GROUPS, SG_CHUNK, width // SG_GROUPS))
            consts = (od_sg_norm_g[i].reshape(1, width), od_sg_w[i].astype(_MXU), sgb)
            zero_st = jnp.zeros((bsz, 1, width), _F32)

            def mixer(xs, m, s0, row_len):
                sh, sc = vecs(m[:2])
                (z,) = _pre(xs, norm_mix_g[l] * (1.0 + sc), sh, w_in)
                h_f, h_b, c_f, c_b = _lru(z, 3, conv_w, conv_b, wgate, ba, bx, sp, s0[0], s0[1])
                return (_post_odd_kernel, z, 3 * width, (h_f, h_b), consts, w_out), (c_f, c_b)

            ctx_args, state = mixer(x_ctx, mod_c, (zero_st, zero_st), None)
            lat_args, _ = mixer(x_lat, mod, state, None)

        def post(args, xs, m, final_g):
            gate1, sh, sc, gate2 = vecs(m[2:])
            xs, h, logits = _post(*args, xs, gate1, norm_ffn_g[l] * (1.0 + sc), sh, wr)
            return _ec_moe(xs, gate2, h, logits[..., :ne], moe_w_gate, moe_w_up, moe_w_down, l, final_g)

        x_lat = post(lat_args, x_lat, mod, norm_final_g if last else None)
        if not last:
            x_ctx = post(ctx_args, x_ctx, mod_c, None)
    return x_lat
```

```python
import functools

import jax
import jax.numpy as jnp
from jax import lax
from jax.experimental import pallas as pl
from jax.experimental.pallas import tpu as pltpu

_MXU = jnp.bfloat16
_ACT = jnp.bfloat16
_F32 = jnp.float32
NORM_EPS = 1e-6
N_MOD = 6
GRID_W = 64
GLA_HEADS = 4
GLA_RANK = 16
GLA_TAU = 16.0
GLA_CHUNK = 64
LRU_HEADS = 4
LRU_C = 8.0
SG_GROUPS = 4
SG_CHUNK = 128
EC_CAPACITY = 2
LANES = 128
SUBLANES = 8
HALO = 16
VMEM_LIMIT = 48 * 1024 * 1024
MOE_VMEM_LIMIT = 56 * 1024 * 1024
MOE_ROWS = 512
SHORT_SEQ = 256
PROJ_ROWS = 512
LRU_ROWS = 1024


def _cparams(*sem):
    return pltpu.CompilerParams(dimension_semantics=sem, vmem_limit_bytes=VMEM_LIMIT)


def _row_tile(t, cap=256):
    tm = min(t, cap)
    assert t % tm == 0
    return tm


def _dot(a, b):
    return jnp.dot(a.astype(_MXU), b.astype(_MXU), preferred_element_type=_F32)


def _dot_nt(a, b):
    return lax.dot_general(a.astype(_MXU), b.astype(_MXU), (((1,), (1,)), ((), ())), preferred_element_type=_F32)


def _dot_tn(a, b):
    return lax.dot_general(a.astype(_MXU), b.astype(_MXU), (((0,), (0,)), ((), ())), preferred_element_type=_F32)


def _gelu(x):
    return 0.5 * x * (1.0 + jnp.tanh(0.7978845608028654 * (x + 0.044715 * (x * x * x))))


def _sigmoid(x):
    return 0.5 * jnp.tanh(0.5 * x) + 0.5


def _silu(x):
    return x * _sigmoid(x)


def _log_sigmoid(x):
    return jnp.minimum(x, 0.0) - jnp.log(1.0 + jnp.exp(-jnp.abs(x)))


def _norm_mod(x, a, s):
    ms = jnp.mean(x * x, axis=-1, keepdims=True)
    return x * lax.rsqrt(ms + NORM_EPS) * a + s


def _adaln_kernel(c_ref, w_ref, b_ref, o_ref):
    o_ref[...] = _dot(_silu(c_ref[...]), w_ref[...]) + b_ref[...]


def _adaln(cond, w, b):
    depth, d, n = w.shape
    r = cond.shape[0]
    tn = n // N_MOD
    return pl.pallas_call(
        _adaln_kernel,
        grid=(depth, N_MOD),
        in_specs=[
            pl.BlockSpec((r, d), lambda l, j: (0, 0)),
            pl.BlockSpec((None, d, tn), lambda l, j: (l, 0, j)),
            pl.BlockSpec((None, 1, tn), lambda l, j: (l, 0, j)),
        ],
        out_specs=pl.BlockSpec((None, r, tn), lambda l, j: (l, 0, j)),
        out_shape=jax.ShapeDtypeStruct((depth, r, n), _F32),
        compiler_params=_cparams("parallel", "parallel"),
        name="adaln",
    )(cond, w, b)


def _pre_kernel(x_ref, a_ref, s_ref, w_ref, o_ref):
    h = _norm_mod(x_ref[...], a_ref[...], s_ref[...])
    o_ref[...] = _dot(h, w_ref[...]).astype(o_ref.dtype)


def _pre_even_kernel(x_ref, a_ref, s_ref, w_ref, wlr_ref, up_ref, db_ref, o_ref, la_ref):
    h = _norm_mod(x_ref[...], a_ref[...], s_ref[...]).astype(_MXU)
    o_ref[...] = _dot(h, w_ref[...]).astype(o_ref.dtype)
    lr = _dot(h, wlr_ref[...])
    la_ref[...] = _log_sigmoid(_dot(lr, up_ref[...]) + db_ref[...]) * (1.0 / GLA_TAU)


def _pre(x, a, s, w, decay=None):
    bsz, t, d = x.shape
    n = w.shape[1]
    tm = _row_tile(t, PROJ_ROWS)
    row = lambda b, i: (b, i, 0)
    vec = pl.BlockSpec((None, 1, d), lambda b, i: (b, 0, 0))
    full = lambda arr: pl.BlockSpec(arr.shape, lambda b, i: (0,) * arr.ndim)
    in_specs = [pl.BlockSpec((None, tm, d), row), vec, vec, full(w)]
    out_specs = [pl.BlockSpec((None, tm, n), row)]
    out_shape = [jax.ShapeDtypeStruct((bsz, t, n), _ACT)]
    args = [x, a, s, w]
    body = _pre_kernel
    if decay is not None:
        body = _pre_even_kernel
        n_la = decay[1].shape[1]
        in_specs += [full(v) for v in decay]
        out_specs.append(pl.BlockSpec((None, tm, n_la), row))
        out_shape.append(jax.ShapeDtypeStruct((bsz, t, n_la), _F32))
        args += list(decay)
    return pl.pallas_call(
        body, grid=(bsz, t // tm), in_specs=in_specs, out_specs=out_specs, out_shape=out_shape,
        compiler_params=_cparams("parallel", "parallel"), name="pre_proj",
    )(*args)


def _bmm(a, b, dims):
    return lax.dot_general(a.astype(_MXU), b.astype(_MXU), (dims, ((0,), (0,))), preferred_element_type=_F32)


def _cumsum_chunks(tri, x):
    hi = x.astype(_MXU)
    r1 = x - hi.astype(_F32)
    mid = r1.astype(_MXU)
    lo = (r1 - mid.astype(_F32)).astype(_MXU)
    mm = ((2,), (1,))
    return _bmm(tri, hi, mm) + _bmm(tri, mid, mm) + _bmm(tri, lo, mm)


def _gla_block(qkv_ref, la_ref, o_ref, st_ref, sin_ref, keep, scale, reverse):
    c = GLA_CHUNK
    tb = qkv_ref.shape[0]
    n = tb // c
    key = la_ref.shape[-1]
    dv = (qkv_ref.shape[-1] - 2 * key) // GLA_HEADS
    dk = key // GLA_HEADS
    tri = jnp.broadcast_to(keep.astype(_MXU)[None], (n, c, c))
    b = _cumsum_chunks(tri, la_ref[...].reshape(n, c, key))
    btot = b[:, 0:1] if reverse else b[:, c - 1:c]
    q = qkv_ref[:, 0:key].astype(_F32).reshape(n, c, key)
    k = qkv_ref[:, key:2 * key].astype(_F32).reshape(n, c, key)
    qd = q * (scale * jnp.exp(b))
    kd = k * jnp.exp(-b)
    kr = k * jnp.exp(btot - b)
    dec = jnp.exp(btot)
    pair = 2 * dk
    lane_lo = lax.broadcasted_iota(jnp.int32, (n, c, pair), 2) < dk
    order = range(n - 1, -1, -1) if reverse else range(n)
    for p in range(GLA_HEADS // 2):
        lanes = slice(p * pair, (p + 1) * pair)
        qd_p, kd_p, kr_p = qd[:, :, lanes], kd[:, :, lanes], kr[:, :, lanes]
        vs = [qkv_ref[:, 2 * key + h * dv:2 * key + (h + 1) * dv].reshape(n, c, dv) for h in (2 * p, 2 * p + 1)]
        km = jnp.concatenate([jnp.where(lane_lo, kr_p, 0.0), jnp.where(lane_lo, 0.0, kr_p)], axis=1)
        upd = _bmm(jnp.concatenate(vs, axis=1), km, ((1,), (1,)))
        st = st_ref[p]
        for j in order:
            sin_ref[j] = st
            st = dec[j, :, lanes] * st + upd[j]
        st_ref[p] = st
        s_in = sin_ref[...]
        for s in range(2):
            h = 2 * p + s
            qm = jnp.where(lane_lo if s == 0 else jnp.logical_not(lane_lo), qd_p, 0.0)
            att = jnp.where(keep[None], _bmm(qm, kd_p, ((2,), (2,))), 0.0)
            o = _bmm(att, vs[s], ((2,), (1,))) + _bmm(qm, s_in, ((2,), (2,)))
            o_ref[:, h * dv:(h + 1) * dv] = o.reshape(tb, dv).astype(o_ref.dtype)


def _gla_kernel(qkv_f_ref, la_f_ref, qkv_b_ref, la_b_ref, s0f_ref, s0b_ref, of_ref, ob_ref, sf_ref, sb_ref, sin_ref, *,
                scale):
    i = pl.program_id(1)

    @pl.when(i == 0)
    def _():
        sf_ref[...] = s0f_ref[...]
        sb_ref[...] = s0b_ref[...]

    c = GLA_CHUNK
    ri = lax.broadcasted_iota(jnp.int32, (c, c), 0)
    ci = lax.broadcasted_iota(jnp.int32, (c, c), 1)
    _gla_block(qkv_f_ref, la_f_ref, of_ref, sf_ref, sin_ref, ri >= ci, scale, False)
    _gla_block(qkv_b_ref, la_b_ref, ob_ref, sb_ref, sin_ref, ri <= ci, scale, True)


def _gla(z, la, s0f, s0b, qkv_block, scale):
    bsz, t, _ = z.shape
    key = la.shape[-1] // 2
    st_shape = s0f.shape[1:]
    val = GLA_HEADS * st_shape[1]
    width = 2 * key + val
    tb = _row_tile(t, 512)
    nt = t // tb
    st_spec = pl.BlockSpec((None,) + st_shape, lambda b, i: (b, 0, 0, 0))
    outs = pl.pallas_call(
        functools.partial(_gla_kernel, scale=scale),
        grid=(bsz, nt),
        in_specs=[
            pl.BlockSpec((None, tb, width), lambda b, i: (b, i, qkv_block)),
            pl.BlockSpec((None, tb, key), lambda b, i: (b, i, 0)),
            pl.BlockSpec((None, tb, width), lambda b, i: (b, nt - 1 - i, qkv_block)),
            pl.BlockSpec((None, tb, key), lambda b, i: (b, nt - 1 - i, 1)),
            st_spec, st_spec,
        ],
        out_specs=[
            pl.BlockSpec((None, tb, val), lambda b, i: (b, i, 0)),
            pl.BlockSpec((None, tb, val), lambda b, i: (b, nt - 1 - i, 0)),
            st_spec, st_spec,
        ],
        out_shape=[
            jax.ShapeDtypeStruct((bsz, t, val), _ACT),
            jax.ShapeDtypeStruct((bsz, t, val), _ACT),
            jax.ShapeDtypeStruct(s0f.shape, _F32),
            jax.ShapeDtypeStruct(s0b.shape, _F32),
        ],
        scratch_shapes=[pltpu.VMEM((tb // GLA_CHUNK,) + st_shape[1:], _F32)],
        compiler_params=_cparams("parallel", "arbitrary"),
        name="gla_scan",
    )(z, la, z, la, s0f, s0b)
    return outs


def _lru_coeffs(x_ref, prev_ref, next_ref, first, last, cw_ref, cb_ref, wg_ref, ba_ref, bx_ref, sp_ref, pad_scr):
    tm, hd = x_ref.shape
    steps = tm // SUBLANES
    pitch = pad_scr.shape[0] // SUBLANES
    for sg in range(SUBLANES):
        pad_scr[sg * pitch:sg * pitch + steps, :] = x_ref[sg * steps:(sg + 1) * steps, :].astype(_F32)
    x = jnp.stack([pad_scr[pl.ds(s, SUBLANES, stride=pitch), :] for s in range(steps)], axis=0)
    prev = jnp.where(first, 0.0, prev_ref[...].astype(_F32)[HALO - SUBLANES:HALO])
    nxt = jnp.where(last, 0.0, next_ref[...].astype(_F32)[0:SUBLANES])
    seg = lax.broadcasted_iota(jnp.int32, (SUBLANES, 1), 0)

    def from_prev_segment(v, fill):
        return jnp.where(seg == 0, fill, pltpu.roll(v, 1, 0))

    def from_next_segment(v, fill):
        return jnp.where(seg == SUBLANES - 1, fill, pltpu.roll(v, SUBLANES - 1, 0))

    m1_edge = from_prev_segment(x[steps - 1], prev[SUBLANES - 1:SUBLANES])
    m2_edge = from_prev_segment(x[steps - 2], prev[SUBLANES - 2:SUBLANES - 1])
    p1_edge = from_next_segment(x[0], nxt[0:1])
    x_m1 = jnp.concatenate([m1_edge[None], x[:steps - 1]], axis=0)
    x_m2 = jnp.concatenate([m2_edge[None], m1_edge[None], x[:steps - 2]], axis=0)
    x_p1 = jnp.concatenate([x[1:], p1_edge[None]], axis=0)
    xc = cb_ref[...] + cw_ref[0:1] * x_m2 + cw_ref[1:2] * x_m1 + cw_ref[2:3] * x + cw_ref[3:4] * x_p1
    xc = xc.reshape(tm, hd)
    g = _dot(xc, wg_ref[...])
    rg = _sigmoid(g[:, :hd] + ba_ref[...])
    ig = _sigmoid(g[:, hd:] + bx_ref[...])
    log_a = rg * sp_ref[...]
    th = jnp.tanh(0.5 * log_a)
    rcp = 1.0 / (1.0 - th)
    a = (1.0 + th) * rcp
    bb = (2.0 * jnp.sqrt(-th) * rcp) * (ig * xc)
    shape = (steps, SUBLANES, hd)
    return a.reshape(shape), bb.reshape(shape)


def _lru_scan_tile(a, bb, h_scr, p_scr, c_scr, pad_scr, h_ref, carry_ref, reverse):
    steps = a.shape[0]
    pitch = pad_scr.shape[0] // SUBLANES
    order = range(steps - 1, -1, -1) if reverse else range(steps)
    h = None
    for s in order:
        h = bb[s] if h is None else a[s] * h + bb[s]
        prod = a[s] if s == order[0] else a[s] * prod
        h_scr[s] = h
        p_scr[s] = prod
    c = carry_ref[...]
    for sg in (range(SUBLANES - 1, -1, -1) if reverse else range(SUBLANES)):
        c_scr[sg:sg + 1, :] = c
        c = h[sg:sg + 1] + prod[sg:sg + 1] * c
    carry_ref[...] = c
    c_in = c_scr[...]
    for s in range(steps):
        pad_scr[pl.ds(s, SUBLANES, stride=pitch), :] = h_scr[s] + p_scr[s] * c_in
    for sg in range(SUBLANES):
        h_ref[sg * steps:(sg + 1) * steps, :] = pad_scr[sg * pitch:sg * pitch + steps, :].astype(h_ref.dtype)


def _lru_kernel(xf_ref, xf_prev_ref, xf_next_ref, xb_ref, xb_prev_ref, xb_next_ref, cw_ref, cb_ref, wg_ref, ba_ref,
                bx_ref, sp_ref, h0f_ref, h0b_ref, hf_ref, hb_ref, cf_ref, cbk_ref, a_scr, b_scr, c_scr, pad_scr):
    i = pl.program_id(2)
    nt = pl.num_programs(2)

    @pl.when(i == 0)
    def _():
        cf_ref[...] = h0f_ref[...]
        cbk_ref[...] = h0b_ref[...]

    a, bb = _lru_coeffs(xf_ref, xf_prev_ref, xf_next_ref, i == 0, i == nt - 1, cw_ref, cb_ref, wg_ref.at[0],
                        ba_ref.at[0], bx_ref.at[0], sp_ref.at[0], pad_scr)
    _lru_scan_tile(a, bb, a_scr, b_scr, c_scr, pad_scr, hf_ref, cf_ref, False)
    a, bb = _lru_coeffs(xb_ref, xb_prev_ref, xb_next_ref, i == nt - 1, i == 0, cw_ref, cb_ref, wg_ref.at[1],
                        ba_ref.at[1], bx_ref.at[1], sp_ref.at[1], pad_scr)
    _lru_scan_tile(a, bb, a_scr, b_scr, c_scr, pad_scr, hb_ref, cbk_ref, True)


def _lru(z, x_block, conv_w, conv_b, wg, ba, bx, sp, h0f, h0b):
    bsz, t, _ = z.shape
    width = conv_w.shape[1]
    hd = width // LRU_HEADS
    assert hd == LANES
    tm = _row_tile(t, LRU_ROWS)
    nt = t // tm
    gpt = tm // SUBLANES
    hpt = tm // HALO
    nhb = t // HALO
    pitch = (gpt + SUBLANES - 1) // SUBLANES * SUBLANES
    pitch += SUBLANES * (1 - (pitch // SUBLANES) % 2)
    col = lambda h: x_block * LRU_HEADS + h

    def tile(rev):
        return lambda b, h, i: (b, (nt - 1 - i) if rev else i, col(h))

    def prev(rev):
        return lambda b, h, i: (b, jnp.maximum(((nt - 1 - i) if rev else i) * hpt - 1, 0), col(h))

    def nxt(rev):
        return lambda b, h, i: (b, jnp.minimum((((nt - 1 - i) if rev else i) + 1) * hpt, nhb - 1), col(h))

    lane = lambda rows: pl.BlockSpec((rows, hd), lambda b, h, i: (0, h))
    dirs = pl.BlockSpec((2, 1, hd), lambda b, h, i: (0, 0, h))
    st_spec = pl.BlockSpec((None, 1, hd), lambda b, h, i: (b, 0, h))
    in_specs = []
    for rev in (False, True):
        in_specs += [pl.BlockSpec((None, tm, hd), tile(rev)),
                     pl.BlockSpec((None, HALO, hd), prev(rev)),
                     pl.BlockSpec((None, HALO, hd), nxt(rev))]
    in_specs += [lane(conv_w.shape[0]), lane(1),
                 pl.BlockSpec((2, None, hd, 2 * hd), lambda b, h, i: (0, h, 0, 0)),
                 dirs, dirs, dirs, st_spec, st_spec]
    return pl.pallas_call(
        _lru_kernel,
        grid=(bsz, LRU_HEADS, nt),
        in_specs=in_specs,
        out_specs=[
            pl.BlockSpec((None, tm, hd), lambda b, h, i: (b, i, h)),
            pl.BlockSpec((None, tm, hd), lambda b, h, i: (b, nt - 1 - i, h)),
            st_spec, st_spec,
        ],
        out_shape=[
            jax.ShapeDtypeStruct((bsz, t, width), _ACT),
            jax.ShapeDtypeStruct((bsz, t, width), _ACT),
            jax.ShapeDtypeStruct((bsz, 1, width), _F32),
            jax.ShapeDtypeStruct((bsz, 1, width), _F32),
        ],
        scratch_shapes=[pltpu.VMEM((gpt, SUBLANES, hd), _F32), pltpu.VMEM((gpt, SUBLANES, hd), _F32),
                        pltpu.VMEM((SUBLANES, hd), _F32), pltpu.VMEM((SUBLANES * pitch, hd), _F32)],
        compiler_params=_cparams("parallel", "parallel", "arbitrary"),
        name="lru_scan",
    )(z, z, z, z, z, z, conv_w, conv_b, wg, ba, bx, sp, h0f, h0b)


def _residual_ffn_prenorm(y, x_ref, gate_ref, a_ref, s_ref, wr_ref, xo_ref, h_ref, lg_ref):
    xn = x_ref[...] + gate_ref[...] * y
    xo_ref[...] = xn
    h = _norm_mod(xn, a_ref[...], s_ref[...]).astype(_MXU)
    h_ref[...] = h
    lg_ref[...] = _dot(h, wr_ref[...])


def _post_even_kernel(za_ref, of_ref, ob_ref, cw_ref, cb_ref, ng_ref, w_ref, x_ref, gate_ref, a_ref, s_ref, wr_ref,
                      xo_ref, h_ref, lg_ref, y_scr, *, row_len):
    tm = za_ref.shape[0]
    scw = cw_ref.shape[1]
    val = ng_ref.shape[1]
    dv = val // GLA_HEADS
    bg = za_ref[:, 0:scw].astype(_F32)
    cx = za_ref[:, scw:2 * scw].astype(_F32) * za_ref[:, 2 * scw:3 * scw].astype(_F32)
    r = lax.broadcasted_iota(jnp.int32, (tm, 1), 0) % row_len
    left = jnp.where(r == 0, 0.0, pltpu.roll(cx, 1, 0))
    right = jnp.where(r == row_len - 1, 0.0, pltpu.roll(cx, tm - 1, 0))
    conv = cb_ref[...] + cw_ref[0:1] * left + cw_ref[1:2] * cx + cw_ref[2:3] * right
    acc = _dot(bg * conv, w_ref[0:scw, :])
    for h in range(GLA_HEADS):
        lanes = slice(h * dv, (h + 1) * dv)
        o = of_ref[:, lanes].astype(_F32) + ob_ref[:, lanes].astype(_F32)
        o = o * lax.rsqrt(jnp.mean(o * o, axis=-1, keepdims=True) + NORM_EPS)
        gate = za_ref[:, 3 * scw + h * dv:3 * scw + (h + 1) * dv].astype(_F32)
        y_scr[:, lanes] = o * ng_ref[:, lanes] * _silu(gate)
    acc = acc + _dot(y_scr[...], w_ref[scw:scw + val, :])
    _residual_ffn_prenorm(acc, x_ref, gate_ref, a_ref, s_ref, wr_ref, xo_ref, h_ref, lg_ref)


def _post_odd_kernel(zc_ref, hf_ref, hb_ref, lng_ref, sgw_ref, sgb_ref, w_ref, x_ref, gate_ref, a_ref, s_ref, wr_ref,
                     xo_ref, h_ref, lg_ref, y_scr):
    tm = zc_ref.shape[0]
    width = hf_ref.shape[1]
    gdim = width // SG_GROUPS
    y_rec = (hf_ref[...].astype(_F32) + hb_ref[...].astype(_F32)) * _gelu(zc_ref[:, 0:width].astype(_F32))
    acc = _dot(y_rec, w_ref[0:width, :])
    gv = _gelu(zc_ref[:, 2 * width:3 * width].astype(_F32))
    gc = gv - jnp.mean(gv, axis=-1, keepdims=True)
    vb = gc * lax.rsqrt(jnp.mean(gc * gc, axis=-1, keepdims=True) + NORM_EPS) * lng_ref[...]
    for n in range(tm // SG_CHUNK):
        rows = slice(n * SG_CHUNK, (n + 1) * SG_CHUNK)
        for g in range(SG_GROUPS):
            lanes = slice(g * gdim, (g + 1) * gdim)
            mixed = _dot(sgw_ref[g], vb[rows, lanes]) + sgb_ref[g]
            u = zc_ref[rows, width + g * gdim:width + (g + 1) * gdim].astype(_F32)
            y_scr[rows, lanes] = _gelu(u) * mixed
    acc = acc + _dot(y_scr[...], w_ref[width:2 * width, :])
    _residual_ffn_prenorm(acc, x_ref, gate_ref, a_ref, s_ref, wr_ref, xo_ref, h_ref, lg_ref)


def _post(body, z, z_width, seq_in, consts, w_out, x, gate, a, s, wr):
    bsz, t, d = x.shape
    tm = _row_tile(t, PROJ_ROWS)
    row = lambda b, i: (b, i, 0)
    vec = pl.BlockSpec((None, 1, d), lambda b, i: (b, 0, 0))
    full = lambda arr: pl.BlockSpec(arr.shape, lambda b, i: (0,) * arr.ndim)
    width = seq_in[0].shape[-1]
    in_specs = [pl.BlockSpec((None, tm, z_width), row)]
    in_specs += [pl.BlockSpec((None, tm, width), row) for _ in seq_in]
    in_specs += [full(v) for v in consts] + [full(w_out), pl.BlockSpec((None, tm, d), row), vec, vec, vec, full(wr)]
    return pl.pallas_call(
        body,
        grid=(bsz, t // tm),
        in_specs=in_specs,
        out_specs=[
            pl.BlockSpec((None, tm, d), row),
            pl.BlockSpec((None, tm, d), row),
            pl.BlockSpec((None, tm, LANES), row),
        ],
        out_shape=[
            jax.ShapeDtypeStruct((bsz, t, d), _F32),
            jax.ShapeDtypeStruct((bsz, t, d), _MXU),
            jax.ShapeDtypeStruct((bsz, t, LANES), _F32),
        ],
        scratch_shapes=[pltpu.VMEM((tm, width), _F32)],
        compiler_params=_cparams("parallel", "parallel"),
        name="post_proj",
    )(z, *seq_in, *consts, w_out, x, gate, a, s, wr)


def _moe_ffn_kernel(x_ref, g_ref, wg_ref, wu_ref, wd_ref, o_ref, wg_scr, wu_scr, wd_scr, *, gather):
    p = pl.program_id(0)
    b = pl.program_id(1)
    nb = pl.num_programs(1)
    fill = p % 2
    d, f = wg_ref.shape

    def cast_slice():
        rd = pl.ds(pl.multiple_of(b * (d // nb), d // nb), d // nb)
        rf = pl.ds(pl.multiple_of(b * (f // nb), f // nb), f // nb)
        wg_scr[fill, rd, :] = wg_ref[rd, :].astype(_MXU)
        wu_scr[fill, rd, :] = wu_ref[rd, :].astype(_MXU)
        wd_scr[fill, rf, :] = wd_ref[rf, :].astype(_MXU)

    @pl.when(p == 0)
    def _():
        cast_slice()
        o_ref[...] = jnp.zeros_like(o_ref)

    @pl.when(p > 0)
    def _():
        cast_slice()
        cur = 1 - fill
        bb, cap = g_ref.shape[0], g_ref.shape[1]
        if gather:
            t = x_ref.shape[1]
            lane_tok = lax.broadcasted_iota(jnp.int32, (cap, t), 1)
            rows = [_dot(jnp.where(lane_tok == g_ref[j, :, 1:2].astype(jnp.int32), 1.0, 0.0), x_ref[j])
                    for j in range(bb)]
            x = jnp.concatenate(rows, axis=0)
        else:
            x = x_ref[...].reshape(bb * cap, x_ref.shape[-1])
        gate = _dot(x, wg_scr[cur])
        up = _dot(x, wu_scr[cur])
        out = _dot(_silu(gate) * up, wd_scr[cur]) * g_ref[:, :, 0:1].reshape(bb * cap, 1)
        o_ref[...] = out.reshape(o_ref.shape).astype(o_ref.dtype)


def _moe_ffn(xs, g, wg, wu, wd, layer, gather):
    bsz, ne, cap, _ = g.shape
    d, f = wg.shape[-2:]
    bb = max(1, min(bsz, MOE_ROWS // cap))
    assert bsz % bb == 0
    nb = bsz // bb
    assert d % (nb * HALO) == 0 and f % (nb * HALO) == 0
    tok = lambda p, b: (jnp.where(p == 0, 0, b), jnp.maximum(p - 1, 0), 0, 0)
    wsel = lambda p, b: (layer, jnp.minimum(p, ne - 1), 0, 0)
    if gather:
        x_spec = pl.BlockSpec((bb,) + xs.shape[1:], lambda p, b: (b, 0, 0))
    else:
        x_spec = pl.BlockSpec((bb, None, cap, d), tok)
    return pl.pallas_call(
        functools.partial(_moe_ffn_kernel, gather=gather),
        grid=(ne + 1, nb),
        in_specs=[
            x_spec,
            pl.BlockSpec((bb, None, cap, 2), tok),
            pl.BlockSpec((None, None, d, f), wsel),
            pl.BlockSpec((None, None, d, f), wsel),
            pl.BlockSpec((None, None, f, d), wsel),
        ],
        out_specs=pl.BlockSpec((bb, None, cap, d), tok),
        out_shape=jax.ShapeDtypeStruct((bsz, ne, cap, d), _MXU),
        scratch_shapes=[pltpu.VMEM((2, d, f), _MXU), pltpu.VMEM((2, d, f), _MXU), pltpu.VMEM((2, f, d), _MXU)],
        compiler_params=pltpu.CompilerParams(dimension_semantics=("arbitrary", "arbitrary"),
                                             vmem_limit_bytes=MOE_VMEM_LIMIT),
        name="moe_ffn",
    )(xs, g, wg, wu, wd)


def _combine_kernel(lo_ref, hi_ref, tok_ref, y_ref, x_ref, gate_ref, fg_ref, o_ref, acc_ref, *, final_norm):
    b = pl.program_id(0)
    j = pl.program_id(1)
    tm = acc_ref.shape[0]
    ch = tok_ref.shape[-1]
    acc_ref[...] = jnp.zeros_like(acc_ref)
    rows = j * tm + lax.broadcasted_iota(jnp.int32, (tm, ch), 0)

    def body(c, carry):
        onehot = jnp.where(rows == tok_ref[c], 1.0, 0.0)
        acc_ref[...] += _dot(onehot, y_ref[pl.ds(pl.multiple_of(c * ch, ch), ch), :])
        return carry

    lax.fori_loop(lo_ref[b, j], hi_ref[b, j] + 1, body, 0)
    xn = x_ref[...] + gate_ref[...] * acc_ref[...]
    if final_norm:
        ms = jnp.mean(xn * xn, axis=-1, keepdims=True)
        xn = xn * lax.rsqrt(ms + NORM_EPS) * fg_ref[...]
    o_ref[...] = xn


def _combine(ys, toks, x, gate, final_g):
    bsz, t, d = x.shape
    p = ys.shape[1]
    tm = _row_tile(t, PROJ_ROWS)
    ch = _row_tile(p)
    nt, nc = t // tm, p // ch
    bounds = jnp.arange(nt + 1, dtype=jnp.int32) * tm
    cut = jnp.sum((toks[:, :, None] < bounds[None, None, :]).astype(jnp.int32), axis=1)
    c_lo = jnp.minimum(cut[:, :-1] // ch, nc - 1)
    c_hi = jnp.maximum((cut[:, 1:] - 1) // ch, c_lo)
    fg = jnp.ones((1, d), _F32) if final_g is None else final_g.reshape(1, d)
    grid_spec = pltpu.PrefetchScalarGridSpec(
        num_scalar_prefetch=2,
        grid=(bsz, nt),
        in_specs=[
            pl.BlockSpec((None, nc, 1, ch), lambda b, j, lo, hi: (b, 0, 0, 0)),
            pl.BlockSpec((None, p, d), lambda b, j, lo, hi: (b, 0, 0)),
            pl.BlockSpec((None, tm, d), lambda b, j, lo, hi: (b, j, 0)),
            pl.BlockSpec((None, 1, d), lambda b, j, lo, hi: (b, 0, 0)),
            pl.BlockSpec((1, d), lambda b, j, lo, hi: (0, 0)),
        ],
        out_specs=pl.BlockSpec((None, tm, d), lambda b, j, lo, hi: (b, j, 0)),
        scratch_shapes=[pltpu.VMEM((tm, d), _F32)],
    )
    return pl.pallas_call(
        functools.partial(_combine_kernel, final_norm=final_g is not None),
        grid_spec=grid_spec,
        out_shape=jax.ShapeDtypeStruct((bsz, t, d), _F32),
        compiler_params=pltpu.CompilerParams(dimension_semantics=("parallel", "arbitrary"),
                                             vmem_limit_bytes=MOE_VMEM_LIMIT),
        name="moe_combine",
    )(c_lo, c_hi, toks.reshape(bsz, nc, 1, ch), ys, x, gate, fg)


def _ec_moe(x, gate, h, logits, wg, wu, wd, layer, final_g):
    bsz, t, d = h.shape
    ne = logits.shape[-1]
    cap = EC_CAPACITY * t // ne
    aff = jax.nn.softmax(logits, axis=-1)
    keys = jnp.transpose(-aff, (1, 0, 2)).reshape(t, bsz * ne)
    token = lax.broadcasted_iota(jnp.int32, (t, bsz * ne), 0)
    keys, token = lax.sort((keys, token), dimension=0, num_keys=1, is_stable=True)
    g = jnp.transpose((-keys[:cap]).reshape(cap, bsz, ne), (1, 2, 0))
    idx = jnp.transpose(token[:cap].reshape(cap, bsz, ne), (1, 2, 0))
    gather = t <= SHORT_SEQ
    xs = h if gather else jax.vmap(lambda hb, ib: hb[ib])(h, idx)
    g_tok = jnp.stack([g, idx.astype(_F32)], axis=-1)
    y = _moe_ffn(xs, g_tok, wg, wu, wd, layer, gather).reshape(bsz, ne * cap, d)
    slots = jnp.broadcast_to(jnp.arange(ne * cap, dtype=jnp.int32), (bsz, ne * cap))
    toks, perm = lax.sort_key_val(idx.reshape(bsz, ne * cap).astype(jnp.int32), slots, dimension=-1)
    ys = jax.vmap(lambda yb, pb: yb[pb])(y, perm)
    return _combine(ys, toks, x, gate, final_g)


def _pad_to(w, axis, size):
    pad = [(0, 0)] * w.ndim
    pad[axis] = (0, size - w.shape[axis])
    return jnp.pad(w, pad)


def kernel(x, c, ctx, c_ctx, ada_w, ada_b, norm_mix_g, norm_ffn_g, norm_final_g, ev_w_in, ev_conv_w, ev_conv_b, ev_decay_up, ev_decay_b, ev_norm_g, ev_w_out, od_w_in, od_conv_w, od_conv_b, od_gate_a_w, od_gate_a_b, od_gate_x_w, od_gate_x_b, od_lambda, od_sg_norm_g, od_sg_w, od_sg_b, od_w_out, moe_router, moe_w_gate, moe_w_up, moe_w_down):
    depth = ada_w.shape[0]
    bsz, seq, d = x.shape
    ne = moe_router.shape[-1]
    x_lat, x_ctx = x, ctx
    cond = _pad_to(jnp.concatenate([c, c_ctx[None, :]], axis=0), 0, -(-(bsz + 1) // SUBLANES) * SUBLANES)
    ada = _adaln(cond, ada_w, ada_b.reshape(depth, 1, N_MOD * d))

    def vecs(m):
        return [v.reshape(bsz, 1, d) for v in m]

    for l in range(depth):
        last = l == depth - 1
        even = l % 2 == 0
        i = l // 2
        mod = [ada[l, :bsz, k * d:(k + 1) * d] for k in range(N_MOD)]
        mod_c = [jnp.broadcast_to(ada[l, bsz:bsz + 1, k * d:(k + 1) * d], (bsz, d)) for k in range(N_MOD)]
        wr = _pad_to(moe_router[l], 1, LANES).astype(_MXU)

        if even:
            w = ev_w_in[i]
            scw = ev_conv_w.shape[-1]
            key = ev_decay_up.shape[-1]
            val = ev_norm_g.shape[-1]
            o_q = 3 * scw
            o_g = o_q + key
            o_k = o_g + val
            o_v = o_k + key
            o_lr = o_v + val
            w_in = jnp.concatenate([w[:, :o_q], w[:, o_g:o_k], w[:, o_q:o_g], w[:, o_k:o_lr]], axis=1).astype(_MXU)
            w_lr = _pad_to(w[:, o_lr:], 1, LANES).astype(_MXU)
            up = jnp.zeros((LANES, 2 * key), _F32)
            up = up.at[:GLA_RANK, :key].set(ev_decay_up[i, 0]).at[GLA_RANK:2 * GLA_RANK, key:].set(ev_decay_up[i, 1])
            decay = (w_lr, up.astype(_MXU), ev_decay_b[i].reshape(1, 2 * key))
            w_out = ev_w_out[i].astype(_MXU)
            conv_w = _pad_to(ev_conv_w[i], 0, SUBLANES)
            consts = (conv_w, ev_conv_b[i].reshape(1, scw), ev_norm_g[i].reshape(1, val))
            epi_width = 3 * scw + val
            assert (3 * scw + val) % (2 * key + val) == 0
            qkv_block = epi_width // (2 * key + val)
            dk, dv = key // GLA_HEADS, val // GLA_HEADS
            scale = dk ** -0.5
            zero_st = jnp.zeros((bsz, GLA_HEADS // 2, dv, 2 * dk), _F32)

            def mixer(xs, m, s0, row_len):
                sh, sc = vecs(m[:2])
                z, la = _pre(xs, norm_mix_g[l] * (1.0 + sc), sh, w_in, decay)
                o_f, o_b, s_f, s_b = _gla(z, la, s0[0], s0[1], qkv_block, scale)
                body = functools.partial(_post_even_kernel, row_len=row_len)
                return (body, z, epi_width, (o_f, o_b), consts, w_out), (s_f, s_b)

            ctx_args, state = mixer(x_ctx, mod_c, (zero_st, zero_st), x_ctx.shape[1])
            lat_args, _ = mixer(x_lat, mod, state, GRID_W)
        else:
            w = od_w_in[i]
            width = od_conv_w.shape[-1]
            hd = width // LRU_HEADS
            w_in = jnp.concatenate([w[:, width:], w[:, :width]], axis=1).astype(_MXU)
            w_out = od_w_out[i].astype(_MXU)
            conv_w = _pad_to(od_conv_w[i], 0, SUBLANES)
            conv_b = od_conv_b[i].reshape(1, width)
            wgate = jnp.concatenate([od_gate_a_w[i], od_gate_x_w[i]], axis=-1).astype(_MXU)
            ba = od_gate_a_b[i].reshape(2, 1, width)
            bx = od_gate_x_b[i].reshape(2, 1, width)
            sp = (-LRU_C * jax.nn.softplus(-od_lambda[i])).reshape(2, 1, width)
            sgb = jnp.broadcast_to(od_sg_b[i][:, :, None], (SG_GROUPS, SG_CHUNK, width // SG_GROUPS))
            consts = (od_sg_norm_g[i].reshape(1, width), od_sg_w[i].astype(_MXU), sgb)
            zero_st = jnp.zeros((bsz, 1, width), _F32)

            def mixer(xs, m, s0, row_len):
                sh, sc = vecs(m[:2])
                (z,) = _pre(xs, norm_mix_g[l] * (1.0 + sc), sh, w_in)
                h_f, h_b, c_f, c_b = _lru(z, 3, conv_w, conv_b, wgate, ba, bx, sp, s0[0], s0[1])
                return (_post_odd_kernel, z, 3 * width, (h_f, h_b), consts, w_out), (c_f, c_b)

            ctx_args, state = mixer(x_ctx, mod_c, (zero_st, zero_st), None)
            lat_args, _ = mixer(x_lat, mod, state, None)

        def post(args, xs, m, final_g):
            gate1, sh, sc, gate2 = vecs(m[2:])
            xs, h, logits = _post(*args, xs, gate1, norm_ffn_g[l] * (1.0 + sc), sh, wr)
            return _ec_moe(xs, gate2, h, logits[..., :ne], moe_w_gate, moe_w_up, moe_w_down, l, final_g)

        x_lat = post(lat_args, x_lat, mod, norm_final_g if last else None)
        if not last:
            x_ctx = post(ctx_args, x_ctx, mod_c, None)
    return x_lat
```

```python
import functools

import jax
import jax.numpy as jnp
from jax import lax
from jax.experimental import pallas as pl
from jax.experimental.pallas import tpu as pltpu

_MXU = jnp.bfloat16
_ACT = jnp.bfloat16
_F32 = jnp.float32
NORM_EPS = 1e-6
N_MOD = 6
GRID_W = 64
GLA_HEADS = 4
GLA_RANK = 16
GLA_TAU = 16.0
GLA_CHUNK = 64
LRU_HEADS = 4
LRU_C = 8.0
SG_GROUPS = 4
SG_CHUNK = 128
EC_CAPACITY = 2
LANES = 128
SUBLANES = 8
HALO = 16
VMEM_LIMIT = 48 * 1024 * 1024
MOE_VMEM_LIMIT = 56 * 1024 * 1024
MOE_ROWS = 512
SHORT_SEQ = 256
PROJ_ROWS = 512
LRU_ROWS = 1024


def _cparams(*sem):
    return pltpu.CompilerParams(dimension_semantics=sem, vmem_limit_bytes=VMEM_LIMIT)


def _row_tile(t, cap=256):
    tm = min(t, cap)
    assert t % tm == 0
    return tm


def _dot(a, b):
    return jnp.dot(a.astype(_MXU), b.astype(_MXU), preferred_element_type=_F32)


def _dot_nt(a, b):
    return lax.dot_general(a.astype(_MXU), b.astype(_MXU), (((1,), (1,)), ((), ())), preferred_element_type=_F32)


def _dot_tn(a, b):
    return lax.dot_general(a.astype(_MXU), b.astype(_MXU), (((0,), (0,)), ((), ())), preferred_element_type=_F32)


def _gelu(x):
    return 0.5 * x * (1.0 + jnp.tanh(0.7978845608028654 * (x + 0.044715 * (x * x * x))))


def _sigmoid(x):
    return 0.5 * jnp.tanh(0.5 * x) + 0.5


def _silu(x):
    return x * _sigmoid(x)


def _log_sigmoid(x):
    return jnp.minimum(x, 0.0) - jnp.log(1.0 + jnp.exp(-jnp.abs(x)))


def _norm_mod(x, a, s):
    ms = jnp.mean(x * x, axis=-1, keepdims=True)
    return x * lax.rsqrt(ms + NORM_EPS) * a + s


def _adaln_kernel(c_ref, w_ref, b_ref, o_ref):
    o_ref[...] = _dot(_silu(c_ref[...]), w_ref[...]) + b_ref[...]


def _adaln(cond, w, b):
    depth, d, n = w.shape
    r = cond.shape[0]
    tn = n // N_MOD
    return pl.pallas_call(
        _adaln_kernel,
        grid=(depth, N_MOD),
        in_specs=[
            pl.BlockSpec((r, d), lambda l, j: (0, 0)),
            pl.BlockSpec((None, d, tn), lambda l, j: (l, 0, j)),
            pl.BlockSpec((None, 1, tn), lambda l, j: (l, 0, j)),
        ],
        out_specs=pl.BlockSpec((None, r, tn), lambda l, j: (l, 0, j)),
        out_shape=jax.ShapeDtypeStruct((depth, r, n), _F32),
        compiler_params=_cparams("parallel", "parallel"),
        name="adaln",
    )(cond, w, b)


def _pre_kernel(x_ref, a_ref, s_ref, w_ref, o_ref):
    h = _norm_mod(x_ref[...], a_ref[...], s_ref[...])
    o_ref[...] = _dot(h, w_ref[...]).astype(o_ref.dtype)


def _pre_even_kernel(x_ref, a_ref, s_ref, w_ref, wlr_ref, up_ref, db_ref, o_ref, la_ref):
    h = _norm_mod(x_ref[...], a_ref[...], s_ref[...]).astype(_MXU)
    o_ref[...] = _dot(h, w_ref[...]).astype(o_ref.dtype)
    lr = _dot(h, wlr_ref[...])
    la_ref[...] = _log_sigmoid(_dot(lr, up_ref[...]) + db_ref[...]) * (1.0 / GLA_TAU)


def _pre(x, a, s, w, decay=None):
    bsz, t, d = x.shape
    n = w.shape[1]
    tm = _row_tile(t, PROJ_ROWS)
    row = lambda b, i: (b, i, 0)
    vec = pl.BlockSpec((None, 1, d), lambda b, i: (b, 0, 0))
    full = lambda arr: pl.BlockSpec(arr.shape, lambda b, i: (0,) * arr.ndim)
    in_specs = [pl.BlockSpec((None, tm, d), row), vec, vec, full(w)]
    out_specs = [pl.BlockSpec((None, tm, n), row)]
    out_shape = [jax.ShapeDtypeStruct((bsz, t, n), _ACT)]
    args = [x, a, s, w]
    body = _pre_kernel
    if decay is not None:
        body = _pre_even_kernel
        n_la = decay[1].shape[1]
        in_specs += [full(v) for v in decay]
        out_specs.append(pl.BlockSpec((None, tm, n_la), row))
        out_shape.append(jax.ShapeDtypeStruct((bsz, t, n_la), _F32))
        args += list(decay)
    return pl.pallas_call(
        body, grid=(bsz, t // tm), in_specs=in_specs, out_specs=out_specs, out_shape=out_shape,
        compiler_params=_cparams("parallel", "parallel"), name="pre_proj",
    )(*args)


def _bmm(a, b, dims):
    return lax.dot_general(a.astype(_MXU), b.astype(_MXU), (dims, ((0,), (0,))), preferred_element_type=_F32)


def _cumsum_chunks(tri, x):
    hi = x.astype(_MXU)
    r1 = x - hi.astype(_F32)
    mid = r1.astype(_MXU)
    lo = (r1 - mid.astype(_F32)).astype(_MXU)
    mm = ((2,), (1,))
    return _bmm(tri, hi, mm) + _bmm(tri, mid, mm) + _bmm(tri, lo, mm)


def _gla_block(qkv_ref, la_ref, o_ref, st_ref, sin_ref, keep, scale, reverse):
    c = GLA_CHUNK
    tb = qkv_ref.shape[0]
    n = tb // c
    key = la_ref.shape[-1]
    dv = (qkv_ref.shape[-1] - 2 * key) // GLA_HEADS
    dk = key // GLA_HEADS
    tri = jnp.broadcast_to(keep.astype(_MXU)[None], (n, c, c))
    b = _cumsum_chunks(tri, la_ref[...].reshape(n, c, key))
    btot = b[:, 0:1] if reverse else b[:, c - 1:c]
    q = qkv_ref[:, 0:key].astype(_F32).reshape(n, c, key)
    k = qkv_ref[:, key:2 * key].astype(_F32).reshape(n, c, key)
    qd = q * (scale * jnp.exp(b))
    kd = k * jnp.exp(-b)
    kr = k * jnp.exp(btot - b)
    dec = jnp.exp(btot)
    pair = 2 * dk
    lane_lo = lax.broadcasted_iota(jnp.int32, (n, c, pair), 2) < dk
    order = range(n - 1, -1, -1) if reverse else range(n)
    for p in range(GLA_HEADS // 2):
        lanes = slice(p * pair, (p + 1) * pair)
        qd_p, kd_p, kr_p = qd[:, :, lanes], kd[:, :, lanes], kr[:, :, lanes]
        vs = [qkv_ref[:, 2 * key + h * dv:2 * key + (h + 1) * dv].reshape(n, c, dv) for h in (2 * p, 2 * p + 1)]
        km = jnp.concatenate([jnp.where(lane_lo, kr_p, 0.0), jnp.where(lane_lo, 0.0, kr_p)], axis=1)
        upd = _bmm(jnp.concatenate(vs, axis=1), km, ((1,), (1,)))
        st = st_ref[p]
        for j in order:
            sin_ref[j] = st
            st = dec[j, :, lanes] * st + upd[j]
        st_ref[p] = st
        s_in = sin_ref[...]
        for s in range(2):
            h = 2 * p + s
            qm = jnp.where(lane_lo if s == 0 else jnp.logical_not(lane_lo), qd_p, 0.0)
            att = jnp.where(keep[None], _bmm(qm, kd_p, ((2,), (2,))), 0.0)
            o = _bmm(att, vs[s], ((2,), (1,))) + _bmm(qm, s_in, ((2,), (2,)))
            o_ref[:, h * dv:(h + 1) * dv] = o.reshape(tb, dv).astype(o_ref.dtype)


def _gla_kernel(qkv_f_ref, la_f_ref, qkv_b_ref, la_b_ref, s0f_ref, s0b_ref, of_ref, ob_ref, sf_ref, sb_ref, sin_ref, *,
                scale):
    i = pl.program_id(1)

    @pl.when(i == 0)
    def _():
        sf_ref[...] = s0f_ref[...]
        sb_ref[...] = s0b_ref[...]

    c = GLA_CHUNK
    ri = lax.broadcasted_iota(jnp.int32, (c, c), 0)
    ci = lax.broadcasted_iota(jnp.int32, (c, c), 1)
    _gla_block(qkv_f_ref, la_f_ref, of_ref, sf_ref, sin_ref, ri >= ci, scale, False)
    _gla_block(qkv_b_ref, la_b_ref, ob_ref, sb_ref, sin_ref, ri <= ci, scale, True)


def _gla(z, la, s0f, s0b, qkv_block, scale):
    bsz, t, _ = z.shape
    key = la.shape[-1] // 2
    st_shape = s0f.shape[1:]
    val = GLA_HEADS * st_shape[1]
    width = 2 * key + val
    tb = _row_tile(t, 512)
    nt = t // tb
    st_spec = pl.BlockSpec((None,) + st_shape, lambda b, i: (b, 0, 0, 0))
    outs = pl.pallas_call(
        functools.partial(_gla_kernel, scale=scale),
        grid=(bsz, nt),
        in_specs=[
            pl.BlockSpec((None, tb, width), lambda b, i: (b, i, qkv_block)),
            pl.BlockSpec((None, tb, key), lambda b, i: (b, i, 0)),
            pl.BlockSpec((None, tb, width), lambda b, i: (b, nt - 1 - i, qkv_block)),
            pl.BlockSpec((None, tb, key), lambda b, i: (b, nt - 1 - i, 1)),
            st_spec, st_spec,
        ],
        out_specs=[
            pl.BlockSpec((None, tb, val), lambda b, i: (b, i, 0)),
            pl.BlockSpec((None, tb, val), lambda b, i: (b, nt - 1 - i, 0)),
            st_spec, st_spec,
        ],
        out_shape=[
            jax.ShapeDtypeStruct((bsz, t, val), _ACT),
            jax.ShapeDtypeStruct((bsz, t, val), _ACT),
            jax.ShapeDtypeStruct(s0f.shape, _F32),
            jax.ShapeDtypeStruct(s0b.shape, _F32),
        ],
        scratch_shapes=[pltpu.VMEM((tb // GLA_CHUNK,) + st_shape[1:], _F32)],
        compiler_params=_cparams("parallel", "arbitrary"),
        name="gla_scan",
    )(z, la, z, la, s0f, s0b)
    return outs


def _lru_coeffs(x_ref, prev_ref, next_ref, first, last, cw_ref, cb_ref, wg_ref, ba_ref, bx_ref, sp_ref, pad_scr):
    tm, hd = x_ref.shape
    steps = tm // SUBLANES
    pitch = pad_scr.shape[0] // SUBLANES
    for sg in range(SUBLANES):
        pad_scr[sg * pitch:sg * pitch + steps, :] = x_ref[sg * steps:(sg + 1) * steps, :].astype(_F32)
    x = jnp.stack([pad_scr[pl.ds(s, SUBLANES, stride=pitch), :] for s in range(steps)], axis=0)
    prev = jnp.where(first, 0.0, prev_ref[...].astype(_F32)[HALO - SUBLANES:HALO])
    nxt = jnp.where(last, 0.0, next_ref[...].astype(_F32)[0:SUBLANES])
    seg = lax.broadcasted_iota(jnp.int32, (SUBLANES, 1), 0)

    def from_prev_segment(v, fill):
        return jnp.where(seg == 0, fill, pltpu.roll(v, 1, 0))

    def from_next_segment(v, fill):
        return jnp.where(seg == SUBLANES - 1, fill, pltpu.roll(v, SUBLANES - 1, 0))

    m1_edge = from_prev_segment(x[steps - 1], prev[SUBLANES - 1:SUBLANES])
    m2_edge = from_prev_segment(x[steps - 2], prev[SUBLANES - 2:SUBLANES - 1])
    p1_edge = from_next_segment(x[0], nxt[0:1])
    x_m1 = jnp.concatenate([m1_edge[None], x[:steps - 1]], axis=0)
    x_m2 = jnp.concatenate([m2_edge[None], m1_edge[None], x[:steps - 2]], axis=0)
    x_p1 = jnp.concatenate([x[1:], p1_edge[None]], axis=0)
    xc = cb_ref[...] + cw_ref[0:1] * x_m2 + cw_ref[1:2] * x_m1 + cw_ref[2:3] * x + cw_ref[3:4] * x_p1
    xc = xc.reshape(tm, hd)
    g = _dot(xc, wg_ref[...])
    rg = _sigmoid(g[:, :hd] + ba_ref[...])
    ig = _sigmoid(g[:, hd:] + bx_ref[...])
    log_a = rg * sp_ref[...]
    th = jnp.tanh(0.5 * log_a)
    rcp = 1.0 / (1.0 - th)
    a = (1.0 + th) * rcp
    bb = (2.0 * jnp.sqrt(-th) * rcp) * (ig * xc)
    shape = (steps, SUBLANES, hd)
    return a.reshape(shape), bb.reshape(shape)


def _lru_scan_tile(a, bb, h_scr, p_scr, c_scr, pad_scr, h_ref, carry_ref, reverse):
    steps = a.shape[0]
    pitch = pad_scr.shape[0] // SUBLANES
    order = range(steps - 1, -1, -1) if reverse else range(steps)
    h = None
    for s in order:
        h = bb[s] if h is None else a[s] * h + bb[s]
        prod = a[s] if s == order[0] else a[s] * prod
        h_scr[s] = h
        p_scr[s] = prod
    c = carry_ref[...]
    for sg in (range(SUBLANES - 1, -1, -1) if reverse else range(SUBLANES)):
        c_scr[sg:sg + 1, :] = c
        c = h[sg:sg + 1] + prod[sg:sg + 1] * c
    carry_ref[...] = c
    c_in = c_scr[...]
    for s in range(steps):
        pad_scr[pl.ds(s, SUBLANES, stride=pitch), :] = h_scr[s] + p_scr[s] * c_in
    for sg in range(SUBLANES):
        h_ref[sg * steps:(sg + 1) * steps, :] = pad_scr[sg * pitch:sg * pitch + steps, :].astype(h_ref.dtype)


def _lru_kernel(xf_ref, xf_prev_ref, xf_next_ref, xb_ref, xb_prev_ref, xb_next_ref, cw_ref, cb_ref, wg_ref, ba_ref,
                bx_ref, sp_ref, h0f_ref, h0b_ref, hf_ref, hb_ref, cf_ref, cbk_ref, a_scr, b_scr, c_scr, pad_scr):
    i = pl.program_id(2)
    nt = pl.num_programs(2)

    @pl.when(i == 0)
    def _():
        cf_ref[...] = h0f_ref[...]
        cbk_ref[...] = h0b_ref[...]

    a, bb = _lru_coeffs(xf_ref, xf_prev_ref, xf_next_ref, i == 0, i == nt - 1, cw_ref, cb_ref, wg_ref.at[0],
                        ba_ref.at[0], bx_ref.at[0], sp_ref.at[0], pad_scr)
    _lru_scan_tile(a, bb, a_scr, b_scr, c_scr, pad_scr, hf_ref, cf_ref, False)
    a, bb = _lru_coeffs(xb_ref, xb_prev_ref, xb_next_ref, i == nt - 1, i == 0, cw_ref, cb_ref, wg_ref.at[1],
                        ba_ref.at[1], bx_ref.at[1], sp_ref.at[1], pad_scr)
    _lru_scan_tile(a, bb, a_scr, b_scr, c_scr, pad_scr, hb_ref, cbk_ref, True)


def _lru(z, x_block, conv_w, conv_b, wg, ba, bx, sp, h0f, h0b):
    bsz, t, _ = z.shape
    width = conv_w.shape[1]
    hd = width // LRU_HEADS
    assert hd == LANES
    tm = _row_tile(t, LRU_ROWS)
    nt = t // tm
    gpt = tm // SUBLANES
    hpt = tm // HALO
    nhb = t // HALO
    pitch = (gpt + SUBLANES - 1) // SUBLANES * SUBLANES
    pitch += SUBLANES * (1 - (pitch // SUBLANES) % 2)
    col = lambda h: x_block * LRU_HEADS + h

    def tile(rev):
        return lambda b, h, i: (b, (nt - 1 - i) if rev else i, col(h))

    def prev(rev):
        return lambda b, h, i: (b, jnp.maximum(((nt - 1 - i) if rev else i) * hpt - 1, 0), col(h))

    def nxt(rev):
        return lambda b, h, i: (b, jnp.minimum((((nt - 1 - i) if rev else i) + 1) * hpt, nhb - 1), col(h))

    lane = lambda rows: pl.BlockSpec((rows, hd), lambda b, h, i: (0, h))
    dirs = pl.BlockSpec((2, 1, hd), lambda b, h, i: (0, 0, h))
    st_spec = pl.BlockSpec((None, 1, hd), lambda b, h, i: (b, 0, h))
    in_specs = []
    for rev in (False, True):
        in_specs += [pl.BlockSpec((None, tm, hd), tile(rev)),
                     pl.BlockSpec((None, HALO, hd), prev(rev)),
                     pl.BlockSpec((None, HALO, hd), nxt(rev))]
    in_specs += [lane(conv_w.shape[0]), lane(1),
                 pl.BlockSpec((2, None, hd, 2 * hd), lambda b, h, i: (0, h, 0, 0)),
                 dirs, dirs, dirs, st_spec, st_spec]
    return pl.pallas_call(
        _lru_kernel,
        grid=(bsz, LRU_HEADS, nt),
        in_specs=in_specs,
        out_specs=[
            pl.BlockSpec((None, tm, hd), lambda b, h, i: (b, i, h)),
            pl.BlockSpec((None, tm, hd), lambda b, h, i: (b, nt - 1 - i, h)),
            st_spec, st_spec,
        ],
        out_shape=[
            jax.ShapeDtypeStruct((bsz, t, width), _ACT),
            jax.ShapeDtypeStruct((bsz, t, width), _ACT),
            jax.ShapeDtypeStruct((bsz, 1, width), _F32),
            jax.ShapeDtypeStruct((bsz, 1, width), _F32),
        ],
        scratch_shapes=[pltpu.VMEM((gpt, SUBLANES, hd), _F32), pltpu.VMEM((gpt, SUBLANES, hd), _F32),
                        pltpu.VMEM((SUBLANES, hd), _F32), pltpu.VMEM((SUBLANES * pitch, hd), _F32)],
        compiler_params=_cparams("parallel", "parallel", "arbitrary"),
        name="lru_scan",
    )(z, z, z, z, z, z, conv_w, conv_b, wg, ba, bx, sp, h0f, h0b)


def _residual_ffn_prenorm(y, x_ref, gate_ref, a_ref, s_ref, wr_ref, xo_ref, h_ref, lg_ref):
    xn = x_ref[...] + gate_ref[...] * y
    xo_ref[...] = xn
    h = _norm_mod(xn, a_ref[...], s_ref[...]).astype(_MXU)
    h_ref[...] = h
    lg_ref[...] = _dot(h, wr_ref[...])


def _post_even_kernel(za_ref, of_ref, ob_ref, cw_ref, cb_ref, ng_ref, w_ref, x_ref, gate_ref, a_ref, s_ref, wr_ref,
                      xo_ref, h_ref, lg_ref, y_scr, *, row_len):
    tm = za_ref.shape[0]
    scw = cw_ref.shape[1]
    val = ng_ref.shape[1]
    dv = val // GLA_HEADS
    bg = za_ref[:, 0:scw].astype(_F32)
    cx = za_ref[:, scw:2 * scw].astype(_F32) * za_ref[:, 2 * scw:3 * scw].astype(_F32)
    r = lax.broadcasted_iota(jnp.int32, (tm, 1), 0) % row_len
    left = jnp.where(r == 0, 0.0, pltpu.roll(cx, 1, 0))
    right = jnp.where(r == row_len - 1, 0.0, pltpu.roll(cx, tm - 1, 0))
    conv = cb_ref[...] + cw_ref[0:1] * left + cw_ref[1:2] * cx + cw_ref[2:3] * right
    acc = _dot(bg * conv, w_ref[0:scw, :])
    for h in range(GLA_HEADS):
        lanes = slice(h * dv, (h + 1) * dv)
        o = of_ref[:, lanes].astype(_F32) + ob_ref[:, lanes].astype(_F32)
        o = o * lax.rsqrt(jnp.mean(o * o, axis=-1, keepdims=True) + NORM_EPS)
        gate = za_ref[:, 3 * scw + h * dv:3 * scw + (h + 1) * dv].astype(_F32)
        y_scr[:, lanes] = o * ng_ref[:, lanes] * _silu(gate)
    acc = acc + _dot(y_scr[...], w_ref[scw:scw + val, :])
    _residual_ffn_prenorm(acc, x_ref, gate_ref, a_ref, s_ref, wr_ref, xo_ref, h_ref, lg_ref)


def _post_odd_kernel(zc_ref, hf_ref, hb_ref, lng_ref, sgw_ref, sgb_ref, w_ref, x_ref, gate_ref, a_ref, s_ref, wr_ref,
                     xo_ref, h_ref, lg_ref, y_scr):
    tm = zc_ref.shape[0]
    width = hf_ref.shape[1]
    gdim = width // SG_GROUPS
    y_rec = (hf_ref[...].astype(_F32) + hb_ref[...].astype(_F32)) * _gelu(zc_ref[:, 0:width].astype(_F32))
    acc = _dot(y_rec, w_ref[0:width, :])
    gv = _gelu(zc_ref[:, 2 * width:3 * width].astype(_F32))
    gc = gv - jnp.mean(gv, axis=-1, keepdims=True)
    vb = gc * lax.rsqrt(jnp.mean(gc * gc, axis=-1, keepdims=True) + NORM_EPS) * lng_ref[...]
    for n in range(tm // SG_CHUNK):
        rows = slice(n * SG_CHUNK, (n + 1) * SG_CHUNK)
        for g in range(SG_GROUPS):
            lanes = slice(g * gdim, (g + 1) * gdim)
            mixed = _dot(sgw_ref[g], vb[rows, lanes]) + sgb_ref[g]
            u = zc_ref[rows, width + g * gdim:width + (g + 1) * gdim].astype(_F32)
            y_scr[rows, lanes] = _gelu(u) * mixed
    acc = acc + _dot(y_scr[...], w_ref[width:2 * width, :])
    _residual_ffn_prenorm(acc, x_ref, gate_ref, a_ref, s_ref, wr_ref, xo_ref, h_ref, lg_ref)


def _post(body, z, z_width, seq_in, consts, w_out, x, gate, a, s, wr):
    bsz, t, d = x.shape
    tm = _row_tile(t, PROJ_ROWS)
    row = lambda b, i: (b, i, 0)
    vec = pl.BlockSpec((None, 1, d), lambda b, i: (b, 0, 0))
    full = lambda arr: pl.BlockSpec(arr.shape, lambda b, i: (0,) * arr.ndim)
    width = seq_in[0].shape[-1]
    in_specs = [pl.BlockSpec((None, tm, z_width), row)]
    in_specs += [pl.BlockSpec((None, tm, width), row) for _ in seq_in]
    in_specs += [full(v) for v in consts] + [full(w_out), pl.BlockSpec((None, tm, d), row), vec, vec, vec, full(wr)]
    return pl.pallas_call(
        body,
        grid=(bsz, t // tm),
        in_specs=in_specs,
        out_specs=[
            pl.BlockSpec((None, tm, d), row),
            pl.BlockSpec((None, tm, d), row),
            pl.BlockSpec((None, tm, LANES), row),
        ],
        out_shape=[
            jax.ShapeDtypeStruct((bsz, t, d), _F32),
            jax.ShapeDtypeStruct((bsz, t, d), _MXU),
            jax.ShapeDtypeStruct((bsz, t, LANES), _F32),
        ],
        scratch_shapes=[pltpu.VMEM((tm, width), _F32)],
        compiler_params=_cparams("parallel", "parallel"),
        name="post_proj",
    )(z, *seq_in, *consts, w_out, x, gate, a, s, wr)


def _moe_ffn_kernel(*refs, with_short):
    if with_short:
        x_ref, g_ref, xs_ref, gs_ref, wg_ref, wu_ref, wd_ref, o_ref, os_ref, wg_scr, wu_scr, wd_scr = refs
    else:
        x_ref, g_ref, wg_ref, wu_ref, wd_ref, o_ref, wg_scr, wu_scr, wd_scr = refs
    b = pl.program_id(1)

    @pl.when(b == 0)
    def _():
        wg_scr[...] = wg_ref[...].astype(_MXU)
        wu_scr[...] = wu_ref[...].astype(_MXU)
        wd_scr[...] = wd_ref[...].astype(_MXU)

    def ffn(xr, gr, outr):
        bb, cap, d = xr.shape
        x = xr[...].reshape(bb * cap, d)
        gate = _dot(x, wg_scr[...])
        up = _dot(x, wu_scr[...])
        out = _dot(_silu(gate) * up, wd_scr[...]) * gr[...].reshape(bb * cap, 1)
        outr[...] = out.reshape(bb, cap, d).astype(outr.dtype)

    if with_short:
        last = pl.num_programs(1) - 1
        pl.when(b < last)(lambda: ffn(x_ref, g_ref, o_ref))
        pl.when(b == last)(lambda: ffn(xs_ref, gs_ref, os_ref))
    else:
        ffn(x_ref, g_ref, o_ref)


def _moe_ffn(xg, g, short, wg, wu, wd, layer):
    bsz, ne, cap, d = xg.shape
    f = wg.shape[-1]
    bb = max(1, min(bsz, MOE_ROWS // cap))
    assert bsz % bb == 0
    nb = bsz // bb
    wsel = lambda e, b: (layer, e, 0, 0)
    w_specs = [pl.BlockSpec((None, None, d, f), wsel), pl.BlockSpec((None, None, d, f), wsel),
               pl.BlockSpec((None, None, f, d), wsel)]
    out_shape = [jax.ShapeDtypeStruct((bsz, ne, cap, d), _MXU)]
    if short is None:
        tok = lambda e, b: (b, e, 0, 0)
        steps = nb
        in_specs = [pl.BlockSpec((bb, None, cap, d), tok), pl.BlockSpec((bb, None, cap, 1), tok)] + w_specs
        out_specs = [pl.BlockSpec((bb, None, cap, d), tok)]
        args = (xg, g, wg, wu, wd)
    else:
        xs, gs = short
        bs, _, caps, _ = xs.shape
        assert bs * caps <= MOE_ROWS
        tok = lambda e, b: (jnp.minimum(b, nb - 1), e, 0, 0)
        tok_s = lambda e, b: (0, e, 0, 0)
        steps = nb + 1
        in_specs = [pl.BlockSpec((bb, None, cap, d), tok), pl.BlockSpec((bb, None, cap, 1), tok),
                    pl.BlockSpec((bs, None, caps, d), tok_s), pl.BlockSpec((bs, None, caps, 1), tok_s)] + w_specs
        out_specs = [pl.BlockSpec((bb, None, cap, d), tok), pl.BlockSpec((bs, None, caps, d), tok_s)]
        out_shape.append(jax.ShapeDtypeStruct((bs, ne, caps, d), _MXU))
        args = (xg, g, xs, gs, wg, wu, wd)
    return pl.pallas_call(
        functools.partial(_moe_ffn_kernel, with_short=short is not None),
        grid=(ne, steps),
        in_specs=in_specs,
        out_specs=out_specs,
        out_shape=out_shape,
        scratch_shapes=[pltpu.VMEM((d, f), _MXU), pltpu.VMEM((d, f), _MXU), pltpu.VMEM((f, d), _MXU)],
        compiler_params=pltpu.CompilerParams(dimension_semantics=("parallel", "arbitrary"),
                                             vmem_limit_bytes=MOE_VMEM_LIMIT),
        name="moe_ffn",
    )(*args)


def _dispatch_kernel(idx_ref, h_ref, o_ref):
    p, t = idx_ref.shape[0], h_ref.shape[0]
    onehot = jnp.where(lax.broadcasted_iota(jnp.int32, (p, t), 1) == idx_ref[...], 1.0, 0.0)
    o_ref[...] = _dot(onehot, h_ref[...]).astype(o_ref.dtype)


def _dispatch(h, idx):
    bsz, t, d = h.shape
    p = idx.shape[1]
    return pl.pallas_call(
        _dispatch_kernel,
        grid=(bsz,),
        in_specs=[pl.BlockSpec((None, p, 1), lambda b: (b, 0, 0)), pl.BlockSpec((None, t, d), lambda b: (b, 0, 0))],
        out_specs=pl.BlockSpec((None, p, d), lambda b: (b, 0, 0)),
        out_shape=jax.ShapeDtypeStruct((bsz, p, d), h.dtype),
        compiler_params=_cparams("parallel"),
        name="moe_dispatch",
    )(idx.reshape(bsz, p, 1), h)


def _combine_kernel(lo_ref, hi_ref, tok_ref, y_ref, x_ref, gate_ref, fg_ref, o_ref, acc_ref, *, final_norm):
    b = pl.program_id(0)
    j = pl.program_id(1)
    tm = acc_ref.shape[0]
    ch = tok_ref.shape[-1]
    acc_ref[...] = jnp.zeros_like(acc_ref)
    rows = j * tm + lax.broadcasted_iota(jnp.int32, (tm, ch), 0)

    def body(c, carry):
        onehot = jnp.where(rows == tok_ref[c], 1.0, 0.0)
        acc_ref[...] += _dot(onehot, y_ref[pl.ds(pl.multiple_of(c * ch, ch), ch), :])
        return carry

    lax.fori_loop(lo_ref[b, j], hi_ref[b, j] + 1, body, 0)
    xn = x_ref[...] + gate_ref[...] * acc_ref[...]
    if final_norm:
        ms = jnp.mean(xn * xn, axis=-1, keepdims=True)
        xn = xn * lax.rsqrt(ms + NORM_EPS) * fg_ref[...]
    o_ref[...] = xn


def _combine(ys, toks, x, gate, final_g):
    bsz, t, d = x.shape
    p = ys.shape[1]
    tm = _row_tile(t, PROJ_ROWS)
    ch = _row_tile(p)
    nt, nc = t // tm, p // ch
    bounds = jnp.arange(nt + 1, dtype=jnp.int32) * tm
    cut = jnp.sum((toks[:, :, None] < bounds[None, None, :]).astype(jnp.int32), axis=1)
    c_lo = jnp.minimum(cut[:, :-1] // ch, nc - 1)
    c_hi = jnp.maximum((cut[:, 1:] - 1) // ch, c_lo)
    fg = jnp.ones((1, d), _F32) if final_g is None else final_g.reshape(1, d)
    grid_spec = pltpu.PrefetchScalarGridSpec(
        num_scalar_prefetch=2,
        grid=(bsz, nt),
        in_specs=[
            pl.BlockSpec((None, nc, 1, ch), lambda b, j, lo, hi: (b, 0, 0, 0)),
            pl.BlockSpec((None, p, d), lambda b, j, lo, hi: (b, 0, 0)),
            pl.BlockSpec((None, tm, d), lambda b, j, lo, hi: (b, j, 0)),
            pl.BlockSpec((None, 1, d), lambda b, j, lo, hi: (b, 0, 0)),
            pl.BlockSpec((1, d), lambda b, j, lo, hi: (0, 0)),
        ],
        out_specs=pl.BlockSpec((None, tm, d), lambda b, j, lo, hi: (b, j, 0)),
        scratch_shapes=[pltpu.VMEM((tm, d), _F32)],
    )
    return pl.pallas_call(
        functools.partial(_combine_kernel, final_norm=final_g is not None),
        grid_spec=grid_spec,
        out_shape=jax.ShapeDtypeStruct((bsz, t, d), _F32),
        compiler_params=pltpu.CompilerParams(dimension_semantics=("parallel", "arbitrary"),
                                             vmem_limit_bytes=MOE_VMEM_LIMIT),
        name="moe_combine",
    )(c_lo, c_hi, toks.reshape(bsz, nc, 1, ch), ys, x, gate, fg)


def _route(h, logits):
    bsz, t, d = h.shape
    ne = logits.shape[-1]
    cap = EC_CAPACITY * t // ne
    aff = jax.nn.softmax(logits, axis=-1)
    keys = jnp.transpose(-aff, (1, 0, 2)).reshape(t, bsz * ne)
    token = lax.broadcasted_iota(jnp.int32, (t, bsz * ne), 0)
    keys, token = lax.sort((keys, token), dimension=0, num_keys=1, is_stable=True)
    g = jnp.transpose((-keys[:cap]).reshape(cap, bsz, ne), (1, 2, 0))
    idx = jnp.transpose(token[:cap].reshape(cap, bsz, ne), (1, 2, 0))
    if t <= SHORT_SEQ:
        xg = _dispatch(h, idx.reshape(bsz, ne * cap)).reshape(bsz, ne, cap, d)
    else:
        xg = jax.vmap(lambda hb, ib: hb[ib])(h, idx)
    return xg, g[..., None], idx


def _unroute(y, idx, x, gate, final_g):
    bsz, ne, cap, d = y.shape
    slots = jnp.broadcast_to(jnp.arange(ne * cap, dtype=jnp.int32), (bsz, ne * cap))
    toks, perm = lax.sort_key_val(idx.reshape(bsz, ne * cap).astype(jnp.int32), slots, dimension=-1)
    ys = jax.vmap(lambda yb, pb: yb[pb])(y.reshape(bsz, ne * cap, d), perm)
    return _combine(ys, toks, x, gate, final_g)


def _pad_to(w, axis, size):
    pad = [(0, 0)] * w.ndim
    pad[axis] = (0, size - w.shape[axis])
    return jnp.pad(w, pad)


def kernel(x, c, ctx, c_ctx, ada_w, ada_b, norm_mix_g, norm_ffn_g, norm_final_g, ev_w_in, ev_conv_w, ev_conv_b, ev_decay_up, ev_decay_b, ev_norm_g, ev_w_out, od_w_in, od_conv_w, od_conv_b, od_gate_a_w, od_gate_a_b, od_gate_x_w, od_gate_x_b, od_lambda, od_sg_norm_g, od_sg_w, od_sg_b, od_w_out, moe_router, moe_w_gate, moe_w_up, moe_w_down):
    depth = ada_w.shape[0]
    bsz, seq, d = x.shape
    ne = moe_router.shape[-1]
    x_lat, x_ctx = x, ctx
    cond = _pad_to(jnp.concatenate([c, c_ctx[None, :]], axis=0), 0, -(-(bsz + 1) // SUBLANES) * SUBLANES)
    ada = _adaln(cond, ada_w, ada_b.reshape(depth, 1, N_MOD * d))

    def vecs(m):
        return [v.reshape(bsz, 1, d) for v in m]

    for l in range(depth):
        last = l == depth - 1
        even = l % 2 == 0
        i = l // 2
        mod = [ada[l, :bsz, k * d:(k + 1) * d] for k in range(N_MOD)]
        mod_c = [jnp.broadcast_to(ada[l, bsz:bsz + 1, k * d:(k + 1) * d], (bsz, d)) for k in range(N_MOD)]
        wr = _pad_to(moe_router[l], 1, LANES).astype(_MXU)

        if even:
            w = ev_w_in[i]
            scw = ev_conv_w.shape[-1]
            key = ev_decay_up.shape[-1]
            val = ev_norm_g.shape[-1]
            o_q = 3 * scw
            o_g = o_q + key
            o_k = o_g + val
            o_v = o_k + key
            o_lr = o_v + val
            w_in = jnp.concatenate([w[:, :o_q], w[:, o_g:o_k], w[:, o_q:o_g], w[:, o_k:o_lr]], axis=1).astype(_MXU)
            w_lr = _pad_to(w[:, o_lr:], 1, LANES).astype(_MXU)
            up = jnp.zeros((LANES, 2 * key), _F32)
            up = up.at[:GLA_RANK, :key].set(ev_decay_up[i, 0]).at[GLA_RANK:2 * GLA_RANK, key:].set(ev_decay_up[i, 1])
            decay = (w_lr, up.astype(_MXU), ev_decay_b[i].reshape(1, 2 * key))
            w_out = ev_w_out[i].astype(_MXU)
            conv_w = _pad_to(ev_conv_w[i], 0, SUBLANES)
            consts = (conv_w, ev_conv_b[i].reshape(1, scw), ev_norm_g[i].reshape(1, val))
            epi_width = 3 * scw + val
            assert (3 * scw + val) % (2 * key + val) == 0
            qkv_block = epi_width // (2 * key + val)
            dk, dv = key // GLA_HEADS, val // GLA_HEADS
            scale = dk ** -0.5
            zero_st = jnp.zeros((bsz, GLA_HEADS // 2, dv, 2 * dk), _F32)

            def mixer(xs, m, s0, row_len):
                sh, sc = vecs(m[:2])
                z, la = _pre(xs, norm_mix_g[l] * (1.0 + sc), sh, w_in, decay)
                o_f, o_b, s_f, s_b = _gla(z, la, s0[0], s0[1], qkv_block, scale)
                body = functools.partial(_post_even_kernel, row_len=row_len)
                return (body, z, epi_width, (o_f, o_b), consts, w_out), (s_f, s_b)

            ctx_args, state = mixer(x_ctx, mod_c, (zero_st, zero_st), x_ctx.shape[1])
            lat_args, _ = mixer(x_lat, mod, state, GRID_W)
        else:
            w = od_w_in[i]
            width = od_conv_w.shape[-1]
            hd = width // LRU_HEADS
            w_in = jnp.concatenate([w[:, width:], w[:, :width]], axis=1).astype(_MXU)
            w_out = od_w_out[i].astype(_MXU)
            conv_w = _pad_to(od_conv_w[i], 0, SUBLANES)
            conv_b = od_conv_b[i].reshape(1, width)
            wgate = jnp.concatenate([od_gate_a_w[i], od_gate_x_w[i]], axis=-1).astype(_MXU)
            ba = od_gate_a_b[i].reshape(2, 1, width)
            bx = od_gate_x_b[i].reshape(2, 1, width)
            sp = (-LRU_C * jax.nn.softplus(-od_lambda[i])).reshape(2, 1, width)
            sgb = jnp.broadcast_to(od_sg_b[i][:, :, None], (SG_GROUPS, SG_CHUNK, width // SG_GROUPS))
            consts = (od_sg_norm_g[i].reshape(1, width), od_sg_w[i].astype(_MXU), sgb)
            zero_st = jnp.zeros((bsz, 1, width), _F32)

            def mixer(xs, m, s0, row_len):
                sh, sc = vecs(m[:2])
                (z,) = _pre(xs, norm_mix_g[l] * (1.0 + sc), sh, w_in)
                h_f, h_b, c_f, c_b = _lru(z, 3, conv_w, conv_b, wgate, ba, bx, sp, s0[0], s0[1])
                return (_post_odd_kernel, z, 3 * width, (h_f, h_b), consts, w_out), (c_f, c_b)

            ctx_args, state = mixer(x_ctx, mod_c, (zero_st, zero_st), None)
            lat_args, _ = mixer(x_lat, mod, state, None)

        def post(args, xs, m):
            gate1, sh, sc, gate2 = vecs(m[2:])
            xs, h, logits = _post(*args, xs, gate1, norm_ffn_g[l] * (1.0 + sc), sh, wr)
            return (xs, gate2) + _route(h, logits[..., :ne])

        x_lat, gate_lat, xg_lat, g_lat, idx_lat = post(lat_args, x_lat, mod)
        weights = (moe_w_gate, moe_w_up, moe_w_down, l)
        if last:
            (y_lat,) = _moe_ffn(xg_lat, g_lat, None, *weights)
        else:
            x_ctx, gate_ctx, xg_ctx, g_ctx, idx_ctx = post(ctx_args, x_ctx, mod_c)
            y_lat, y_ctx = _moe_ffn(xg_lat, g_lat, (xg_ctx, g_ctx), *weights)
            x_ctx = _unroute(y_ctx, idx_ctx, x_ctx, gate_ctx, None)
        x_lat = _unroute(y_lat, idx_lat, x_lat, gate_lat, norm_final_g if last else None)
    return x_lat
```

```python
import functools

import jax
import jax.numpy as jnp
from jax import lax
from jax.experimental import pallas as pl
from jax.experimental.pallas import tpu as pltpu

_MXU = jnp.bfloat16
_ACT = jnp.bfloat16
_F32 = jnp.float32
NORM_EPS = 1e-6
N_MOD = 6
GRID_W = 64
GLA_HEADS = 4
GLA_RANK = 16
GLA_TAU = 16.0
GLA_CHUNK = 64
LRU_HEADS = 4
LRU_C = 8.0
SG_GROUPS = 4
SG_CHUNK = 128
EC_CAPACITY = 2
LANES = 128
SUBLANES = 8
HALO = 16
VMEM_LIMIT = 48 * 1024 * 1024
MOE_VMEM_LIMIT = 56 * 1024 * 1024
MOE_ROWS = 512
SHORT_SEQ = 256
PROJ_ROWS = 512
LRU_ROWS = 1024


def _cparams(*sem):
    return pltpu.CompilerParams(dimension_semantics=sem, vmem_limit_bytes=VMEM_LIMIT)


def _row_tile(t, cap=256):
    tm = min(t, cap)
    assert t % tm == 0
    return tm


def _dot(a, b):
    return jnp.dot(a.astype(_MXU), b.astype(_MXU), preferred_element_type=_F32)


def _dot_nt(a, b):
    return lax.dot_general(a.astype(_MXU), b.astype(_MXU), (((1,), (1,)), ((), ())), preferred_element_type=_F32)


def _dot_tn(a, b):
    return lax.dot_general(a.astype(_MXU), b.astype(_MXU), (((0,), (0,)), ((), ())), preferred_element_type=_F32)


def _gelu(x):
    return 0.5 * x * (1.0 + jnp.tanh(0.7978845608028654 * (x + 0.044715 * (x * x * x))))


def _sigmoid(x):
    return 0.5 * jnp.tanh(0.5 * x) + 0.5


def _silu(x):
    return x * _sigmoid(x)


def _log_sigmoid(x):
    return jnp.minimum(x, 0.0) - jnp.log(1.0 + jnp.exp(-jnp.abs(x)))


def _norm_mod(x, a, s):
    ms = jnp.mean(x * x, axis=-1, keepdims=True)
    return x * lax.rsqrt(ms + NORM_EPS) * a + s


def _adaln_kernel(c_ref, w_ref, b_ref, o_ref):
    o_ref[...] = _dot(_silu(c_ref[...]), w_ref[...]) + b_ref[...]


def _adaln(cond, w, b):
    depth, d, n = w.shape
    r = cond.shape[0]
    tn = n // N_MOD
    return pl.pallas_call(
        _adaln_kernel,
        grid=(depth, N_MOD),
        in_specs=[
            pl.BlockSpec((r, d), lambda l, j: (0, 0)),
            pl.BlockSpec((None, d, tn), lambda l, j: (l, 0, j)),
            pl.BlockSpec((None, 1, tn), lambda l, j: (l, 0, j)),
        ],
        out_specs=pl.BlockSpec((None, r, tn), lambda l, j: (l, 0, j)),
        out_shape=jax.ShapeDtypeStruct((depth, r, n), _F32),
        compiler_params=_cparams("parallel", "parallel"),
        name="adaln",
    )(cond, w, b)


def _pre_kernel(x_ref, a_ref, s_ref, w_ref, o_ref):
    h = _norm_mod(x_ref[...], a_ref[...], s_ref[...])
    o_ref[...] = _dot(h, w_ref[...]).astype(o_ref.dtype)


def _pre_even_kernel(x_ref, a_ref, s_ref, w_ref, wlr_ref, up_ref, db_ref, o_ref, la_ref):
    h = _norm_mod(x_ref[...], a_ref[...], s_ref[...]).astype(_MXU)
    o_ref[...] = _dot(h, w_ref[...]).astype(o_ref.dtype)
    lr = _dot(h, wlr_ref[...])
    la_ref[...] = _log_sigmoid(_dot(lr, up_ref[...]) + db_ref[...]) * (1.0 / GLA_TAU)


def _pre(x, a, s, w, decay=None):
    bsz, t, d = x.shape
    n = w.shape[1]
    tm = _row_tile(t, PROJ_ROWS)
    row = lambda b, i: (b, i, 0)
    vec = pl.BlockSpec((None, 1, d), lambda b, i: (b, 0, 0))
    full = lambda arr: pl.BlockSpec(arr.shape, lambda b, i: (0,) * arr.ndim)
    in_specs = [pl.BlockSpec((None, tm, d), row), vec, vec, full(w)]
    out_specs = [pl.BlockSpec((None, tm, n), row)]
    out_shape = [jax.ShapeDtypeStruct((bsz, t, n), _ACT)]
    args = [x, a, s, w]
    body = _pre_kernel
    if decay is not None:
        body = _pre_even_kernel
        n_la = decay[1].shape[1]
        in_specs += [full(v) for v in decay]
        out_specs.append(pl.BlockSpec((None, tm, n_la), row))
        out_shape.append(jax.ShapeDtypeStruct((bsz, t, n_la), _F32))
        args += list(decay)
    return pl.pallas_call(
        body, grid=(bsz, t // tm), in_specs=in_specs, out_specs=out_specs, out_shape=out_shape,
        compiler_params=_cparams("parallel", "parallel"), name="pre_proj",
    )(*args)


def _bmm(a, b, dims):
    return lax.dot_general(a.astype(_MXU), b.astype(_MXU), (dims, ((0,), (0,))), preferred_element_type=_F32)


def _cumsum_chunks(tri, x):
    hi = x.astype(_MXU)
    r1 = x - hi.astype(_F32)
    mid = r1.astype(_MXU)
    lo = (r1 - mid.astype(_F32)).astype(_MXU)
    mm = ((2,), (1,))
    return _bmm(tri, hi, mm) + _bmm(tri, mid, mm) + _bmm(tri, lo, mm)


def _gla_block(qkv_ref, la_ref, o_ref, st_ref, sin_ref, keep, scale, reverse):
    c = GLA_CHUNK
    tb = qkv_ref.shape[0]
    n = tb // c
    key = la_ref.shape[-1]
    dv = (qkv_ref.shape[-1] - 2 * key) // GLA_HEADS
    dk = key // GLA_HEADS
    tri = jnp.broadcast_to(keep.astype(_MXU)[None], (n, c, c))
    b = _cumsum_chunks(tri, la_ref[...].reshape(n, c, key))
    btot = b[:, 0:1] if reverse else b[:, c - 1:c]
    q = qkv_ref[:, 0:key].astype(_F32).reshape(n, c, key)
    k = qkv_ref[:, key:2 * key].astype(_F32).reshape(n, c, key)
    qd = q * (scale * jnp.exp(b))
    kd = k * jnp.exp(-b)
    kr = k * jnp.exp(btot - b)
    dec = jnp.exp(btot)
    pair = 2 * dk
    lane_lo = lax.broadcasted_iota(jnp.int32, (n, c, pair), 2) < dk
    order = range(n - 1, -1, -1) if reverse else range(n)
    for p in range(GLA_HEADS // 2):
        lanes = slice(p * pair, (p + 1) * pair)
        qd_p, kd_p, kr_p = qd[:, :, lanes], kd[:, :, lanes], kr[:, :, lanes]
        vs = [qkv_ref[:, 2 * key + h * dv:2 * key + (h + 1) * dv].reshape(n, c, dv) for h in (2 * p, 2 * p + 1)]
        km = jnp.concatenate([jnp.where(lane_lo, kr_p, 0.0), jnp.where(lane_lo, 0.0, kr_p)], axis=1)
        upd = _bmm(jnp.concatenate(vs, axis=1), km, ((1,), (1,)))
        st = st_ref[p]
        for j in order:
            sin_ref[j] = st
            st = dec[j, :, lanes] * st + upd[j]
        st_ref[p] = st
        s_in = sin_ref[...]
        for s in range(2):
            h = 2 * p + s
            qm = jnp.where(lane_lo if s == 0 else jnp.logical_not(lane_lo), qd_p, 0.0)
            att = jnp.where(keep[None], _bmm(qm, kd_p, ((2,), (2,))), 0.0)
            o = _bmm(att, vs[s], ((2,), (1,))) + _bmm(qm, s_in, ((2,), (2,)))
            o_ref[:, h * dv:(h + 1) * dv] = o.reshape(tb, dv).astype(o_ref.dtype)


def _gla_kernel(qkv_f_ref, la_f_ref, qkv_b_ref, la_b_ref, s0f_ref, s0b_ref, of_ref, ob_ref, sf_ref, sb_ref, sin_ref, *,
                scale):
    i = pl.program_id(1)

    @pl.when(i == 0)
    def _():
        sf_ref[...] = s0f_ref[...]
        sb_ref[...] = s0b_ref[...]

    c = GLA_CHUNK
    ri = lax.broadcasted_iota(jnp.int32, (c, c), 0)
    ci = lax.broadcasted_iota(jnp.int32, (c, c), 1)
    _gla_block(qkv_f_ref, la_f_ref, of_ref, sf_ref, sin_ref, ri >= ci, scale, False)
    _gla_block(qkv_b_ref, la_b_ref, ob_ref, sb_ref, sin_ref, ri <= ci, scale, True)


def _gla(z, la, s0f, s0b, qkv_block, scale):
    bsz, t, _ = z.shape
    key = la.shape[-1] // 2
    st_shape = s0f.shape[1:]
    val = GLA_HEADS * st_shape[1]
    width = 2 * key + val
    tb = _row_tile(t, 512)
    nt = t // tb
    st_spec = pl.BlockSpec((None,) + st_shape, lambda b, i: (b, 0, 0, 0))
    outs = pl.pallas_call(
        functools.partial(_gla_kernel, scale=scale),
        grid=(bsz, nt),
        in_specs=[
            pl.BlockSpec((None, tb, width), lambda b, i: (b, i, qkv_block)),
            pl.BlockSpec((None, tb, key), lambda b, i: (b, i, 0)),
            pl.BlockSpec((None, tb, width), lambda b, i: (b, nt - 1 - i, qkv_block)),
            pl.BlockSpec((None, tb, key), lambda b, i: (b, nt - 1 - i, 1)),
            st_spec, st_spec,
        ],
        out_specs=[
            pl.BlockSpec((None, tb, val), lambda b, i: (b, i, 0)),
            pl.BlockSpec((None, tb, val), lambda b, i: (b, nt - 1 - i, 0)),
            st_spec, st_spec,
        ],
        out_shape=[
            jax.ShapeDtypeStruct((bsz, t, val), _ACT),
            jax.ShapeDtypeStruct((bsz, t, val), _ACT),
            jax.ShapeDtypeStruct(s0f.shape, _F32),
            jax.ShapeDtypeStruct(s0b.shape, _F32),
        ],
        scratch_shapes=[pltpu.VMEM((tb // GLA_CHUNK,) + st_shape[1:], _F32)],
        compiler_params=_cparams("parallel", "arbitrary"),
        name="gla_scan",
    )(z, la, z, la, s0f, s0b)
    return outs


def _lru_coeffs(x_ref, prev_ref, next_ref, first, last, cw_ref, cb_ref, wg_ref, ba_ref, bx_ref, sp_ref, pad_scr):
    tm, hd = x_ref.shape
    steps = tm // SUBLANES
    pitch = pad_scr.shape[0] // SUBLANES
    for sg in range(SUBLANES):
        pad_scr[sg * pitch:sg * pitch + steps, :] = x_ref[sg * steps:(sg + 1) * steps, :].astype(_F32)
    x = jnp.stack([pad_scr[pl.ds(s, SUBLANES, stride=pitch), :] for s in range(steps)], axis=0)
    prev = jnp.where(first, 0.0, prev_ref[...].astype(_F32)[HALO - SUBLANES:HALO])
    nxt = jnp.where(last, 0.0, next_ref[...].astype(_F32)[0:SUBLANES])
    seg = lax.broadcasted_iota(jnp.int32, (SUBLANES, 1), 0)

    def from_prev_segment(v, fill):
        return jnp.where(seg == 0, fill, pltpu.roll(v, 1, 0))

    def from_next_segment(v, fill):
        return jnp.where(seg == SUBLANES - 1, fill, pltpu.roll(v, SUBLANES - 1, 0))

    m1_edge = from_prev_segment(x[steps - 1], prev[SUBLANES - 1:SUBLANES])
    m2_edge = from_prev_segment(x[steps - 2], prev[SUBLANES - 2:SUBLANES - 1])
    p1_edge = from_next_segment(x[0], nxt[0:1])
    x_m1 = jnp.concatenate([m1_edge[None], x[:steps - 1]], axis=0)
    x_m2 = jnp.concatenate([m2_edge[None], m1_edge[None], x[:steps - 2]], axis=0)
    x_p1 = jnp.concatenate([x[1:], p1_edge[None]], axis=0)
    xc = cb_ref[...] + cw_ref[0:1] * x_m2 + cw_ref[1:2] * x_m1 + cw_ref[2:3] * x + cw_ref[3:4] * x_p1
    xc = xc.reshape(tm, hd)
    g = _dot(xc, wg_ref[...])
    rg = _sigmoid(g[:, :hd] + ba_ref[...])
    ig = _sigmoid(g[:, hd:] + bx_ref[...])
    log_a = rg * sp_ref[...]
    th = jnp.tanh(0.5 * log_a)
    rcp = 1.0 / (1.0 - th)
    a = (1.0 + th) * rcp
    bb = (2.0 * jnp.sqrt(-th) * rcp) * (ig * xc)
    shape = (steps, SUBLANES, hd)
    return a.reshape(shape), bb.reshape(shape)


def _lru_scan_tile(a, bb, h_scr, p_scr, c_scr, pad_scr, h_ref, carry_ref, reverse):
    steps = a.shape[0]
    pitch = pad_scr.shape[0] // SUBLANES
    order = range(steps - 1, -1, -1) if reverse else range(steps)
    h = None
    for s in order:
        h = bb[s] if h is None else a[s] * h + bb[s]
        prod = a[s] if s == order[0] else a[s] * prod
        h_scr[s] = h
        p_scr[s] = prod
    c = carry_ref[...]
    for sg in (range(SUBLANES - 1, -1, -1) if reverse else range(SUBLANES)):
        c_scr[sg:sg + 1, :] = c
        c = h[sg:sg + 1] + prod[sg:sg + 1] * c
    carry_ref[...] = c
    c_in = c_scr[...]
    for s in range(steps):
        pad_scr[pl.ds(s, SUBLANES, stride=pitch), :] = h_scr[s] + p_scr[s] * c_in
    for sg in range(SUBLANES):
        h_ref[sg * steps:(sg + 1) * steps, :] = pad_scr[sg * pitch:sg * pitch + steps, :].astype(h_ref.dtype)


def _lru_kernel(xf_ref, xf_prev_ref, xf_next_ref, xb_ref, xb_prev_ref, xb_next_ref, cw_ref, cb_ref, wg_ref, ba_ref,
                bx_ref, sp_ref, h0f_ref, h0b_ref, hf_ref, hb_ref, cf_ref, cbk_ref, a_scr, b_scr, c_scr, pad_scr):
    i = pl.program_id(2)
    nt = pl.num_programs(2)

    @pl.when(i == 0)
    def _():
        cf_ref[...] = h0f_ref[...]
        cbk_ref[...] = h0b_ref[...]

    a, bb = _lru_coeffs(xf_ref, xf_prev_ref, xf_next_ref, i == 0, i == nt - 1, cw_ref, cb_ref, wg_ref.at[0],
                        ba_ref.at[0], bx_ref.at[0], sp_ref.at[0], pad_scr)
    _lru_scan_tile(a, bb, a_scr, b_scr, c_scr, pad_scr, hf_ref, cf_ref, False)
    a, bb = _lru_coeffs(xb_ref, xb_prev_ref, xb_next_ref, i == nt - 1, i == 0, cw_ref, cb_ref, wg_ref.at[1],
                        ba_ref.at[1], bx_ref.at[1], sp_ref.at[1], pad_scr)
    _lru_scan_tile(a, bb, a_scr, b_scr, c_scr, pad_scr, hb_ref, cbk_ref, True)


def _lru(z, x_block, conv_w, conv_b, wg, ba, bx, sp, h0f, h0b):
    bsz, t, _ = z.shape
    width = conv_w.shape[1]
    hd = width // LRU_HEADS
    assert hd == LANES
    tm = _row_tile(t, LRU_ROWS)
    nt = t // tm
    gpt = tm // SUBLANES
    hpt = tm // HALO
    nhb = t // HALO
    pitch = (gpt + SUBLANES - 1) // SUBLANES * SUBLANES
    pitch += SUBLANES * (1 - (pitch // SUBLANES) % 2)
    col = lambda h: x_block * LRU_HEADS + h

    def tile(rev):
        return lambda b, h, i: (b, (nt - 1 - i) if rev else i, col(h))

    def prev(rev):
        return lambda b, h, i: (b, jnp.maximum(((nt - 1 - i) if rev else i) * hpt - 1, 0), col(h))

    def nxt(rev):
        return lambda b, h, i: (b, jnp.minimum((((nt - 1 - i) if rev else i) + 1) * hpt, nhb - 1), col(h))

    lane = lambda rows: pl.BlockSpec((rows, hd), lambda b, h, i: (0, h))
    dirs = pl.BlockSpec((2, 1, hd), lambda b, h, i: (0, 0, h))
    st_spec = pl.BlockSpec((None, 1, hd), lambda b, h, i: (b, 0, h))
    in_specs = []
    for rev in (False, True):
        in_specs += [pl.BlockSpec((None, tm, hd), tile(rev)),
                     pl.BlockSpec((None, HALO, hd), prev(rev)),
                     pl.BlockSpec((None, HALO, hd), nxt(rev))]
    in_specs += [lane(conv_w.shape[0]), lane(1),
                 pl.BlockSpec((2, None, hd, 2 * hd), lambda b, h, i: (0, h, 0, 0)),
                 dirs, dirs, dirs, st_spec, st_spec]
    return pl.pallas_call(
        _lru_kernel,
        grid=(bsz, LRU_HEADS, nt),
        in_specs=in_specs,
        out_specs=[
            pl.BlockSpec((None, tm, hd), lambda b, h, i: (b, i, h)),
            pl.BlockSpec((None, tm, hd), lambda b, h, i: (b, nt - 1 - i, h)),
            st_spec, st_spec,
        ],
        out_shape=[
            jax.ShapeDtypeStruct((bsz, t, width), _ACT),
            jax.ShapeDtypeStruct((bsz, t, width), _ACT),
            jax.ShapeDtypeStruct((bsz, 1, width), _F32),
            jax.ShapeDtypeStruct((bsz, 1, width), _F32),
        ],
        scratch_shapes=[pltpu.VMEM((gpt, SUBLANES, hd), _F32), pltpu.VMEM((gpt, SUBLANES, hd), _F32),
                        pltpu.VMEM((SUBLANES, hd), _F32), pltpu.VMEM((SUBLANES * pitch, hd), _F32)],
        compiler_params=_cparams("parallel", "parallel", "arbitrary"),
        name="lru_scan",
    )(z, z, z, z, z, z, conv_w, conv_b, wg, ba, bx, sp, h0f, h0b)


def _residual_ffn_prenorm(y, x_ref, gate_ref, a_ref, s_ref, wr_ref, xo_ref, h_ref, na_ref, n_experts):
    xn = x_ref[...] + gate_ref[...] * y
    xo_ref[...] = xn
    h = _norm_mod(xn, a_ref[...], s_ref[...]).astype(_MXU)
    h_ref[...] = h
    logits = _dot(h, wr_ref[...])
    lane = lax.broadcasted_iota(jnp.int32, logits.shape, 1)
    logits = jnp.where(lane < n_experts, logits, -jnp.inf)
    e = jnp.exp(logits - jnp.max(logits, axis=-1, keepdims=True))
    na_ref[...] = -(e / jnp.sum(e, axis=-1, keepdims=True))


def _shift_rows(v, up):
    r, w = v.shape
    v3 = v.reshape(r // SUBLANES, SUBLANES, w)
    sub = lax.broadcasted_iota(jnp.int32, (1, SUBLANES, 1), 1)
    if up:
        rot = pltpu.roll(v3, SUBLANES - 1, 1)
        out = jnp.where(sub == SUBLANES - 1, jnp.concatenate([rot[1:], rot[-1:]], axis=0), rot)
    else:
        rot = pltpu.roll(v3, 1, 1)
        out = jnp.where(sub == 0, jnp.concatenate([rot[:1], rot[:-1]], axis=0), rot)
    return out.reshape(r, w)


def _post_even_kernel(za_ref, of_ref, ob_ref, cw_ref, cb_ref, ng_ref, w_ref, x_ref, gate_ref, a_ref, s_ref, wr_ref,
                      xo_ref, h_ref, na_ref, y_scr, *, row_len, n_experts):
    tm = za_ref.shape[0]
    scw = cw_ref.shape[1]
    val = ng_ref.shape[1]
    dv = val // GLA_HEADS
    bg = za_ref[:, 0:scw].astype(_F32)
    cx = za_ref[:, scw:2 * scw].astype(_F32) * za_ref[:, 2 * scw:3 * scw].astype(_F32)
    r = lax.broadcasted_iota(jnp.int32, (tm, 1), 0) % row_len
    left = jnp.where(r == 0, 0.0, _shift_rows(cx, False))
    right = jnp.where(r == row_len - 1, 0.0, _shift_rows(cx, True))
    conv = cb_ref[...] + cw_ref[0:1] * left + cw_ref[1:2] * cx + cw_ref[2:3] * right
    acc = _dot(bg * conv, w_ref[0:scw, :])
    for h in range(GLA_HEADS):
        lanes = slice(h * dv, (h + 1) * dv)
        o = of_ref[:, lanes].astype(_F32) + ob_ref[:, lanes].astype(_F32)
        o = o * lax.rsqrt(jnp.mean(o * o, axis=-1, keepdims=True) + NORM_EPS)
        gate = za_ref[:, 3 * scw + h * dv:3 * scw + (h + 1) * dv].astype(_F32)
        y_scr[:, lanes] = o * ng_ref[:, lanes] * _silu(gate)
    acc = acc + _dot(y_scr[...], w_ref[scw:scw + val, :])
    _residual_ffn_prenorm(acc, x_ref, gate_ref, a_ref, s_ref, wr_ref, xo_ref, h_ref, na_ref, n_experts)


def _post_odd_kernel(zc_ref, hf_ref, hb_ref, lng_ref, sgw_ref, sgb_ref, w_ref, x_ref, gate_ref, a_ref, s_ref, wr_ref,
                     xo_ref, h_ref, na_ref, y_scr, *, n_experts):
    tm = zc_ref.shape[0]
    width = hf_ref.shape[1]
    gdim = width // SG_GROUPS
    y_rec = (hf_ref[...].astype(_F32) + hb_ref[...].astype(_F32)) * _gelu(zc_ref[:, 0:width].astype(_F32))
    acc = _dot(y_rec, w_ref[0:width, :])
    gv = _gelu(zc_ref[:, 2 * width:3 * width].astype(_F32))
    gc = gv - jnp.mean(gv, axis=-1, keepdims=True)
    vb = gc * lax.rsqrt(jnp.mean(gc * gc, axis=-1, keepdims=True) + NORM_EPS) * lng_ref[...]
    for n in range(tm // SG_CHUNK):
        rows = slice(n * SG_CHUNK, (n + 1) * SG_CHUNK)
        for g in range(SG_GROUPS):
            lanes = slice(g * gdim, (g + 1) * gdim)
            mixed = _dot(sgw_ref[g], vb[rows, lanes]) + sgb_ref[g]
            u = zc_ref[rows, width + g * gdim:width + (g + 1) * gdim].astype(_F32)
            y_scr[rows, lanes] = _gelu(u) * mixed
    acc = acc + _dot(y_scr[...], w_ref[width:2 * width, :])
    _residual_ffn_prenorm(acc, x_ref, gate_ref, a_ref, s_ref, wr_ref, xo_ref, h_ref, na_ref, n_experts)


def _post(body, z, z_width, seq_in, consts, w_out, x, gate, a, s, wr, n_experts):
    bsz, t, d = x.shape
    tm = _row_tile(t, PROJ_ROWS)
    row = lambda b, i: (b, i, 0)
    vec = pl.BlockSpec((None, 1, d), lambda b, i: (b, 0, 0))
    full = lambda arr: pl.BlockSpec(arr.shape, lambda b, i: (0,) * arr.ndim)
    width = seq_in[0].shape[-1]
    in_specs = [pl.BlockSpec((None, tm, z_width), row)]
    in_specs += [pl.BlockSpec((None, tm, width), row) for _ in seq_in]
    in_specs += [full(v) for v in consts] + [full(w_out), pl.BlockSpec((None, tm, d), row), vec, vec, vec, full(wr)]
    return pl.pallas_call(
        functools.partial(body, n_experts=n_experts),
        grid=(bsz, t // tm),
        in_specs=in_specs,
        out_specs=[
            pl.BlockSpec((None, tm, d), row),
            pl.BlockSpec((None, tm, d), row),
            pl.BlockSpec((None, tm, LANES), row),
        ],
        out_shape=[
            jax.ShapeDtypeStruct((bsz, t, d), _F32),
            jax.ShapeDtypeStruct((bsz, t, d), _MXU),
            jax.ShapeDtypeStruct((bsz, t, LANES), _F32),
        ],
        scratch_shapes=[pltpu.VMEM((tm, width), _F32)],
        compiler_params=_cparams("parallel", "parallel"),
        name="post_proj",
    )(z, *seq_in, *consts, w_out, x, gate, a, s, wr)


def _moe_ffn_kernel(*refs, with_short):
    if with_short:
        x_ref, g_ref, xs_ref, gs_ref, wg_ref, wu_ref, wd_ref, o_ref, os_ref, wg_scr, wu_scr, wd_scr = refs
    else:
        x_ref, g_ref, wg_ref, wu_ref, wd_ref, o_ref, wg_scr, wu_scr, wd_scr = refs
    b = pl.program_id(1)

    @pl.when(b == 0)
    def _():
        wg_scr[...] = wg_ref[...].astype(_MXU)
        wu_scr[...] = wu_ref[...].astype(_MXU)
        wd_scr[...] = wd_ref[...].astype(_MXU)

    def ffn(xr, gr, outr):
        bb, cap, d = xr.shape
        x = xr[...].reshape(bb * cap, d)
        gate = _dot(x, wg_scr[...])
        up = _dot(x, wu_scr[...])
        out = _dot(_silu(gate) * up, wd_scr[...]) * gr[...].reshape(bb * cap, 1)
        outr[...] = out.reshape(bb, cap, d).astype(outr.dtype)

    if with_short:
        last = pl.num_programs(1) - 1
        pl.when(b < last)(lambda: ffn(x_ref, g_ref, o_ref))
        pl.when(b == last)(lambda: ffn(xs_ref, gs_ref, os_ref))
    else:
        ffn(x_ref, g_ref, o_ref)


def _moe_ffn(xg, g, short, wg, wu, wd, layer):
    bsz, ne, cap, d = xg.shape
    f = wg.shape[-1]
    bb = max(1, min(bsz, MOE_ROWS // cap))
    assert bsz % bb == 0
    nb = bsz // bb
    wsel = lambda e, b: (layer, e, 0, 0)
    w_specs = [pl.BlockSpec((None, None, d, f), wsel), pl.BlockSpec((None, None, d, f), wsel),
               pl.BlockSpec((None, None, f, d), wsel)]
    out_shape = [jax.ShapeDtypeStruct((bsz, ne, cap, d), _MXU)]
    if short is None:
        tok = lambda e, b: (b, e, 0, 0)
        steps = nb
        in_specs = [pl.BlockSpec((bb, None, cap, d), tok), pl.BlockSpec((bb, None, cap, 1), tok)] + w_specs
        out_specs = [pl.BlockSpec((bb, None, cap, d), tok)]
        args = (xg, g, wg, wu, wd)
    else:
        xs, gs = short
        bs, _, caps, _ = xs.shape
        assert bs * caps <= MOE_ROWS
        tok = lambda e, b: (jnp.minimum(b, nb - 1), e, 0, 0)
        tok_s = lambda e, b: (0, e, 0, 0)
        steps = nb + 1
        in_specs = [pl.BlockSpec((bb, None, cap, d), tok), pl.BlockSpec((bb, None, cap, 1), tok),
                    pl.BlockSpec((bs, None, caps, d), tok_s), pl.BlockSpec((bs, None, caps, 1), tok_s)] + w_specs
        out_specs = [pl.BlockSpec((bb, None, cap, d), tok), pl.BlockSpec((bs, None, caps, d), tok_s)]
        out_shape.append(jax.ShapeDtypeStruct((bs, ne, caps, d), _MXU))
        args = (xg, g, xs, gs, wg, wu, wd)
    return pl.pallas_call(
        functools.partial(_moe_ffn_kernel, with_short=short is not None),
        grid=(ne, steps),
        in_specs=in_specs,
        out_specs=out_specs,
        out_shape=out_shape,
        scratch_shapes=[pltpu.VMEM((d, f), _MXU), pltpu.VMEM((d, f), _MXU), pltpu.VMEM((f, d), _MXU)],
        compiler_params=pltpu.CompilerParams(dimension_semantics=("parallel", "arbitrary"),
                                             vmem_limit_bytes=MOE_VMEM_LIMIT),
        name="moe_ffn",
    )(*args)


def _dispatch_kernel(idx_ref, h_ref, o_ref):
    p, t = idx_ref.shape[0], h_ref.shape[0]
    onehot = jnp.where(lax.broadcasted_iota(jnp.int32, (p, t), 1) == idx_ref[...], 1.0, 0.0)
    o_ref[...] = _dot(onehot, h_ref[...]).astype(o_ref.dtype)


def _dispatch(h, idx):
    bsz, t, d = h.shape
    p = idx.shape[1]
    return pl.pallas_call(
        _dispatch_kernel,
        grid=(bsz,),
        in_specs=[pl.BlockSpec((None, p, 1), lambda b: (b, 0, 0)), pl.BlockSpec((None, t, d), lambda b: (b, 0, 0))],
        out_specs=pl.BlockSpec((None, p, d), lambda b: (b, 0, 0)),
        out_shape=jax.ShapeDtypeStruct((bsz, p, d), h.dtype),
        compiler_params=_cparams("parallel"),
        name="moe_dispatch",
    )(idx.reshape(bsz, p, 1), h)


def _combine_kernel(lo_ref, hi_ref, tok_ref, y_ref, x_ref, gate_ref, fg_ref, o_ref, acc_ref, *, final_norm):
    b = pl.program_id(0)
    j = pl.program_id(1)
    tm = acc_ref.shape[0]
    ch = tok_ref.shape[-1]
    acc_ref[...] = jnp.zeros_like(acc_ref)
    rows = j * tm + lax.broadcasted_iota(jnp.int32, (tm, ch), 0)

    def body(c, carry):
        onehot = jnp.where(rows == tok_ref[c], 1.0, 0.0)
        acc_ref[...] += _dot(onehot, y_ref[pl.ds(pl.multiple_of(c * ch, ch), ch), :])
        return carry

    lax.fori_loop(lo_ref[b, j], hi_ref[b, j] + 1, body, 0)
    xn = x_ref[...] + gate_ref[...] * acc_ref[...]
    if final_norm:
        ms = jnp.mean(xn * xn, axis=-1, keepdims=True)
        xn = xn * lax.rsqrt(ms + NORM_EPS) * fg_ref[...]
    o_ref[...] = xn


def _combine(ys, toks, x, gate, final_g):
    bsz, t, d = x.shape
    p = ys.shape[1]
    tm = _row_tile(t, PROJ_ROWS)
    ch = _row_tile(p)
    nt, nc = t // tm, p // ch
    bounds = jnp.arange(nt + 1, dtype=jnp.int32) * tm
    cut = jnp.sum((toks[:, :, None] < bounds[None, None, :]).astype(jnp.int32), axis=1)
    c_lo = jnp.minimum(cut[:, :-1] // ch, nc - 1)
    c_hi = jnp.maximum((cut[:, 1:] - 1) // ch, c_lo)
    fg = jnp.ones((1, d), _F32) if final_g is None else final_g.reshape(1, d)
    grid_spec = pltpu.PrefetchScalarGridSpec(
        num_scalar_prefetch=2,
        grid=(bsz, nt),
        in_specs=[
            pl.BlockSpec((None, nc, 1, ch), lambda b, j, lo, hi: (b, 0, 0, 0)),
            pl.BlockSpec((None, p, d), lambda b, j, lo, hi: (b, 0, 0)),
            pl.BlockSpec((None, tm, d), lambda b, j, lo, hi: (b, j, 0)),
            pl.BlockSpec((None, 1, d), lambda b, j, lo, hi: (b, 0, 0)),
            pl.BlockSpec((1, d), lambda b, j, lo, hi: (0, 0)),
        ],
        out_specs=pl.BlockSpec((None, tm, d), lambda b, j, lo, hi: (b, j, 0)),
        scratch_shapes=[pltpu.VMEM((tm, d), _F32)],
    )
    return pl.pallas_call(
        functools.partial(_combine_kernel, final_norm=final_g is not None),
        grid_spec=grid_spec,
        out_shape=jax.ShapeDtypeStruct((bsz, t, d), _F32),
        compiler_params=pltpu.CompilerParams(dimension_semantics=("parallel", "arbitrary"),
                                             vmem_limit_bytes=MOE_VMEM_LIMIT),
        name="moe_combine",
    )(c_lo, c_hi, toks.reshape(bsz, nc, 1, ch), ys, x, gate, fg)


def _route(h, neg_aff):
    bsz, t, d = h.shape
    ne = neg_aff.shape[-1]
    cap = EC_CAPACITY * t // ne
    keys = jnp.transpose(neg_aff, (1, 0, 2)).reshape(t, bsz * ne)
    token = lax.broadcasted_iota(jnp.int32, (t, bsz * ne), 0)
    keys, token = lax.sort((keys, token), dimension=0, num_keys=1, is_stable=True)
    g = jnp.transpose((-keys[:cap]).reshape(cap, bsz, ne), (1, 2, 0))
    idx = jnp.transpose(token[:cap].reshape(cap, bsz, ne), (1, 2, 0))
    if t <= SHORT_SEQ:
        xg = _dispatch(h, idx.reshape(bsz, ne * cap)).reshape(bsz, ne, cap, d)
    else:
        xg = jax.vmap(lambda hb, ib: hb[ib])(h, idx)
    return xg, g[..., None], idx


def _unroute(y, idx, x, gate, final_g):
    bsz, ne, cap, d = y.shape
    slots = jnp.broadcast_to(jnp.arange(ne * cap, dtype=jnp.int32), (bsz, ne * cap))
    toks, perm = lax.sort_key_val(idx.reshape(bsz, ne * cap).astype(jnp.int32), slots, dimension=-1)
    ys = jax.vmap(lambda yb, pb: yb[pb])(y.reshape(bsz, ne * cap, d), perm)
    return _combine(ys, toks, x, gate, final_g)


def _pad_to(w, axis, size):
    pad = [(0, 0)] * w.ndim
    pad[axis] = (0, size - w.shape[axis])
    return jnp.pad(w, pad)


def kernel(x, c, ctx, c_ctx, ada_w, ada_b, norm_mix_g, norm_ffn_g, norm_final_g, ev_w_in, ev_conv_w, ev_conv_b, ev_decay_up, ev_decay_b, ev_norm_g, ev_w_out, od_w_in, od_conv_w, od_conv_b, od_gate_a_w, od_gate_a_b, od_gate_x_w, od_gate_x_b, od_lambda, od_sg_norm_g, od_sg_w, od_sg_b, od_w_out, moe_router, moe_w_gate, moe_w_up, moe_w_down):
    depth = ada_w.shape[0]
    bsz, seq, d = x.shape
    ne = moe_router.shape[-1]
    x_lat, x_ctx = x, ctx
    cond = _pad_to(jnp.concatenate([c, c_ctx[None, :]], axis=0), 0, -(-(bsz + 1) // SUBLANES) * SUBLANES)
    ada = _adaln(cond, ada_w, ada_b.reshape(depth, 1, N_MOD * d))

    def vecs(m):
        return [v.reshape(bsz, 1, d) for v in m]

    for l in range(depth):
        last = l == depth - 1
        even = l % 2 == 0
        i = l // 2
        mod = [ada[l, :bsz, k * d:(k + 1) * d] for k in range(N_MOD)]
        mod_c = [jnp.broadcast_to(ada[l, bsz:bsz + 1, k * d:(k + 1) * d], (bsz, d)) for k in range(N_MOD)]
        wr = _pad_to(moe_router[l], 1, LANES).astype(_MXU)

        if even:
            w = ev_w_in[i]
            scw = ev_conv_w.shape[-1]
            key = ev_decay_up.shape[-1]
            val = ev_norm_g.shape[-1]
            o_q = 3 * scw
            o_g = o_q + key
            o_k = o_g + val
            o_v = o_k + key
            o_lr = o_v + val
            w_in = jnp.concatenate([w[:, :o_q], w[:, o_g:o_k], w[:, o_q:o_g], w[:, o_k:o_lr]], axis=1).astype(_MXU)
            w_lr = _pad_to(w[:, o_lr:], 1, LANES).astype(_MXU)
            up = jnp.zeros((LANES, 2 * key), _F32)
            up = up.at[:GLA_RANK, :key].set(ev_decay_up[i, 0]).at[GLA_RANK:2 * GLA_RANK, key:].set(ev_decay_up[i, 1])
            decay = (w_lr, up.astype(_MXU), ev_decay_b[i].reshape(1, 2 * key))
            w_out = ev_w_out[i].astype(_MXU)
            conv_w = _pad_to(ev_conv_w[i], 0, SUBLANES)
            consts = (conv_w, ev_conv_b[i].reshape(1, scw), ev_norm_g[i].reshape(1, val))
            epi_width = 3 * scw + val
            assert (3 * scw + val) % (2 * key + val) == 0
            qkv_block = epi_width // (2 * key + val)
            dk, dv = key // GLA_HEADS, val // GLA_HEADS
            scale = dk ** -0.5
            zero_st = jnp.zeros((bsz, GLA_HEADS // 2, dv, 2 * dk), _F32)

            def mixer(xs, m, s0, row_len):
                sh, sc = vecs(m[:2])
                z, la = _pre(xs, norm_mix_g[l] * (1.0 + sc), sh, w_in, decay)
                o_f, o_b, s_f, s_b = _gla(z, la, s0[0], s0[1], qkv_block, scale)
                body = functools.partial(_post_even_kernel, row_len=row_len)
                return (body, z, epi_width, (o_f, o_b), consts, w_out), (s_f, s_b)

            ctx_args, state = mixer(x_ctx, mod_c, (zero_st, zero_st), x_ctx.shape[1])
            lat_args, _ = mixer(x_lat, mod, state, GRID_W)
        else:
            w = od_w_in[i]
            width = od_conv_w.shape[-1]
            hd = width // LRU_HEADS
            w_in = jnp.concatenate([w[:, width:], w[:, :width]], axis=1).astype(_MXU)
            w_out = od_w_out[i].astype(_MXU)
            conv_w = _pad_to(od_conv_w[i], 0, SUBLANES)
            conv_b = od_conv_b[i].reshape(1, width)
            wgate = jnp.concatenate([od_gate_a_w[i], od_gate_x_w[i]], axis=-1).astype(_MXU)
            ba = od_gate_a_b[i].reshape(2, 1, width)
            bx = od_gate_x_b[i].reshape(2, 1, width)
            sp = (-LRU_C * jax.nn.softplus(-od_lambda[i])).reshape(2, 1, width)
            sgb = jnp.broadcast_to(od_sg_b[i][:, :, None], (SG_GROUPS, SG_CHUNK, width // SG_GROUPS))
            consts = (od_sg_norm_g[i].reshape(1, width), od_sg_w[i].astype(_MXU), sgb)
            zero_st = jnp.zeros((bsz, 1, width), _F32)

            def mixer(xs, m, s0, row_len):
                sh, sc = vecs(m[:2])
                (z,) = _pre(xs, norm_mix_g[l] * (1.0 + sc), sh, w_in)
                h_f, h_b, c_f, c_b = _lru(z, 3, conv_w, conv_b, wgate, ba, bx, sp, s0[0], s0[1])
                return (_post_odd_kernel, z, 3 * width, (h_f, h_b), consts, w_out), (c_f, c_b)

            ctx_args, state = mixer(x_ctx, mod_c, (zero_st, zero_st), None)
            lat_args, _ = mixer(x_lat, mod, state, None)

        def post(args, xs, m):
            gate1, sh, sc, gate2 = vecs(m[2:])
            xs, h, neg_aff = _post(*args, xs, gate1, norm_ffn_g[l] * (1.0 + sc), sh, wr, ne)
            return (xs, gate2) + _route(h, neg_aff[..., :ne])

        x_lat, gate_lat, xg_lat, g_lat, idx_lat = post(lat_args, x_lat, mod)
        weights = (moe_w_gate, moe_w_up, moe_w_down, l)
        if last:
            (y_lat,) = _moe_ffn(xg_lat, g_lat, None, *weights)
        else:
            x_ctx, gate_ctx, xg_ctx, g_ctx, idx_ctx = post(ctx_args, x_ctx, mod_c)
            y_lat, y_ctx = _moe_ffn(xg_lat, g_lat, (xg_ctx, g_ctx), *weights)
            x_ctx = _unroute(y_ctx, idx_ctx, x_ctx, gate_ctx, None)
        x_lat = _unroute(y_lat, idx_lat, x_lat, gate_lat, norm_final_g if last else None)
    return x_lat
```

```python
import functools

import jax
import jax.numpy as jnp
from jax import lax
from jax.experimental import pallas as pl
from jax.experimental.pallas import tpu as pltpu

_MXU = jnp.bfloat16
_ACT = jnp.bfloat16
_F32 = jnp.float32
NORM_EPS = 1e-6
N_MOD = 6
GRID_W = 64
GLA_HEADS = 4
GLA_RANK = 16
GLA_TAU = 16.0
GLA_CHUNK = 64
LRU_HEADS = 4
LRU_C = 8.0
SG_GROUPS = 4
SG_CHUNK = 128
EC_CAPACITY = 2
LANES = 128
SUBLANES = 8
HALO = 16
VMEM_LIMIT = 48 * 1024 * 1024
MOE_VMEM_LIMIT = 56 * 1024 * 1024
MOE_ROWS = 512
SHORT_SEQ = 256
PROJ_ROWS = 512
LRU_ROWS = 1024


def _cparams(*sem):
    return pltpu.CompilerParams(dimension_semantics=sem, vmem_limit_bytes=VMEM_LIMIT)


def _row_tile(t, cap=256):
    tm = min(t, cap)
    assert t % tm == 0
    return tm


def _dot(a, b):
    return jnp.dot(a.astype(_MXU), b.astype(_MXU), preferred_element_type=_F32)


def _gelu(x):
    return 0.5 * x * (1.0 + jnp.tanh(0.7978845608028654 * (x + 0.044715 * (x * x * x))))


def _sigmoid(x):
    return 0.5 * jnp.tanh(0.5 * x) + 0.5


def _silu(x):
    return x * _sigmoid(x)


def _log_sigmoid(x):
    return jnp.minimum(x, 0.0) - jnp.log(1.0 + jnp.exp(-jnp.abs(x)))


def _norm_mod(x, a, s):
    ms = jnp.mean(x * x, axis=-1, keepdims=True)
    return x * lax.rsqrt(ms + NORM_EPS) * a + s


def _adaln_kernel(c_ref, w_ref, b_ref, o_ref):
    o_ref[...] = _dot(_silu(c_ref[...]), w_ref[...]) + b_ref[...]


def _adaln(cond, w, b):
    depth, d, n = w.shape
    r = cond.shape[0]
    tn = n // N_MOD
    return pl.pallas_call(
        _adaln_kernel,
        grid=(depth, N_MOD),
        in_specs=[
            pl.BlockSpec((r, d), lambda l, j: (0, 0)),
            pl.BlockSpec((None, d, tn), lambda l, j: (l, 0, j)),
            pl.BlockSpec((None, 1, tn), lambda l, j: (l, 0, j)),
        ],
        out_specs=pl.BlockSpec((None, r, tn), lambda l, j: (l, 0, j)),
        out_shape=jax.ShapeDtypeStruct((depth, r, n), _F32),
        compiler_params=_cparams("parallel", "parallel"),
        name="adaln",
    )(cond, w, b)


def _pre_kernel(x_ref, a_ref, s_ref, w_ref, o_ref):
    h = _norm_mod(x_ref[...], a_ref[...], s_ref[...])
    o_ref[...] = _dot(h, w_ref[...]).astype(o_ref.dtype)


def _pre_even_kernel(x_ref, a_ref, s_ref, w_ref, wlr_ref, up_ref, db_ref, o_ref, la_ref):
    h = _norm_mod(x_ref[...], a_ref[...], s_ref[...]).astype(_MXU)
    o_ref[...] = _dot(h, w_ref[...]).astype(o_ref.dtype)
    lr = _dot(h, wlr_ref[...])
    la_ref[...] = _log_sigmoid(_dot(lr, up_ref[...]) + db_ref[...]) * (1.0 / GLA_TAU)


def _pre(x, a, s, w, decay=None):
    bsz, t, d = x.shape
    n = w.shape[1]
    tm = _row_tile(t, PROJ_ROWS)
    row = lambda b, i: (b, i, 0)
    vec = pl.BlockSpec((None, 1, d), lambda b, i: (b, 0, 0))
    full = lambda arr: pl.BlockSpec(arr.shape, lambda b, i: (0,) * arr.ndim)
    in_specs = [pl.BlockSpec((None, tm, d), row), vec, vec, full(w)]
    out_specs = [pl.BlockSpec((None, tm, n), row)]
    out_shape = [jax.ShapeDtypeStruct((bsz, t, n), _ACT)]
    args = [x, a, s, w]
    body = _pre_kernel
    if decay is not None:
        body = _pre_even_kernel
        n_la = decay[1].shape[1]
        in_specs += [full(v) for v in decay]
        out_specs.append(pl.BlockSpec((None, tm, n_la), row))
        out_shape.append(jax.ShapeDtypeStruct((bsz, t, n_la), _F32))
        args += list(decay)
    return pl.pallas_call(
        body, grid=(bsz, t // tm), in_specs=in_specs, out_specs=out_specs, out_shape=out_shape,
        compiler_params=_cparams("parallel", "parallel"), name="pre_proj",
    )(*args)


def _bmm(a, b, dims):
    return lax.dot_general(a.astype(_MXU), b.astype(_MXU), (dims, ((0,), (0,))), preferred_element_type=_F32)


def _cumsum_chunks(tri, x):
    hi = x.astype(_MXU)
    r1 = x - hi.astype(_F32)
    mid = r1.astype(_MXU)
    lo = (r1 - mid.astype(_F32)).astype(_MXU)
    mm = ((2,), (1,))
    return _bmm(tri, hi, mm) + _bmm(tri, mid, mm) + _bmm(tri, lo, mm)


def _gla_block(qkv_ref, la_ref, o_ref, st_ref, sin_ref, keep, scale, reverse):
    c = GLA_CHUNK
    tb = qkv_ref.shape[0]
    n = tb // c
    key = la_ref.shape[-1]
    dv = (qkv_ref.shape[-1] - 2 * key) // GLA_HEADS
    dk = key // GLA_HEADS
    tri = jnp.broadcast_to(keep.astype(_MXU)[None], (n, c, c))
    b = _cumsum_chunks(tri, la_ref[...].reshape(n, c, key))
    btot = b[:, 0:1] if reverse else b[:, c - 1:c]
    q = qkv_ref[:, 0:key].astype(_F32).reshape(n, c, key)
    k = qkv_ref[:, key:2 * key].astype(_F32).reshape(n, c, key)
    qd = q * (scale * jnp.exp(b))
    kd = k * jnp.exp(-b)
    kr = k * jnp.exp(btot - b)
    dec = jnp.exp(btot)
    pair = 2 * dk
    lane_lo = lax.broadcasted_iota(jnp.int32, (n, c, pair), 2) < dk
    order = range(n - 1, -1, -1) if reverse else range(n)
    for p in range(GLA_HEADS // 2):
        lanes = slice(p * pair, (p + 1) * pair)
        qd_p, kd_p, kr_p = qd[:, :, lanes], kd[:, :, lanes], kr[:, :, lanes]
        vs = [qkv_ref[:, 2 * key + h * dv:2 * key + (h + 1) * dv].reshape(n, c, dv) for h in (2 * p, 2 * p + 1)]
        km = jnp.concatenate([jnp.where(lane_lo, kr_p, 0.0), jnp.where(lane_lo, 0.0, kr_p)], axis=1)
        upd = _bmm(jnp.concatenate(vs, axis=1), km, ((1,), (1,)))
        st = st_ref[p]
        for j in order:
            sin_ref[j] = st
            st = dec[j, :, lanes] * st + upd[j]
        st_ref[p] = st
        s_in = sin_ref[...]
        for s in range(2):
            h = 2 * p + s
            qm = jnp.where(lane_lo if s == 0 else jnp.logical_not(lane_lo), qd_p, 0.0)
            att = jnp.where(keep[None], _bmm(qm, kd_p, ((2,), (2,))), 0.0)
            o = _bmm(att, vs[s], ((2,), (1,))) + _bmm(qm, s_in, ((2,), (2,)))
            o_ref[:, h * dv:(h + 1) * dv] = o.reshape(tb, dv).astype(o_ref.dtype)


def _gla_kernel(qkv_f_ref, la_f_ref, qkv_b_ref, la_b_ref, s0f_ref, s0b_ref, of_ref, ob_ref, sf_ref, sb_ref, sin_ref, *,
                scale):
    i = pl.program_id(1)

    @pl.when(i == 0)
    def _():
        sf_ref[...] = s0f_ref[...]
        sb_ref[...] = s0b_ref[...]

    c = GLA_CHUNK
    ri = lax.broadcasted_iota(jnp.int32, (c, c), 0)
    ci = lax.broadcasted_iota(jnp.int32, (c, c), 1)
    _gla_block(qkv_f_ref, la_f_ref, of_ref, sf_ref, sin_ref, ri >= ci, scale, False)
    _gla_block(qkv_b_ref, la_b_ref, ob_ref, sb_ref, sin_ref, ri <= ci, scale, True)


def _gla(z, la, s0f, s0b, qkv_block, scale):
    bsz, t, _ = z.shape
    key = la.shape[-1] // 2
    st_shape = s0f.shape[1:]
    val = GLA_HEADS * st_shape[1]
    width = 2 * key + val
    tb = _row_tile(t, 512)
    nt = t // tb
    st_spec = pl.BlockSpec((None,) + st_shape, lambda b, i: (b, 0, 0, 0))
    outs = pl.pallas_call(
        functools.partial(_gla_kernel, scale=scale),
        grid=(bsz, nt),
        in_specs=[
            pl.BlockSpec((None, tb, width), lambda b, i: (b, i, qkv_block)),
            pl.BlockSpec((None, tb, key), lambda b, i: (b, i, 0)),
            pl.BlockSpec((None, tb, width), lambda b, i: (b, nt - 1 - i, qkv_block)),
            pl.BlockSpec((None, tb, key), lambda b, i: (b, nt - 1 - i, 1)),
            st_spec, st_spec,
        ],
        out_specs=[
            pl.BlockSpec((None, tb, val), lambda b, i: (b, i, 0)),
            pl.BlockSpec((None, tb, val), lambda b, i: (b, nt - 1 - i, 0)),
            st_spec, st_spec,
        ],
        out_shape=[
            jax.ShapeDtypeStruct((bsz, t, val), _ACT),
            jax.ShapeDtypeStruct((bsz, t, val), _ACT),
            jax.ShapeDtypeStruct(s0f.shape, _F32),
            jax.ShapeDtypeStruct(s0b.shape, _F32),
        ],
        scratch_shapes=[pltpu.VMEM((tb // GLA_CHUNK,) + st_shape[1:], _F32)],
        compiler_params=_cparams("parallel", "arbitrary"),
        name="gla_scan",
    )(z, la, z, la, s0f, s0b)
    return outs


def _lru_coeffs(x_ref, prev_ref, next_ref, first, last, cw_ref, cb_ref, wg_ref, ba_ref, bx_ref, sp_ref, pad_scr):
    tm, hd = x_ref.shape
    steps = tm // SUBLANES
    pitch = pad_scr.shape[0] // SUBLANES
    for sg in range(SUBLANES):
        pad_scr[sg * pitch:sg * pitch + steps, :] = x_ref[sg * steps:(sg + 1) * steps, :].astype(_F32)
    x = jnp.stack([pad_scr[pl.ds(s, SUBLANES, stride=pitch), :] for s in range(steps)], axis=0)
    prev = jnp.where(first, 0.0, prev_ref[...].astype(_F32)[HALO - SUBLANES:HALO])
    nxt = jnp.where(last, 0.0, next_ref[...].astype(_F32)[0:SUBLANES])
    seg = lax.broadcasted_iota(jnp.int32, (SUBLANES, 1), 0)

    def from_prev_segment(v, fill):
        return jnp.where(seg == 0, fill, pltpu.roll(v, 1, 0))

    def from_next_segment(v, fill):
        return jnp.where(seg == SUBLANES - 1, fill, pltpu.roll(v, SUBLANES - 1, 0))

    m1_edge = from_prev_segment(x[steps - 1], prev[SUBLANES - 1:SUBLANES])
    m2_edge = from_prev_segment(x[steps - 2], prev[SUBLANES - 2:SUBLANES - 1])
    p1_edge = from_next_segment(x[0], nxt[0:1])
    x_m1 = jnp.concatenate([m1_edge[None], x[:steps - 1]], axis=0)
    x_m2 = jnp.concatenate([m2_edge[None], m1_edge[None], x[:steps - 2]], axis=0)
    x_p1 = jnp.concatenate([x[1:], p1_edge[None]], axis=0)
    xc = cb_ref[...] + cw_ref[0:1] * x_m2 + cw_ref[1:2] * x_m1 + cw_ref[2:3] * x + cw_ref[3:4] * x_p1
    xc = xc.reshape(tm, hd)
    g = _dot(xc, wg_ref[...])
    tr = jnp.tanh(g[:, :hd] + ba_ref[...])
    ti = jnp.tanh(g[:, hd:] + bx_ref[...])
    th = jnp.tanh(sp_ref[...] * tr + sp_ref[...])
    rcp = 1.0 / (1.0 - th)
    a = (1.0 + th) * rcp
    bb = (jnp.sqrt(-th) * rcp) * ((ti + 1.0) * xc)
    shape = (steps, SUBLANES, hd)
    return a.reshape(shape), bb.reshape(shape)


def _lru_scan_tile(a, bb, h_scr, p_scr, c_scr, pad_scr, h_ref, carry_ref, reverse):
    steps = a.shape[0]
    pitch = pad_scr.shape[0] // SUBLANES
    order = range(steps - 1, -1, -1) if reverse else range(steps)
    h = None
    for s in order:
        h = bb[s] if h is None else a[s] * h + bb[s]
        prod = a[s] if s == order[0] else a[s] * prod
        h_scr[s] = h
        p_scr[s] = prod
    c = carry_ref[...]
    for sg in (range(SUBLANES - 1, -1, -1) if reverse else range(SUBLANES)):
        c_scr[sg:sg + 1, :] = c
        c = h[sg:sg + 1] + prod[sg:sg + 1] * c
    carry_ref[...] = c
    c_in = c_scr[...]
    for s in range(steps):
        pad_scr[pl.ds(s, SUBLANES, stride=pitch), :] = h_scr[s] + p_scr[s] * c_in
    for sg in range(SUBLANES):
        h_ref[sg * steps:(sg + 1) * steps, :] = pad_scr[sg * pitch:sg * pitch + steps, :].astype(h_ref.dtype)


def _lru_kernel(xf_ref, xf_prev_ref, xf_next_ref, xb_ref, xb_prev_ref, xb_next_ref, cw_ref, cb_ref, wg_ref, ba_ref,
                bx_ref, sp_ref, h0f_ref, h0b_ref, hf_ref, hb_ref, cf_ref, cbk_ref, a_scr, b_scr, c_scr, pad_scr):
    i = pl.program_id(2)
    nt = pl.num_programs(2)

    @pl.when(i == 0)
    def _():
        cf_ref[...] = h0f_ref[...]
        cbk_ref[...] = h0b_ref[...]

    a, bb = _lru_coeffs(xf_ref, xf_prev_ref, xf_next_ref, i == 0, i == nt - 1, cw_ref, cb_ref, wg_ref.at[0],
                        ba_ref.at[0], bx_ref.at[0], sp_ref.at[0], pad_scr)
    _lru_scan_tile(a, bb, a_scr, b_scr, c_scr, pad_scr, hf_ref, cf_ref, False)
    a, bb = _lru_coeffs(xb_ref, xb_prev_ref, xb_next_ref, i == nt - 1, i == 0, cw_ref, cb_ref, wg_ref.at[1],
                        ba_ref.at[1], bx_ref.at[1], sp_ref.at[1], pad_scr)
    _lru_scan_tile(a, bb, a_scr, b_scr, c_scr, pad_scr, hb_ref, cbk_ref, True)


def _lru(z, x_block, conv_w, conv_b, wg, ba, bx, sp, h0f, h0b):
    bsz, t, _ = z.shape
    width = conv_w.shape[1]
    hd = width // LRU_HEADS
    assert hd == LANES
    tm = _row_tile(t, LRU_ROWS)
    nt = t // tm
    gpt = tm // SUBLANES
    hpt = tm // HALO
    nhb = t // HALO
    pitch = (gpt + SUBLANES - 1) // SUBLANES * SUBLANES
    pitch += SUBLANES * (1 - (pitch // SUBLANES) % 2)
    col = lambda h: x_block * LRU_HEADS + h

    def tile(rev):
        return lambda b, h, i: (b, (nt - 1 - i) if rev else i, col(h))

    def prev(rev):
        return lambda b, h, i: (b, jnp.maximum(((nt - 1 - i) if rev else i) * hpt - 1, 0), col(h))

    def nxt(rev):
        return lambda b, h, i: (b, jnp.minimum((((nt - 1 - i) if rev else i) + 1) * hpt, nhb - 1), col(h))

    lane = lambda rows: pl.BlockSpec((rows, hd), lambda b, h, i: (0, h))
    dirs = pl.BlockSpec((2, 1, hd), lambda b, h, i: (0, 0, h))
    st_spec = pl.BlockSpec((None, 1, hd), lambda b, h, i: (b, 0, h))
    in_specs = []
    for rev in (False, True):
        in_specs += [pl.BlockSpec((None, tm, hd), tile(rev)),
                     pl.BlockSpec((None, HALO, hd), prev(rev)),
                     pl.BlockSpec((None, HALO, hd), nxt(rev))]
    in_specs += [lane(conv_w.shape[0]), lane(1),
                 pl.BlockSpec((2, None, hd, 2 * hd), lambda b, h, i: (0, h, 0, 0)),
                 dirs, dirs, dirs, st_spec, st_spec]
    return pl.pallas_call(
        _lru_kernel,
        grid=(bsz, LRU_HEADS, nt),
        in_specs=in_specs,
        out_specs=[
            pl.BlockSpec((None, tm, hd), lambda b, h, i: (b, i, h)),
            pl.BlockSpec((None, tm, hd), lambda b, h, i: (b, nt - 1 - i, h)),
            st_spec, st_spec,
        ],
        out_shape=[
            jax.ShapeDtypeStruct((bsz, t, width), _ACT),
            jax.ShapeDtypeStruct((bsz, t, width), _ACT),
            jax.ShapeDtypeStruct((bsz, 1, width), _F32),
            jax.ShapeDtypeStruct((bsz, 1, width), _F32),
        ],
        scratch_shapes=[pltpu.VMEM((gpt, SUBLANES, hd), _F32), pltpu.VMEM((gpt, SUBLANES, hd), _F32),
                        pltpu.VMEM((SUBLANES, hd), _F32), pltpu.VMEM((SUBLANES * pitch, hd), _F32)],
        compiler_params=_cparams("parallel", "parallel", "arbitrary"),
        name="lru_scan",
    )(z, z, z, z, z, z, conv_w, conv_b, wg, ba, bx, sp, h0f, h0b)


def _residual_ffn_prenorm(y, x_ref, gate_ref, a_ref, s_ref, wr_ref, xo_ref, h_ref, na_ref, n_experts):
    xn = x_ref[...] + gate_ref[...] * y
    xo_ref[...] = xn
    h = _norm_mod(xn, a_ref[...], s_ref[...]).astype(_MXU)
    h_ref[...] = h
    logits = _dot(h, wr_ref[...])
    lane = lax.broadcasted_iota(jnp.int32, logits.shape, 1)
    logits = jnp.where(lane < n_experts, logits, -jnp.inf)
    e = jnp.exp(logits - jnp.max(logits, axis=-1, keepdims=True))
    na_ref[...] = -(e / jnp.sum(e, axis=-1, keepdims=True))


def _shift_rows(v, up):
    r, w = v.shape
    v3 = v.reshape(r // SUBLANES, SUBLANES, w)
    sub = lax.broadcasted_iota(jnp.int32, (1, SUBLANES, 1), 1)
    if up:
        rot = pltpu.roll(v3, SUBLANES - 1, 1)
        out = jnp.where(sub == SUBLANES - 1, jnp.concatenate([rot[1:], rot[-1:]], axis=0), rot)
    else:
        rot = pltpu.roll(v3, 1, 1)
        out = jnp.where(sub == 0, jnp.concatenate([rot[:1], rot[:-1]], axis=0), rot)
    return out.reshape(r, w)


def _post_even_kernel(za_ref, of_ref, ob_ref, cw_ref, cb_ref, ng_ref, w_ref, x_ref, gate_ref, a_ref, s_ref, wr_ref,
                      xo_ref, h_ref, na_ref, y_scr, *, row_len, n_experts):
    tm = za_ref.shape[0]
    scw = cw_ref.shape[1]
    val = ng_ref.shape[1]
    dv = val // GLA_HEADS
    bg = za_ref[:, 0:scw].astype(_F32)
    cx = za_ref[:, scw:2 * scw].astype(_F32) * za_ref[:, 2 * scw:3 * scw].astype(_F32)
    r = lax.broadcasted_iota(jnp.int32, (tm, 1), 0) % row_len
    left = jnp.where(r == 0, 0.0, _shift_rows(cx, False))
    right = jnp.where(r == row_len - 1, 0.0, _shift_rows(cx, True))
    conv = cb_ref[...] + cw_ref[0:1] * left + cw_ref[1:2] * cx + cw_ref[2:3] * right
    acc = _dot(bg * conv, w_ref[0:scw, :])
    for h in range(GLA_HEADS):
        lanes = slice(h * dv, (h + 1) * dv)
        o = of_ref[:, lanes].astype(_F32) + ob_ref[:, lanes].astype(_F32)
        o = o * lax.rsqrt(jnp.mean(o * o, axis=-1, keepdims=True) + NORM_EPS)
        gate = za_ref[:, 3 * scw + h * dv:3 * scw + (h + 1) * dv].astype(_F32)
        y_scr[:, lanes] = o * ng_ref[:, lanes] * _silu(gate)
    acc = acc + _dot(y_scr[...], w_ref[scw:scw + val, :])
    _residual_ffn_prenorm(acc, x_ref, gate_ref, a_ref, s_ref, wr_ref, xo_ref, h_ref, na_ref, n_experts)


def _post_odd_kernel(zc_ref, hf_ref, hb_ref, lng_ref, sgw_ref, sgb_ref, w_ref, x_ref, gate_ref, a_ref, s_ref, wr_ref,
                     xo_ref, h_ref, na_ref, y_scr, *, n_experts):
    tm = zc_ref.shape[0]
    width = hf_ref.shape[1]
    gdim = width // SG_GROUPS
    y_rec = (hf_ref[...].astype(_F32) + hb_ref[...].astype(_F32)) * _gelu(zc_ref[:, 0:width].astype(_F32))
    acc = _dot(y_rec, w_ref[0:width, :])
    gv = _gelu(zc_ref[:, 2 * width:3 * width].astype(_F32))
    gc = gv - jnp.mean(gv, axis=-1, keepdims=True)
    vb = gc * lax.rsqrt(jnp.mean(gc * gc, axis=-1, keepdims=True) + NORM_EPS) * lng_ref[...]
    for n in range(tm // SG_CHUNK):
        rows = slice(n * SG_CHUNK, (n + 1) * SG_CHUNK)
        for g in range(SG_GROUPS):
            lanes = slice(g * gdim, (g + 1) * gdim)
            mixed = _dot(sgw_ref[g], vb[rows, lanes]) + sgb_ref[g]
            u = zc_ref[rows, width + g * gdim:width + (g + 1) * gdim].astype(_F32)
            y_scr[rows, lanes] = _gelu(u) * mixed
    acc = acc + _dot(y_scr[...], w_ref[width:2 * width, :])
    _residual_ffn_prenorm(acc, x_ref, gate_ref, a_ref, s_ref, wr_ref, xo_ref, h_ref, na_ref, n_experts)


def _post(body, z, z_width, seq_in, consts, w_out, x, gate, a, s, wr, n_experts):
    bsz, t, d = x.shape
    tm = _row_tile(t, PROJ_ROWS)
    row = lambda b, i: (b, i, 0)
    vec = pl.BlockSpec((None, 1, d), lambda b, i: (b, 0, 0))
    full = lambda arr: pl.BlockSpec(arr.shape, lambda b, i: (0,) * arr.ndim)
    width = seq_in[0].shape[-1]
    in_specs = [pl.BlockSpec((None, tm, z_width), row)]
    in_specs += [pl.BlockSpec((None, tm, width), row) for _ in seq_in]
    in_specs += [full(v) for v in consts] + [full(w_out), pl.BlockSpec((None, tm, d), row), vec, vec, vec, full(wr)]
    return pl.pallas_call(
        functools.partial(body, n_experts=n_experts),
        grid=(bsz, t // tm),
        in_specs=in_specs,
        out_specs=[
            pl.BlockSpec((None, tm, d), row),
            pl.BlockSpec((None, tm, d), row),
            pl.BlockSpec((None, tm, LANES), row),
        ],
        out_shape=[
            jax.ShapeDtypeStruct((bsz, t, d), _F32),
            jax.ShapeDtypeStruct((bsz, t, d), _MXU),
            jax.ShapeDtypeStruct((bsz, t, LANES), _F32),
        ],
        scratch_shapes=[pltpu.VMEM((tm, width), _F32)],
        compiler_params=_cparams("parallel", "parallel"),
        name="post_proj",
    )(z, *seq_in, *consts, w_out, x, gate, a, s, wr)


def _moe_ffn_kernel(*refs, with_short):
    if with_short:
        x_ref, g_ref, xs_ref, gs_ref, wg_ref, wu_ref, wd_ref, o_ref, os_ref, wg_scr, wu_scr, wd_scr = refs
    else:
        x_ref, g_ref, wg_ref, wu_ref, wd_ref, o_ref, wg_scr, wu_scr, wd_scr = refs
    b = pl.program_id(1)

    @pl.when(b == 0)
    def _():
        wg_scr[...] = wg_ref[...].astype(_MXU)
        wu_scr[...] = wu_ref[...].astype(_MXU)
        wd_scr[...] = wd_ref[...].astype(_MXU)

    def ffn(xr, gr, outr):
        bb, cap, d = xr.shape
        x = xr[...].reshape(bb * cap, d)
        gate = _dot(x, wg_scr[...])
        up = _dot(x, wu_scr[...])
        out = _dot(_silu(gate) * up, wd_scr[...]) * gr[...].reshape(bb * cap, 1)
        outr[...] = out.reshape(bb, cap, d).astype(outr.dtype)

    if with_short:
        last = pl.num_programs(1) - 1
        pl.when(b < last)(lambda: ffn(x_ref, g_ref, o_ref))
        pl.when(b == last)(lambda: ffn(xs_ref, gs_ref, os_ref))
    else:
        ffn(x_ref, g_ref, o_ref)


def _moe_ffn(xg, g, short, wg, wu, wd, layer):
    bsz, ne, cap, d = xg.shape
    f = wg.shape[-1]
    bb = max(1, min(bsz, MOE_ROWS // cap))
    assert bsz % bb == 0
    nb = bsz // bb
    wsel = lambda e, b: (layer, e, 0, 0)
    w_specs = [pl.BlockSpec((None, None, d, f), wsel), pl.BlockSpec((None, None, d, f), wsel),
               pl.BlockSpec((None, None, f, d), wsel)]
    out_shape = [jax.ShapeDtypeStruct((bsz, ne, cap, d), _MXU)]
    if short is None:
        tok = lambda e, b: (b, e, 0, 0)
        steps = nb
        in_specs = [pl.BlockSpec((bb, None, cap, d), tok), pl.BlockSpec((bb, None, cap, 1), tok)] + w_specs
        out_specs = [pl.BlockSpec((bb, None, cap, d), tok)]
        args = (xg, g, wg, wu, wd)
    else:
        xs, gs = short
        bs, _, caps, _ = xs.shape
        assert bs * caps <= MOE_ROWS
        tok = lambda e, b: (jnp.minimum(b, nb - 1), e, 0, 0)
        tok_s = lambda e, b: (0, e, 0, 0)
        steps = nb + 1
        in_specs = [pl.BlockSpec((bb, None, cap, d), tok), pl.BlockSpec((bb, None, cap, 1), tok),
                    pl.BlockSpec((bs, None, caps, d), tok_s), pl.BlockSpec((bs, None, caps, 1), tok_s)] + w_specs
        out_specs = [pl.BlockSpec((bb, None, cap, d), tok), pl.BlockSpec((bs, None, caps, d), tok_s)]
        out_shape.append(jax.ShapeDtypeStruct((bs, ne, caps, d), _MXU))
        args = (xg, g, xs, gs, wg, wu, wd)
    return pl.pallas_call(
        functools.partial(_moe_ffn_kernel, with_short=short is not None),
        grid=(ne, steps),
        in_specs=in_specs,
        out_specs=out_specs,
        out_shape=out_shape,
        scratch_shapes=[pltpu.VMEM((d, f), _MXU), pltpu.VMEM((d, f), _MXU), pltpu.VMEM((f, d), _MXU)],
        compiler_params=pltpu.CompilerParams(dimension_semantics=("parallel", "arbitrary"),
                                             vmem_limit_bytes=MOE_VMEM_LIMIT),
        name="moe_ffn",
    )(*args)


def _dispatch_kernel(idx_ref, h_ref, o_ref):
    p, t = idx_ref.shape[0], h_ref.shape[0]
    onehot = jnp.where(lax.broadcasted_iota(jnp.int32, (p, t), 1) == idx_ref[...], 1.0, 0.0)
    o_ref[...] = _dot(onehot, h_ref[...]).astype(o_ref.dtype)


def _dispatch(h, idx):
    bsz, t, d = h.shape
    p = idx.shape[1]
    return pl.pallas_call(
        _dispatch_kernel,
        grid=(bsz,),
        in_specs=[pl.BlockSpec((None, p, 1), lambda b: (b, 0, 0)), pl.BlockSpec((None, t, d), lambda b: (b, 0, 0))],
        out_specs=pl.BlockSpec((None, p, d), lambda b: (b, 0, 0)),
        out_shape=jax.ShapeDtypeStruct((bsz, p, d), h.dtype),
        compiler_params=_cparams("parallel"),
        name="moe_dispatch",
    )(idx.reshape(bsz, p, 1), h)


def _combine_kernel(lo_ref, hi_ref, tok_ref, y_ref, x_ref, gate_ref, fg_ref, o_ref, acc_ref, *, final_norm):
    b = pl.program_id(0)
    j = pl.program_id(1)
    tm = acc_ref.shape[0]
    ch = tok_ref.shape[-1]
    acc_ref[...] = jnp.zeros_like(acc_ref)
    rows = j * tm + lax.broadcasted_iota(jnp.int32, (tm, ch), 0)

    def body(c, carry):
        onehot = jnp.where(rows == tok_ref[c], 1.0, 0.0)
        acc_ref[...] += _dot(onehot, y_ref[pl.ds(pl.multiple_of(c * ch, ch), ch), :])
        return carry

    lax.fori_loop(lo_ref[b, j], hi_ref[b, j] + 1, body, 0)
    xn = x_ref[...] + gate_ref[...] * acc_ref[...]
    if final_norm:
        ms = jnp.mean(xn * xn, axis=-1, keepdims=True)
        xn = xn * lax.rsqrt(ms + NORM_EPS) * fg_ref[...]
    o_ref[...] = xn


def _combine(ys, toks, x, gate, final_g):
    bsz, t, d = x.shape
    p = ys.shape[1]
    tm = _row_tile(t, PROJ_ROWS)
    ch = _row_tile(p)
    nt, nc = t // tm, p // ch
    bounds = jnp.arange(nt + 1, dtype=jnp.int32) * tm
    cut = jnp.sum((toks[:, :, None] < bounds[None, None, :]).astype(jnp.int32), axis=1)
    c_lo = jnp.minimum(cut[:, :-1] // ch, nc - 1)
    c_hi = jnp.maximum((cut[:, 1:] - 1) // ch, c_lo)
    fg = jnp.ones((1, d), _F32) if final_g is None else final_g.reshape(1, d)
    grid_spec = pltpu.PrefetchScalarGridSpec(
        num_scalar_prefetch=2,
        grid=(bsz, nt),
        in_specs=[
            pl.BlockSpec((None, nc, 1, ch), lambda b, j, lo, hi: (b, 0, 0, 0)),
            pl.BlockSpec((None, p, d), lambda b, j, lo, hi: (b, 0, 0)),
            pl.BlockSpec((None, tm, d), lambda b, j, lo, hi: (b, j, 0)),
            pl.BlockSpec((None, 1, d), lambda b, j, lo, hi: (b, 0, 0)),
            pl.BlockSpec((1, d), lambda b, j, lo, hi: (0, 0)),
        ],
        out_specs=pl.BlockSpec((None, tm, d), lambda b, j, lo, hi: (b, j, 0)),
        scratch_shapes=[pltpu.VMEM((tm, d), _F32)],
    )
    return pl.pallas_call(
        functools.partial(_combine_kernel, final_norm=final_g is not None),
        grid_spec=grid_spec,
        out_shape=jax.ShapeDtypeStruct((bsz, t, d), _F32),
        compiler_params=pltpu.CompilerParams(dimension_semantics=("parallel", "arbitrary"),
                                             vmem_limit_bytes=MOE_VMEM_LIMIT),
        name="moe_combine",
    )(c_lo, c_hi, toks.reshape(bsz, nc, 1, ch), ys, x, gate, fg)


def _route(h, neg_aff):
    bsz, t, d = h.shape
    ne = neg_aff.shape[-1]
    cap = EC_CAPACITY * t // ne
    keys = jnp.transpose(neg_aff, (1, 0, 2)).reshape(t, bsz * ne)
    token = lax.broadcasted_iota(jnp.int32, (t, bsz * ne), 0)
    keys, token = lax.sort((keys, token), dimension=0, num_keys=1, is_stable=True)
    g = jnp.transpose((-keys[:cap]).reshape(cap, bsz, ne), (1, 2, 0))
    idx = jnp.transpose(token[:cap].reshape(cap, bsz, ne), (1, 2, 0))
    if t <= SHORT_SEQ:
        xg = _dispatch(h, idx.reshape(bsz, ne * cap)).reshape(bsz, ne, cap, d)
    else:
        xg = jax.vmap(lambda hb, ib: hb[ib])(h, idx)
    return xg, g[..., None], idx


def _unroute(y, idx, x, gate, final_g):
    bsz, ne, cap, d = y.shape
    slots = jnp.broadcast_to(jnp.arange(ne * cap, dtype=jnp.int32), (bsz, ne * cap))
    toks, perm = lax.sort_key_val(idx.reshape(bsz, ne * cap).astype(jnp.int32), slots, dimension=-1)
    ys = jax.vmap(lambda yb, pb: yb[pb])(y.reshape(bsz, ne * cap, d), perm)
    return _combine(ys, toks, x, gate, final_g)


def _pad_to(w, axis, size):
    pad = [(0, 0)] * w.ndim
    pad[axis] = (0, size - w.shape[axis])
    return jnp.pad(w, pad)


def kernel(x, c, ctx, c_ctx, ada_w, ada_b, norm_mix_g, norm_ffn_g, norm_final_g, ev_w_in, ev_conv_w, ev_conv_b, ev_decay_up, ev_decay_b, ev_norm_g, ev_w_out, od_w_in, od_conv_w, od_conv_b, od_gate_a_w, od_gate_a_b, od_gate_x_w, od_gate_x_b, od_lambda, od_sg_norm_g, od_sg_w, od_sg_b, od_w_out, moe_router, moe_w_gate, moe_w_up, moe_w_down):
    depth = ada_w.shape[0]
    bsz, seq, d = x.shape
    ne = moe_router.shape[-1]
    x_lat, x_ctx = x, ctx
    cond = _pad_to(jnp.concatenate([c, c_ctx[None, :]], axis=0), 0, -(-(bsz + 1) // SUBLANES) * SUBLANES)
    ada = _adaln(cond, ada_w, ada_b.reshape(depth, 1, N_MOD * d))

    def vecs(m):
        return [v.reshape(bsz, 1, d) for v in m]

    for l in range(depth):
        last = l == depth - 1
        even = l % 2 == 0
        i = l // 2
        mod = [ada[l, :bsz, k * d:(k + 1) * d] for k in range(N_MOD)]
        mod_c = [jnp.broadcast_to(ada[l, bsz:bsz + 1, k * d:(k + 1) * d], (bsz, d)) for k in range(N_MOD)]
        wr = _pad_to(moe_router[l], 1, LANES).astype(_MXU)

        if even:
            w = ev_w_in[i]
            scw = ev_conv_w.shape[-1]
            key = ev_decay_up.shape[-1]
            val = ev_norm_g.shape[-1]
            o_q = 3 * scw
            o_g = o_q + key
            o_k = o_g + val
            o_v = o_k + key
            o_lr = o_v + val
            w_in = jnp.concatenate([w[:, :o_q], w[:, o_g:o_k], w[:, o_q:o_g], w[:, o_k:o_lr]], axis=1).astype(_MXU)
            w_lr = _pad_to(w[:, o_lr:], 1, LANES).astype(_MXU)
            up = jnp.zeros((LANES, 2 * key), _F32)
            up = up.at[:GLA_RANK, :key].set(ev_decay_up[i, 0]).at[GLA_RANK:2 * GLA_RANK, key:].set(ev_decay_up[i, 1])
            decay = (w_lr, up.astype(_MXU), ev_decay_b[i].reshape(1, 2 * key))
            w_out = ev_w_out[i].astype(_MXU)
            conv_w = _pad_to(ev_conv_w[i], 0, SUBLANES)
            consts = (conv_w, ev_conv_b[i].reshape(1, scw), ev_norm_g[i].reshape(1, val))
            epi_width = 3 * scw + val
            assert (3 * scw + val) % (2 * key + val) == 0
            qkv_block = epi_width // (2 * key + val)
            dk, dv = key // GLA_HEADS, val // GLA_HEADS
            scale = dk ** -0.5
            zero_st = jnp.zeros((bsz, GLA_HEADS // 2, dv, 2 * dk), _F32)

            def mixer(xs, m, s0, row_len):
                sh, sc = vecs(m[:2])
                z, la = _pre(xs, norm_mix_g[l] * (1.0 + sc), sh, w_in, decay)
                o_f, o_b, s_f, s_b = _gla(z, la, s0[0], s0[1], qkv_block, scale)
                body = functools.partial(_post_even_kernel, row_len=row_len)
                return (body, z, epi_width, (o_f, o_b), consts, w_out), (s_f, s_b)

            ctx_args, state = mixer(x_ctx, mod_c, (zero_st, zero_st), x_ctx.shape[1])
            lat_args, _ = mixer(x_lat, mod, state, GRID_W)
        else:
            w = od_w_in[i]
            width = od_conv_w.shape[-1]
            hd = width // LRU_HEADS
            w_in = jnp.concatenate([w[:, width:], w[:, :width]], axis=1).astype(_MXU)
            w_out = od_w_out[i].astype(_MXU)
            conv_w = _pad_to(od_conv_w[i], 0, SUBLANES)
            conv_b = od_conv_b[i].reshape(1, width)
            wgate = (0.5 * jnp.concatenate([od_gate_a_w[i], od_gate_x_w[i]], axis=-1)).astype(_MXU)
            ba = 0.5 * od_gate_a_b[i].reshape(2, 1, width)
            bx = 0.5 * od_gate_x_b[i].reshape(2, 1, width)
            sp = (-0.25 * LRU_C * jax.nn.softplus(-od_lambda[i])).reshape(2, 1, width)
            sgb = jnp.broadcast_to(od_sg_b[i][:, :, None], (SG_GROUPS, SG_CHUNK, width // SG_GROUPS))
            consts = (od_sg_norm_g[i].reshape(1, width), od_sg_w[i].astype(_MXU), sgb)
            zero_st = jnp.zeros((bsz, 1, width), _F32)

            def mixer(xs, m, s0, row_len):
                sh, sc = vecs(m[:2])
                (z,) = _pre(xs, norm_mix_g[l] * (1.0 + sc), sh, w_in)
                h_f, h_b, c_f, c_b = _lru(z, 3, conv_w, conv_b, wgate, ba, bx, sp, s0[0], s0[1])
                return (_post_odd_kernel, z, 3 * width, (h_f, h_b), consts, w_out), (c_f, c_b)

            ctx_args, state = mixer(x_ctx, mod_c, (zero_st, zero_st), None)
            lat_args, _ = mixer(x_lat, mod, state, None)

        def post(args, xs, m):
            gate1, sh, sc, gate2 = vecs(m[2:])
            xs, h, neg_aff = _post(*args, xs, gate1, norm_ffn_g[l] * (1.0 + sc), sh, wr, ne)
            return (xs, gate2) + _route(h, neg_aff[..., :ne])

        x_lat, gate_lat, xg_lat, g_lat, idx_lat = post(lat_args, x_lat, mod)
        weights = (moe_w_gate, moe_w_up, moe_w_down, l)
        if last:
            (y_lat,) = _moe_ffn(xg_lat, g_lat, None, *weights)
        else:
            x_ctx, gate_ctx, xg_ctx, g_ctx, idx_ctx = post(ctx_args, x_ctx, mod_c)
            y_lat, y_ctx = _moe_ffn(xg_lat, g_lat, (xg_ctx, g_ctx), *weights)
            x_ctx = _unroute(y_ctx, idx_ctx, x_ctx, gate_ctx, None)
        x_lat = _unroute(y_lat, idx_lat, x_lat, gate_lat, norm_final_g if last else None)
    return x_lat
```

```python
import functools

import jax
import jax.numpy as jnp
from jax import lax
from jax.experimental import pallas as pl
from jax.experimental.pallas import tpu as pltpu

_MXU = jnp.bfloat16
_ACT = jnp.bfloat16
_F32 = jnp.float32
NORM_EPS = 1e-6
N_MOD = 6
GRID_W = 64
GLA_HEADS = 4
GLA_RANK = 16
GLA_TAU = 16.0
GLA_CHUNK = 64
LRU_HEADS = 4
LRU_C = 8.0
SG_GROUPS = 4
SG_CHUNK = 128
EC_CAPACITY = 2
LANES = 128
SUBLANES = 8
HALO = 16
VMEM_LIMIT = 48 * 1024 * 1024
MOE_VMEM_LIMIT = 56 * 1024 * 1024
MOE_ROWS = 512
SHORT_SEQ = 256
PROJ_ROWS = 512
LRU_ROWS = 2048
GLA_ROWS = 1024


def _cparams(*sem):
    return pltpu.CompilerParams(dimension_semantics=sem, vmem_limit_bytes=VMEM_LIMIT)


def _row_tile(t, cap=256):
    tm = min(t, cap)
    assert t % tm == 0
    return tm


def _dot(a, b):
    return jnp.dot(a.astype(_MXU), b.astype(_MXU), preferred_element_type=_F32)


def _gelu(x):
    return 0.5 * x * (1.0 + jnp.tanh(0.7978845608028654 * (x + 0.044715 * (x * x * x))))


def _sigmoid(x):
    return 0.5 * jnp.tanh(0.5 * x) + 0.5


def _silu(x):
    return x * _sigmoid(x)


def _log_sigmoid(x):
    return jnp.minimum(x, 0.0) - jnp.log(1.0 + jnp.exp(-jnp.abs(x)))


def _norm_mod(x, a, s):
    ms = jnp.mean(x * x, axis=-1, keepdims=True)
    return x * lax.rsqrt(ms + NORM_EPS) * a + s


def _adaln_kernel(c_ref, w_ref, b_ref, o_ref):
    o_ref[...] = _dot(_silu(c_ref[...]), w_ref[...]) + b_ref[...]


def _adaln(cond, w, b):
    depth, d, n = w.shape
    r = cond.shape[0]
    tn = n // N_MOD
    return pl.pallas_call(
        _adaln_kernel,
        grid=(depth, N_MOD),
        in_specs=[
            pl.BlockSpec((r, d), lambda l, j: (0, 0)),
            pl.BlockSpec((None, d, tn), lambda l, j: (l, 0, j)),
            pl.BlockSpec((None, 1, tn), lambda l, j: (l, 0, j)),
        ],
        out_specs=pl.BlockSpec((None, r, tn), lambda l, j: (l, 0, j)),
        out_shape=jax.ShapeDtypeStruct((depth, r, n), _F32),
        compiler_params=_cparams("parallel", "parallel"),
        name="adaln",
    )(cond, w, b)


def _pre_kernel(x_ref, a_ref, s_ref, w_ref, o_ref):
    h = _norm_mod(x_ref[...], a_ref[...], s_ref[...])
    o_ref[...] = _dot(h, w_ref[...]).astype(o_ref.dtype)


def _pre_even_kernel(x_ref, a_ref, s_ref, w_ref, wlr_ref, up_ref, db_ref, o_ref, la_ref):
    h = _norm_mod(x_ref[...], a_ref[...], s_ref[...]).astype(_MXU)
    o_ref[...] = _dot(h, w_ref[...]).astype(o_ref.dtype)
    lr = _dot(h, wlr_ref[...])
    la_ref[...] = _log_sigmoid(_dot(lr, up_ref[...]) + db_ref[...]) * (1.0 / GLA_TAU)


def _pre(x, a, s, w, decay=None):
    bsz, t, d = x.shape
    n = w.shape[1]
    tm = _row_tile(t, PROJ_ROWS)
    row = lambda b, i: (b, i, 0)
    vec = pl.BlockSpec((None, 1, d), lambda b, i: (b, 0, 0))
    full = lambda arr: pl.BlockSpec(arr.shape, lambda b, i: (0,) * arr.ndim)
    in_specs = [pl.BlockSpec((None, tm, d), row), vec, vec, full(w)]
    out_specs = [pl.BlockSpec((None, tm, n), row)]
    out_shape = [jax.ShapeDtypeStruct((bsz, t, n), _ACT)]
    args = [x, a, s, w]
    body = _pre_kernel
    if decay is not None:
        body = _pre_even_kernel
        n_la = decay[1].shape[1]
        in_specs += [full(v) for v in decay]
        out_specs.append(pl.BlockSpec((None, tm, n_la), row))
        out_shape.append(jax.ShapeDtypeStruct((bsz, t, n_la), _F32))
        args += list(decay)
    return pl.pallas_call(
        body, grid=(bsz, t // tm), in_specs=in_specs, out_specs=out_specs, out_shape=out_shape,
        compiler_params=_cparams("parallel", "parallel"), name="pre_proj",
    )(*args)


def _bmm(a, b, dims):
    return lax.dot_general(a.astype(_MXU), b.astype(_MXU), (dims, ((0,), (0,))), preferred_element_type=_F32)


def _cumsum_chunks(tri, x):
    hi = x.astype(_MXU)
    r1 = x - hi.astype(_F32)
    mid = r1.astype(_MXU)
    lo = (r1 - mid.astype(_F32)).astype(_MXU)
    mm = ((2,), (1,))
    return _bmm(tri, hi, mm) + _bmm(tri, mid, mm) + _bmm(tri, lo, mm)


def _gla_block(qkv_ref, la_ref, o_ref, st_ref, sin_ref, keep, scale, reverse):
    c = GLA_CHUNK
    tb = qkv_ref.shape[0]
    n = tb // c
    key = la_ref.shape[-1]
    dv = (qkv_ref.shape[-1] - 2 * key) // GLA_HEADS
    dk = key // GLA_HEADS
    tri = jnp.broadcast_to(keep.astype(_MXU)[None], (n, c, c))
    b = _cumsum_chunks(tri, la_ref[...].reshape(n, c, key))
    btot = b[:, 0:1] if reverse else b[:, c - 1:c]
    q = qkv_ref[:, 0:key].astype(_F32).reshape(n, c, key)
    k = qkv_ref[:, key:2 * key].astype(_F32).reshape(n, c, key)
    qd = q * (scale * jnp.exp(b))
    kd = k * jnp.exp(-b)
    kr = k * jnp.exp(btot - b)
    dec = jnp.exp(btot)
    pair = 2 * dk
    lane_lo = lax.broadcasted_iota(jnp.int32, (n, c, pair), 2) < dk
    order = range(n - 1, -1, -1) if reverse else range(n)
    for p in range(GLA_HEADS // 2):
        lanes = slice(p * pair, (p + 1) * pair)
        qd_p, kd_p, kr_p = qd[:, :, lanes], kd[:, :, lanes], kr[:, :, lanes]
        vs = [qkv_ref[:, 2 * key + h * dv:2 * key + (h + 1) * dv].reshape(n, c, dv) for h in (2 * p, 2 * p + 1)]
        km = jnp.concatenate([jnp.where(lane_lo, kr_p, 0.0), jnp.where(lane_lo, 0.0, kr_p)], axis=1)
        upd = _bmm(jnp.concatenate(vs, axis=1), km, ((1,), (1,)))
        st = st_ref[p]
        for j in order:
            sin_ref[j] = st
            st = dec[j, :, lanes] * st + upd[j]
        st_ref[p] = st
        s_in = sin_ref[...]
        for s in range(2):
            h = 2 * p + s
            qm = jnp.where(lane_lo if s == 0 else jnp.logical_not(lane_lo), qd_p, 0.0)
            att = jnp.where(keep[None], _bmm(qm, kd_p, ((2,), (2,))), 0.0)
            o = _bmm(att, vs[s], ((2,), (1,))) + _bmm(qm, s_in, ((2,), (2,)))
            o_ref[:, h * dv:(h + 1) * dv] = o.reshape(tb, dv).astype(o_ref.dtype)


def _gla_kernel(qkv_f_ref, la_f_ref, qkv_b_ref, la_b_ref, s0f_ref, s0b_ref, of_ref, ob_ref, sf_ref, sb_ref, sin_ref, *,
                scale):
    i = pl.program_id(1)

    @pl.when(i == 0)
    def _():
        sf_ref[...] = s0f_ref[...]
        sb_ref[...] = s0b_ref[...]

    c = GLA_CHUNK
    ri = lax.broadcasted_iota(jnp.int32, (c, c), 0)
    ci = lax.broadcasted_iota(jnp.int32, (c, c), 1)
    _gla_block(qkv_f_ref, la_f_ref, of_ref, sf_ref, sin_ref, ri >= ci, scale, False)
    _gla_block(qkv_b_ref, la_b_ref, ob_ref, sb_ref, sin_ref, ri <= ci, scale, True)


def _gla(z, la, s0f, s0b, qkv_block, scale):
    bsz, t, _ = z.shape
    key = la.shape[-1] // 2
    st_shape = s0f.shape[1:]
    val = GLA_HEADS * st_shape[1]
    width = 2 * key + val
    tb = _row_tile(t, GLA_ROWS)
    nt = t // tb
    st_spec = pl.BlockSpec((None,) + st_shape, lambda b, i: (b, 0, 0, 0))
    outs = pl.pallas_call(
        functools.partial(_gla_kernel, scale=scale),
        grid=(bsz, nt),
        in_specs=[
            pl.BlockSpec((None, tb, width), lambda b, i: (b, i, qkv_block)),
            pl.BlockSpec((None, tb, key), lambda b, i: (b, i, 0)),
            pl.BlockSpec((None, tb, width), lambda b, i: (b, nt - 1 - i, qkv_block)),
            pl.BlockSpec((None, tb, key), lambda b, i: (b, nt - 1 - i, 1)),
            st_spec, st_spec,
        ],
        out_specs=[
            pl.BlockSpec((None, tb, val), lambda b, i: (b, i, 0)),
            pl.BlockSpec((None, tb, val), lambda b, i: (b, nt - 1 - i, 0)),
            st_spec, st_spec,
        ],
        out_shape=[
            jax.ShapeDtypeStruct((bsz, t, val), _ACT),
            jax.ShapeDtypeStruct((bsz, t, val), _ACT),
            jax.ShapeDtypeStruct(s0f.shape, _F32),
            jax.ShapeDtypeStruct(s0b.shape, _F32),
        ],
        scratch_shapes=[pltpu.VMEM((tb // GLA_CHUNK,) + st_shape[1:], _F32)],
        compiler_params=_cparams("parallel", "arbitrary"),
        name="gla_scan",
    )(z, la, z, la, s0f, s0b)
    return outs


def _lru_coeffs(x_ref, prev_ref, next_ref, first, last, cw_ref, cb_ref, wg_ref, ba_ref, bx_ref, sp_ref, pad_scr):
    tm, hd = x_ref.shape
    steps = tm // SUBLANES
    pitch = pad_scr.shape[0] // SUBLANES
    for sg in range(SUBLANES):
        pad_scr[sg * pitch:sg * pitch + steps, :] = x_ref[sg * steps:(sg + 1) * steps, :].astype(_F32)
    x = jnp.stack([pad_scr[pl.ds(s, SUBLANES, stride=pitch), :] for s in range(steps)], axis=0)
    prev = jnp.where(first, 0.0, prev_ref[...].astype(_F32)[HALO - SUBLANES:HALO])
    nxt = jnp.where(last, 0.0, next_ref[...].astype(_F32)[0:SUBLANES])
    seg = lax.broadcasted_iota(jnp.int32, (SUBLANES, 1), 0)

    def from_prev_segment(v, fill):
        return jnp.where(seg == 0, fill, pltpu.roll(v, 1, 0))

    def from_next_segment(v, fill):
        return jnp.where(seg == SUBLANES - 1, fill, pltpu.roll(v, SUBLANES - 1, 0))

    m1_edge = from_prev_segment(x[steps - 1], prev[SUBLANES - 1:SUBLANES])
    m2_edge = from_prev_segment(x[steps - 2], prev[SUBLANES - 2:SUBLANES - 1])
    p1_edge = from_next_segment(x[0], nxt[0:1])
    x_m1 = jnp.concatenate([m1_edge[None], x[:steps - 1]], axis=0)
    x_m2 = jnp.concatenate([m2_edge[None], m1_edge[None], x[:steps - 2]], axis=0)
    x_p1 = jnp.concatenate([x[1:], p1_edge[None]], axis=0)
    xc = cb_ref[...] + cw_ref[0:1] * x_m2 + cw_ref[1:2] * x_m1 + cw_ref[2:3] * x + cw_ref[3:4] * x_p1
    xc = xc.reshape(tm, hd)
    g = _dot(xc, wg_ref[...])
    tr = jnp.tanh(g[:, :hd] + ba_ref[...])
    ti = jnp.tanh(g[:, hd:] + bx_ref[...])
    th = jnp.tanh(sp_ref[...] * tr + sp_ref[...])
    rcp = 1.0 / (1.0 - th)
    a = (1.0 + th) * rcp
    bb = (jnp.sqrt(-th) * rcp) * ((ti + 1.0) * xc)
    shape = (steps, SUBLANES, hd)
    return a.reshape(shape), bb.reshape(shape)


def _lru_scan_tile(a, bb, h_scr, p_scr, c_scr, pad_scr, h_ref, carry_ref, reverse):
    steps = a.shape[0]
    pitch = pad_scr.shape[0] // SUBLANES
    order = range(steps - 1, -1, -1) if reverse else range(steps)
    h = None
    for s in order:
        h = bb[s] if h is None else a[s] * h + bb[s]
        prod = a[s] if s == order[0] else a[s] * prod
        h_scr[s] = h
        p_scr[s] = prod
    c = carry_ref[...]
    for sg in (range(SUBLANES - 1, -1, -1) if reverse else range(SUBLANES)):
        c_scr[sg:sg + 1, :] = c
        c = h[sg:sg + 1] + prod[sg:sg + 1] * c
    carry_ref[...] = c
    c_in = c_scr[...]
    for s in range(steps):
        pad_scr[pl.ds(s, SUBLANES, stride=pitch), :] = h_scr[s] + p_scr[s] * c_in
    for sg in range(SUBLANES):
        h_ref[sg * steps:(sg + 1) * steps, :] = pad_scr[sg * pitch:sg * pitch + steps, :].astype(h_ref.dtype)


def _lru_kernel(xf_ref, xf_prev_ref, xf_next_ref, xb_ref, xb_prev_ref, xb_next_ref, cw_ref, cb_ref, wg_ref, ba_ref,
                bx_ref, sp_ref, h0f_ref, h0b_ref, hf_ref, hb_ref, cf_ref, cbk_ref, a_scr, b_scr, c_scr, pad_scr):
    i = pl.program_id(2)
    nt = pl.num_programs(2)

    @pl.when(i == 0)
    def _():
        cf_ref[...] = h0f_ref[...]
        cbk_ref[...] = h0b_ref[...]

    a, bb = _lru_coeffs(xf_ref, xf_prev_ref, xf_next_ref, i == 0, i == nt - 1, cw_ref, cb_ref, wg_ref.at[0],
                        ba_ref.at[0], bx_ref.at[0], sp_ref.at[0], pad_scr)
    _lru_scan_tile(a, bb, a_scr, b_scr, c_scr, pad_scr, hf_ref, cf_ref, False)
    a, bb = _lru_coeffs(xb_ref, xb_prev_ref, xb_next_ref, i == nt - 1, i == 0, cw_ref, cb_ref, wg_ref.at[1],
                        ba_ref.at[1], bx_ref.at[1], sp_ref.at[1], pad_scr)
    _lru_scan_tile(a, bb, a_scr, b_scr, c_scr, pad_scr, hb_ref, cbk_ref, True)


def _lru(z, x_block, conv_w, conv_b, wg, ba, bx, sp, h0f, h0b):
    bsz, t, _ = z.shape
    width = conv_w.shape[1]
    hd = width // LRU_HEADS
    assert hd == LANES
    tm = _row_tile(t, LRU_ROWS)
    nt = t // tm
    gpt = tm // SUBLANES
    hpt = tm // HALO
    nhb = t // HALO
    pitch = (gpt + SUBLANES - 1) // SUBLANES * SUBLANES
    pitch += SUBLANES * (1 - (pitch // SUBLANES) % 2)
    col = lambda h: x_block * LRU_HEADS + h

    def tile(rev):
        return lambda b, h, i: (b, (nt - 1 - i) if rev else i, col(h))

    def prev(rev):
        return lambda b, h, i: (b, jnp.maximum(((nt - 1 - i) if rev else i) * hpt - 1, 0), col(h))

    def nxt(rev):
        return lambda b, h, i: (b, jnp.minimum((((nt - 1 - i) if rev else i) + 1) * hpt, nhb - 1), col(h))

    lane = lambda rows: pl.BlockSpec((rows, hd), lambda b, h, i: (0, h))
    dirs = pl.BlockSpec((2, 1, hd), lambda b, h, i: (0, 0, h))
    st_spec = pl.BlockSpec((None, 1, hd), lambda b, h, i: (b, 0, h))
    in_specs = []
    for rev in (False, True):
        in_specs += [pl.BlockSpec((None, tm, hd), tile(rev)),
                     pl.BlockSpec((None, HALO, hd), prev(rev)),
                     pl.BlockSpec((None, HALO, hd), nxt(rev))]
    in_specs += [lane(conv_w.shape[0]), lane(1),
                 pl.BlockSpec((2, None, hd, 2 * hd), lambda b, h, i: (0, h, 0, 0)),
                 dirs, dirs, dirs, st_spec, st_spec]
    return pl.pallas_call(
        _lru_kernel,
        grid=(bsz, LRU_HEADS, nt),
        in_specs=in_specs,
        out_specs=[
            pl.BlockSpec((None, tm, hd), lambda b, h, i: (b, i, h)),
            pl.BlockSpec((None, tm, hd), lambda b, h, i: (b, nt - 1 - i, h)),
            st_spec, st_spec,
        ],
        out_shape=[
            jax.ShapeDtypeStruct((bsz, t, width), _ACT),
            jax.ShapeDtypeStruct((bsz, t, width), _ACT),
            jax.ShapeDtypeStruct((bsz, 1, width), _F32),
            jax.ShapeDtypeStruct((bsz, 1, width), _F32),
        ],
        scratch_shapes=[pltpu.VMEM((gpt, SUBLANES, hd), _F32), pltpu.VMEM((gpt, SUBLANES, hd), _F32),
                        pltpu.VMEM((SUBLANES, hd), _F32), pltpu.VMEM((SUBLANES * pitch, hd), _F32)],
        compiler_params=_cparams("parallel", "parallel", "arbitrary"),
        name="lru_scan",
    )(z, z, z, z, z, z, conv_w, conv_b, wg, ba, bx, sp, h0f, h0b)


def _residual_ffn_prenorm(y, x_ref, gate_ref, a_ref, s_ref, wr_ref, xo_ref, h_ref, na_ref, n_experts):
    xn = x_ref[...] + gate_ref[...] * y
    xo_ref[...] = xn
    h = _norm_mod(xn, a_ref[...], s_ref[...]).astype(_MXU)
    h_ref[...] = h
    logits = _dot(h, wr_ref[...])
    lane = lax.broadcasted_iota(jnp.int32, logits.shape, 1)
    logits = jnp.where(lane < n_experts, logits, -jnp.inf)
    e = jnp.exp(logits - jnp.max(logits, axis=-1, keepdims=True))
    na_ref[...] = -(e / jnp.sum(e, axis=-1, keepdims=True))


def _shift_rows(v, up):
    r, w = v.shape
    v3 = v.reshape(r // SUBLANES, SUBLANES, w)
    sub = lax.broadcasted_iota(jnp.int32, (1, SUBLANES, 1), 1)
    if up:
        rot = pltpu.roll(v3, SUBLANES - 1, 1)
        out = jnp.where(sub == SUBLANES - 1, jnp.concatenate([rot[1:], rot[-1:]], axis=0), rot)
    else:
        rot = pltpu.roll(v3, 1, 1)
        out = jnp.where(sub == 0, jnp.concatenate([rot[:1], rot[:-1]], axis=0), rot)
    return out.reshape(r, w)


def _post_even_kernel(za_ref, of_ref, ob_ref, cw_ref, cb_ref, ng_ref, w_ref, x_ref, gate_ref, a_ref, s_ref, wr_ref,
                      xo_ref, h_ref, na_ref, y_scr, *, row_len, n_experts):
    tm = za_ref.shape[0]
    scw = cw_ref.shape[1]
    val = ng_ref.shape[1]
    dv = val // GLA_HEADS
    bg = za_ref[:, 0:scw].astype(_F32)
    cx = za_ref[:, scw:2 * scw].astype(_F32) * za_ref[:, 2 * scw:3 * scw].astype(_F32)
    r = lax.broadcasted_iota(jnp.int32, (tm, 1), 0) % row_len
    left = jnp.where(r == 0, 0.0, _shift_rows(cx, False))
    right = jnp.where(r == row_len - 1, 0.0, _shift_rows(cx, True))
    conv = cb_ref[...] + cw_ref[0:1] * left + cw_ref[1:2] * cx + cw_ref[2:3] * right
    acc = _dot(bg * conv, w_ref[0:scw, :])
    for h in range(GLA_HEADS):
        lanes = slice(h * dv, (h + 1) * dv)
        o = of_ref[:, lanes].astype(_F32) + ob_ref[:, lanes].astype(_F32)
        o = o * lax.rsqrt(jnp.mean(o * o, axis=-1, keepdims=True) + NORM_EPS)
        gate = za_ref[:, 3 * scw + h * dv:3 * scw + (h + 1) * dv].astype(_F32)
        y_scr[:, lanes] = o * ng_ref[:, lanes] * _silu(gate)
    acc = acc + _dot(y_scr[...], w_ref[scw:scw + val, :])
    _residual_ffn_prenorm(acc, x_ref, gate_ref, a_ref, s_ref, wr_ref, xo_ref, h_ref, na_ref, n_experts)


def _post_odd_kernel(zc_ref, hf_ref, hb_ref, lng_ref, sgw_ref, sgb_ref, w_ref, x_ref, gate_ref, a_ref, s_ref, wr_ref,
                     xo_ref, h_ref, na_ref, y_scr, *, n_experts):
    tm = zc_ref.shape[0]
    width = hf_ref.shape[1]
    gdim = width // SG_GROUPS
    y_rec = (hf_ref[...].astype(_F32) + hb_ref[...].astype(_F32)) * _gelu(zc_ref[:, 0:width].astype(_F32))
    acc = _dot(y_rec, w_ref[0:width, :])
    gv = _gelu(zc_ref[:, 2 * width:3 * width].astype(_F32))
    gc = gv - jnp.mean(gv, axis=-1, keepdims=True)
    vb = gc * lax.rsqrt(jnp.mean(gc * gc, axis=-1, keepdims=True) + NORM_EPS) * lng_ref[...]
    for n in range(tm // SG_CHUNK):
        rows = slice(n * SG_CHUNK, (n + 1) * SG_CHUNK)
        for g in range(SG_GROUPS):
            lanes = slice(g * gdim, (g + 1) * gdim)
            mixed = _dot(sgw_ref[g], vb[rows, lanes]) + sgb_ref[g]
            u = zc_ref[rows, width + g * gdim:width + (g + 1) * gdim].astype(_F32)
            y_scr[rows, lanes] = _gelu(u) * mixed
    acc = acc + _dot(y_scr[...], w_ref[width:2 * width, :])
    _residual_ffn_prenorm(acc, x_ref, gate_ref, a_ref, s_ref, wr_ref, xo_ref, h_ref, na_ref, n_experts)


def _post(body, z, z_width, seq_in, consts, w_out, x, gate, a, s, wr, n_experts):
    bsz, t, d = x.shape
    tm = _row_tile(t, PROJ_ROWS)
    row = lambda b, i: (b, i, 0)
    vec = pl.BlockSpec((None, 1, d), lambda b, i: (b, 0, 0))
    full = lambda arr: pl.BlockSpec(arr.shape, lambda b, i: (0,) * arr.ndim)
    width = seq_in[0].shape[-1]
    in_specs = [pl.BlockSpec((None, tm, z_width), row)]
    in_specs += [pl.BlockSpec((None, tm, width), row) for _ in seq_in]
    in_specs += [full(v) for v in consts] + [full(w_out), pl.BlockSpec((None, tm, d), row), vec, vec, vec, full(wr)]
    return pl.pallas_call(
        functools.partial(body, n_experts=n_experts),
        grid=(bsz, t // tm),
        in_specs=in_specs,
        out_specs=[
            pl.BlockSpec((None, tm, d), row),
            pl.BlockSpec((None, tm, d), row),
            pl.BlockSpec((None, tm, LANES), row),
        ],
        out_shape=[
            jax.ShapeDtypeStruct((bsz, t, d), _F32),
            jax.ShapeDtypeStruct((bsz, t, d), _MXU),
            jax.ShapeDtypeStruct((bsz, t, LANES), _F32),
        ],
        scratch_shapes=[pltpu.VMEM((tm, width), _F32)],
        compiler_params=_cparams("parallel", "parallel"),
        name="post_proj",
    )(z, *seq_in, *consts, w_out, x, gate, a, s, wr)


def _moe_ffn_kernel(*refs, with_short):
    if with_short:
        x_ref, g_ref, xs_ref, gs_ref, wg_ref, wu_ref, wd_ref, o_ref, os_ref, wg_scr, wu_scr, wd_scr = refs
    else:
        x_ref, g_ref, wg_ref, wu_ref, wd_ref, o_ref, wg_scr, wu_scr, wd_scr = refs
    b = pl.program_id(1)

    @pl.when(b == 0)
    def _():
        wg_scr[...] = wg_ref[...].astype(_MXU)
        wu_scr[...] = wu_ref[...].astype(_MXU)
        wd_scr[...] = wd_ref[...].astype(_MXU)

    def ffn(xr, gr, outr):
        bb, cap, d = xr.shape
        x = xr[...].reshape(bb * cap, d)
        gate = _dot(x, wg_scr[...])
        up = _dot(x, wu_scr[...])
        out = _dot(_silu(gate) * up, wd_scr[...]) * gr[...].reshape(bb * cap, 1)
        outr[...] = out.reshape(bb, cap, d).astype(outr.dtype)

    if with_short:
        last = pl.num_programs(1) - 1
        pl.when(b < last)(lambda: ffn(x_ref, g_ref, o_ref))
        pl.when(b == last)(lambda: ffn(xs_ref, gs_ref, os_ref))
    else:
        ffn(x_ref, g_ref, o_ref)


def _moe_ffn(xg, g, short, wg, wu, wd, layer):
    bsz, ne, cap, d = xg.shape
    f = wg.shape[-1]
    bb = max(1, min(bsz, MOE_ROWS // cap))
    assert bsz % bb == 0
    nb = bsz // bb
    wsel = lambda e, b: (layer, e, 0, 0)
    w_specs = [pl.BlockSpec((None, None, d, f), wsel), pl.BlockSpec((None, None, d, f), wsel),
               pl.BlockSpec((None, None, f, d), wsel)]
    out_shape = [jax.ShapeDtypeStruct((bsz, ne, cap, d), _MXU)]
    if short is None:
        tok = lambda e, b: (b, e, 0, 0)
        steps = nb
        in_specs = [pl.BlockSpec((bb, None, cap, d), tok), pl.BlockSpec((bb, None, cap, 1), tok)] + w_specs
        out_specs = [pl.BlockSpec((bb, None, cap, d), tok)]
        args = (xg, g, wg, wu, wd)
    else:
        xs, gs = short
        bs, _, caps, _ = xs.shape
        assert bs * caps <= MOE_ROWS
        tok = lambda e, b: (jnp.minimum(b, nb - 1), e, 0, 0)
        tok_s = lambda e, b: (0, e, 0, 0)
        steps = nb + 1
        in_specs = [pl.BlockSpec((bb, None, cap, d), tok), pl.BlockSpec((bb, None, cap, 1), tok),
                    pl.BlockSpec((bs, None, caps, d), tok_s), pl.BlockSpec((bs, None, caps, 1), tok_s)] + w_specs
        out_specs = [pl.BlockSpec((bb, None, cap, d), tok), pl.BlockSpec((bs, None, caps, d), tok_s)]
        out_shape.append(jax.ShapeDtypeStruct((bs, ne, caps, d), _MXU))
        args = (xg, g, xs, gs, wg, wu, wd)
    return pl.pallas_call(
        functools.partial(_moe_ffn_kernel, with_short=short is not None),
        grid=(ne, steps),
        in_specs=in_specs,
        out_specs=out_specs,
        out_shape=out_shape,
        scratch_shapes=[pltpu.VMEM((d, f), _MXU), pltpu.VMEM((d, f), _MXU), pltpu.VMEM((f, d), _MXU)],
        compiler_params=pltpu.CompilerParams(dimension_semantics=("parallel", "arbitrary"),
                                             vmem_limit_bytes=MOE_VMEM_LIMIT),
        name="moe_ffn",
    )(*args)


def _dispatch_kernel(idx_ref, h_ref, o_ref):
    p, t = idx_ref.shape[0], h_ref.shape[0]
    onehot = jnp.where(lax.broadcasted_iota(jnp.int32, (p, t), 1) == idx_ref[...], 1.0, 0.0)
    o_ref[...] = _dot(onehot, h_ref[...]).astype(o_ref.dtype)


def _dispatch(h, idx):
    bsz, t, d = h.shape
    p = idx.shape[1]
    return pl.pallas_call(
        _dispatch_kernel,
        grid=(bsz,),
        in_specs=[pl.BlockSpec((None, p, 1), lambda b: (b, 0, 0)), pl.BlockSpec((None, t, d), lambda b: (b, 0, 0))],
        out_specs=pl.BlockSpec((None, p, d), lambda b: (b, 0, 0)),
        out_shape=jax.ShapeDtypeStruct((bsz, p, d), h.dtype),
        compiler_params=_cparams("parallel"),
        name="moe_dispatch",
    )(idx.reshape(bsz, p, 1), h)


def _combine_kernel(lo_ref, hi_ref, tok_ref, y_ref, x_ref, gate_ref, fg_ref, o_ref, acc_ref, *, final_norm):
    b = pl.program_id(0)
    j = pl.program_id(1)
    tm = acc_ref.shape[0]
    ch = tok_ref.shape[-1]
    acc_ref[...] = jnp.zeros_like(acc_ref)
    rows = j * tm + lax.broadcasted_iota(jnp.int32, (tm, ch), 0)

    def body(c, carry):
        onehot = jnp.where(rows == tok_ref[c], 1.0, 0.0)
        acc_ref[...] += _dot(onehot, y_ref[pl.ds(pl.multiple_of(c * ch, ch), ch), :])
        return carry

    lax.fori_loop(lo_ref[b, j], hi_ref[b, j] + 1, body, 0)
    xn = x_ref[...] + gate_ref[...] * acc_ref[...]
    if final_norm:
        ms = jnp.mean(xn * xn, axis=-1, keepdims=True)
        xn = xn * lax.rsqrt(ms + NORM_EPS) * fg_ref[...]
    o_ref[...] = xn


def _combine(ys, toks, x, gate, final_g):
    bsz, t, d = x.shape
    p = ys.shape[1]
    tm = _row_tile(t, PROJ_ROWS)
    ch = _row_tile(p)
    nt, nc = t // tm, p // ch
    bounds = jnp.arange(nt + 1, dtype=jnp.int32) * tm
    cut = jnp.sum((toks[:, :, None] < bounds[None, None, :]).astype(jnp.int32), axis=1)
    c_lo = jnp.minimum(cut[:, :-1] // ch, nc - 1)
    c_hi = jnp.maximum((cut[:, 1:] - 1) // ch, c_lo)
    fg = jnp.ones((1, d), _F32) if final_g is None else final_g.reshape(1, d)
    grid_spec = pltpu.PrefetchScalarGridSpec(
        num_scalar_prefetch=2,
        grid=(bsz, nt),
        in_specs=[
            pl.BlockSpec((None, nc, 1, ch), lambda b, j, lo, hi: (b, 0, 0, 0)),
            pl.BlockSpec((None, p, d), lambda b, j, lo, hi: (b, 0, 0)),
            pl.BlockSpec((None, tm, d), lambda b, j, lo, hi: (b, j, 0)),
            pl.BlockSpec((None, 1, d), lambda b, j, lo, hi: (b, 0, 0)),
            pl.BlockSpec((1, d), lambda b, j, lo, hi: (0, 0)),
        ],
        out_specs=pl.BlockSpec((None, tm, d), lambda b, j, lo, hi: (b, j, 0)),
        scratch_shapes=[pltpu.VMEM((tm, d), _F32)],
    )
    return pl.pallas_call(
        functools.partial(_combine_kernel, final_norm=final_g is not None),
        grid_spec=grid_spec,
        out_shape=jax.ShapeDtypeStruct((bsz, t, d), _F32),
        compiler_params=pltpu.CompilerParams(dimension_semantics=("parallel", "arbitrary"),
                                             vmem_limit_bytes=MOE_VMEM_LIMIT),
        name="moe_combine",
    )(c_lo, c_hi, toks.reshape(bsz, nc, 1, ch), ys, x, gate, fg)


def _route(h, neg_aff):
    bsz, t, d = h.shape
    ne = neg_aff.shape[-1]
    cap = EC_CAPACITY * t // ne
    keys = jnp.transpose(neg_aff, (1, 0, 2)).reshape(t, bsz * ne)
    token = lax.broadcasted_iota(jnp.int32, (t, bsz * ne), 0)
    keys, token = lax.sort((keys, token), dimension=0, num_keys=1, is_stable=True)
    g = jnp.transpose((-keys[:cap]).reshape(cap, bsz, ne), (1, 2, 0))
    idx = jnp.transpose(token[:cap].reshape(cap, bsz, ne), (1, 2, 0))
    if t <= SHORT_SEQ:
        xg = _dispatch(h, idx.reshape(bsz, ne * cap)).reshape(bsz, ne, cap, d)
    else:
        xg = jax.vmap(lambda hb, ib: hb[ib])(h, idx)
    return xg, g[..., None], idx


def _unroute(y, idx, x, gate, final_g):
    bsz, ne, cap, d = y.shape
    slots = jnp.broadcast_to(jnp.arange(ne * cap, dtype=jnp.int32), (bsz, ne * cap))
    toks, perm = lax.sort_key_val(idx.reshape(bsz, ne * cap).astype(jnp.int32), slots, dimension=-1)
    ys = jax.vmap(lambda yb, pb: yb[pb])(y.reshape(bsz, ne * cap, d), perm)
    return _combine(ys, toks, x, gate, final_g)


def _pad_to(w, axis, size):
    pad = [(0, 0)] * w.ndim
    pad[axis] = (0, size - w.shape[axis])
    return jnp.pad(w, pad)


def kernel(x, c, ctx, c_ctx, ada_w, ada_b, norm_mix_g, norm_ffn_g, norm_final_g, ev_w_in, ev_conv_w, ev_conv_b, ev_decay_up, ev_decay_b, ev_norm_g, ev_w_out, od_w_in, od_conv_w, od_conv_b, od_gate_a_w, od_gate_a_b, od_gate_x_w, od_gate_x_b, od_lambda, od_sg_norm_g, od_sg_w, od_sg_b, od_w_out, moe_router, moe_w_gate, moe_w_up, moe_w_down):
    depth = ada_w.shape[0]
    bsz, seq, d = x.shape
    ne = moe_router.shape[-1]
    x_lat, x_ctx = x, ctx
    cond = _pad_to(jnp.concatenate([c, c_ctx[None, :]], axis=0), 0, -(-(bsz + 1) // SUBLANES) * SUBLANES)
    ada = _adaln(cond, ada_w, ada_b.reshape(depth, 1, N_MOD * d))

    def vecs(m):
        return [v.reshape(bsz, 1, d) for v in m]

    for l in range(depth):
        last = l == depth - 1
        even = l % 2 == 0
        i = l // 2
        mod = [ada[l, :bsz, k * d:(k + 1) * d] for k in range(N_MOD)]
        mod_c = [jnp.broadcast_to(ada[l, bsz:bsz + 1, k * d:(k + 1) * d], (bsz, d)) for k in range(N_MOD)]
        wr = _pad_to(moe_router[l], 1, LANES).astype(_MXU)

        if even:
            w = ev_w_in[i]
            scw = ev_conv_w.shape[-1]
            key = ev_decay_up.shape[-1]
            val = ev_norm_g.shape[-1]
            o_q = 3 * scw
            o_g = o_q + key
            o_k = o_g + val
            o_v = o_k + key
            o_lr = o_v + val
            w_in = jnp.concatenate([w[:, :o_q], w[:, o_g:o_k], w[:, o_q:o_g], w[:, o_k:o_lr]], axis=1).astype(_MXU)
            w_lr = _pad_to(w[:, o_lr:], 1, LANES).astype(_MXU)
            up = jnp.zeros((LANES, 2 * key), _F32)
            up = up.at[:GLA_RANK, :key].set(ev_decay_up[i, 0]).at[GLA_RANK:2 * GLA_RANK, key:].set(ev_decay_up[i, 1])
            decay = (w_lr, up.astype(_MXU), ev_decay_b[i].reshape(1, 2 * key))
            w_out = ev_w_out[i].astype(_MXU)
            conv_w = _pad_to(ev_conv_w[i], 0, SUBLANES)
            consts = (conv_w, ev_conv_b[i].reshape(1, scw), ev_norm_g[i].reshape(1, val))
            epi_width = 3 * scw + val
            assert (3 * scw + val) % (2 * key + val) == 0
            qkv_block = epi_width // (2 * key + val)
            dk, dv = key // GLA_HEADS, val // GLA_HEADS
            scale = dk ** -0.5
            zero_st = jnp.zeros((bsz, GLA_HEADS // 2, dv, 2 * dk), _F32)

            def mixer(xs, m, s0, row_len):
                sh, sc = vecs(m[:2])
                z, la = _pre(xs, norm_mix_g[l] * (1.0 + sc), sh, w_in, decay)
                o_f, o_b, s_f, s_b = _gla(z, la, s0[0], s0[1], qkv_block, scale)
                body = functools.partial(_post_even_kernel, row_len=row_len)
                return (body, z, epi_width, (o_f, o_b), consts, w_out), (s_f, s_b)

            ctx_args, state = mixer(x_ctx, mod_c, (zero_st, zero_st), x_ctx.shape[1])
            lat_args, _ = mixer(x_lat, mod, state, GRID_W)
        else:
            w = od_w_in[i]
            width = od_conv_w.shape[-1]
            hd = width // LRU_HEADS
            w_in = jnp.concatenate([w[:, width:], w[:, :width]], axis=1).astype(_MXU)
            w_out = od_w_out[i].astype(_MXU)
            conv_w = _pad_to(od_conv_w[i], 0, SUBLANES)
            conv_b = od_conv_b[i].reshape(1, width)
            wgate = (0.5 * jnp.concatenate([od_gate_a_w[i], od_gate_x_w[i]], axis=-1)).astype(_MXU)
            ba = 0.5 * od_gate_a_b[i].reshape(2, 1, width)
            bx = 0.5 * od_gate_x_b[i].reshape(2, 1, width)
            sp = (-0.25 * LRU_C * jax.nn.softplus(-od_lambda[i])).reshape(2, 1, width)
            sgb = jnp.broadcast_to(od_sg_b[i][:, :, None], (SG_GROUPS, SG_CHUNK, width // SG_GROUPS))
            consts = (od_sg_norm_g[i].reshape(1, width), od_sg_w[i].astype(_MXU), sgb)
            zero_st = jnp.zeros((bsz, 1, width), _F32)

            def mixer(xs, m, s0, row_len):
                sh, sc = vecs(m[:2])
                (z,) = _pre(xs, norm_mix_g[l] * (1.0 + sc), sh, w_in)
                h_f, h_b, c_f, c_b = _lru(z, 3, conv_w, conv_b, wgate, ba, bx, sp, s0[0], s0[1])
                return (_post_odd_kernel, z, 3 * width, (h_f, h_b), consts, w_out), (c_f, c_b)

            ctx_args, state = mixer(x_ctx, mod_c, (zero_st, zero_st), None)
            lat_args, _ = mixer(x_lat, mod, state, None)

        def post(args, xs, m):
            gate1, sh, sc, gate2 = vecs(m[2:])
            xs, h, neg_aff = _post(*args, xs, gate1, norm_ffn_g[l] * (1.0 + sc), sh, wr, ne)
            return (xs, gate2) + _route(h, neg_aff[..., :ne])

        x_lat, gate_lat, xg_lat, g_lat, idx_lat = post(lat_args, x_lat, mod)
        weights = (moe_w_gate, moe_w_up, moe_w_down, l)
        if last:
            (y_lat,) = _moe_ffn(xg_lat, g_lat, None, *weights)
        else:
            x_ctx, gate_ctx, xg_ctx, g_ctx, idx_ctx = post(ctx_args, x_ctx, mod_c)
            y_lat, y_ctx = _moe_ffn(xg_lat, g_lat, (xg_ctx, g_ctx), *weights)
            x_ctx = _unroute(y_ctx, idx_ctx, x_ctx, gate_ctx, None)
        x_lat = _unroute(y_lat, idx_lat, x_lat, gate_lat, norm_final_g if last else None)
    return x_lat
```
